```python
import math
import jax, jax.numpy as jnp
from jax import lax
import numpy as np

D_MODEL = 1024
BATCH = 4
SEQ = 8192
DEPTH = 4

f32 = jnp.float32
CTX_LEN = 256
GRID_W = 64
EPS = 1e-6
POS_BASE = 10000.0
N_MOD = 6
HG_HEADS = 4
HG_DK = 128
HG_DV = 128
HG_WIDTH = HG_HEADS * HG_DK
GLA_HEADS = 4
GLA_DK = 64
GLA_DV = 128
GLA_RANK = 16
GLA_TAU = 16.0
SSD_HEADS = 8
SSD_HEADDIM = 64
SSD_INNER = SSD_HEADS * SSD_HEADDIM
SSD_GROUPS = 2
SSD_RPG = SSD_HEADS // SSD_GROUPS
SSD_STATE = 128
SSD_CONV = 5
SSD_CONV_CH = SSD_INNER + 2 * SSD_GROUPS * SSD_STATE
CHUNK_VEC = 32
CHUNK_SSD = 64
N_BRANCH = 3
N_EXPERTS = 64
TOP_K = 8
N_GROUPS = 8
TOPK_GROUPS = 4
EXPERT_FF = 256
SHARED_FF = 256
ROUTE_SCALE = 2.5
MOE_BLOCK = 256
NEG_MASK = -1e4

IN_SIZES = (HG_WIDTH, HG_WIDTH, HG_WIDTH, HG_HEADS * HG_DV, HG_HEADS * HG_DV,
            GLA_HEADS * GLA_DK, GLA_HEADS * GLA_DK, GLA_HEADS * GLA_DV, GLA_HEADS * GLA_DV,
            GLA_RANK, GLA_RANK,
            SSD_INNER, SSD_INNER, SSD_GROUPS * SSD_STATE, SSD_GROUPS * SSD_STATE,
            SSD_HEADS, SSD_HEADS,
            N_BRANCH * D_MODEL)
IN_TOTAL = sum(IN_SIZES)
IN_OFFSETS = tuple(int(v) for v in np.cumsum(IN_SIZES)[:-1])

kernel_name = "hybrid_hgrn2_gla_ssd_moe_dit"


def rms_norm(x, w):
    xf = x.astype(f32)
    y = xf * lax.rsqrt(jnp.mean(xf * xf, axis=-1, keepdims=True) + EPS)
    return (y * w.astype(f32)).astype(x.dtype)


def grid_sincos(rows, cols, dim):
    quarter = dim // 4
    omega = 1.0 / (POS_BASE ** (jnp.arange(quarter, dtype=f32) / quarter))

    def axis_embed(n):
        ang = jnp.arange(n, dtype=f32)[:, None] * omega
        return jnp.concatenate([jnp.sin(ang), jnp.cos(ang)], axis=-1)

    er, ec = axis_embed(rows), axis_embed(cols)
    pe = jnp.concatenate([jnp.broadcast_to(er[:, None], (rows, cols, dim // 2)),
                          jnp.broadcast_to(ec[None], (rows, cols, dim // 2))], axis=-1)
    return pe.reshape(rows * cols, dim)


def to_heads(t, n):
    Bn, L, _ = t.shape
    return t.reshape(Bn, L, n, -1).transpose(0, 2, 1, 3)


def chunk_gla(q, k, v, log_g, s0, with_out):
    Bn, H, L, _ = q.shape
    n = L // CHUNK_VEC

    def blk(t):
        t = t.astype(f32)
        return jnp.moveaxis(t.reshape(Bn, H, n, CHUNK_VEC, t.shape[-1]), 2, 0)

    causal = jnp.tril(jnp.ones((CHUNK_VEC, CHUNK_VEC), bool))[:, :, None]

    def step(S, inp):
        qb, kb, vb, gb = inp
        b = jnp.cumsum(gb, axis=2)
        b_end = b[:, :, -1:]
        S_new = S * jnp.exp(b_end[:, :, 0])[..., None] + jnp.einsum(
            'bhsk,bhsv->bhkv', kb * jnp.exp(b_end - b), vb)
        if not with_out:
            return S_new, None
        diff = b[:, :, :, None] - b[:, :, None]
        decay = jnp.where(causal, jnp.exp(jnp.where(causal, diff, 0.0)), 0.0)
        att = jnp.einsum('bhtk,bhsk,bhtsk->bhts', qb, kb, decay)
        o = jnp.einsum('bhts,bhsv->bhtv', att, vb) + jnp.einsum('bhtk,bhkv->bhtv', qb * jnp.exp(b), S)
        return S_new, o

    S, ys = lax.scan(step, s0.astype(f32), (blk(q), blk(k), blk(v), blk(log_g)))
    o = None if ys is None else jnp.moveaxis(ys, 0, 2).reshape(Bn, H, L, v.shape[-1])
    return o, S


def chunk_ssd(x, dt, bm, cm, a, s0, with_out):
    Bn, L = x.shape[:2]
    n = L // CHUNK_SSD
    a = a.astype(f32)

    def blk(t):
        t = t.astype(f32)
        return jnp.moveaxis(t.reshape(Bn, n, CHUNK_SSD, *t.shape[2:]), 1, 0)

    causal = jnp.tril(jnp.ones((CHUNK_SSD, CHUNK_SSD), bool))[None, :, :, None, None]

    def step(S, inp):
        xb, dtb, bb, cb = inp
        cum = jnp.cumsum(dtb * a, axis=1)
        xdt = xb * dtb[..., None]
        S_new = S * jnp.exp(cum[:, -1])[..., None, None] + jnp.einsum(
            'bsgn,bsgr,bsgrp->bgrnp', bb, jnp.exp(cum[:, -1:] - cum), xdt)
        if not with_out:
            return S_new, None
        diff = cum[:, :, None] - cum[:, None]
        seg = jnp.where(causal, jnp.exp(jnp.where(causal, diff, 0.0)), 0.0)
        scores = jnp.einsum('btgn,bsgn->btsg', cb, bb)
        y = jnp.einsum('btsg,btsgr,bsgrp->btgrp', scores, seg, xdt) + jnp.einsum(
            'btgn,bgrnp->btgrp', cb, S) * jnp.exp(cum)[..., None]
        return S_new, y

    S, ys = lax.scan(step, s0.astype(f32), (blk(x), blk(dt), blk(bm), blk(cm)))
    y = None if ys is None else jnp.moveaxis(ys, 0, 1).reshape(x.shape)
    return y, S


def bidirectional(core, ctx_f, ctx_b, lat_f, lat_b, s0, seq_axis, ctx_out):
    oc_f, sc_f = core(*ctx_f, s0, ctx_out)
    oc_b, sc_b = core(*ctx_b, s0, ctx_out)
    ox_f, _ = core(*lat_f, sc_f, True)
    ox_b, _ = core(*lat_b, sc_b, True)
    o_lat = ox_f + jnp.flip(ox_b, seq_axis)
    o_ctx = oc_f + jnp.flip(oc_b, seq_axis) if ctx_out else None
    return o_ctx, o_lat


def gated_head_norm(o, w, gate):
    Bn, H, L, d = o.shape
    o = rms_norm(o.transpose(0, 2, 1, 3).astype(gate.dtype), w).reshape(Bn, L, H * d)
    return o * jax.nn.silu(gate)


def hgrn2_lower_bounds(logits):
    p = jax.nn.softmax(logits.astype(f32), axis=1)
    return jnp.cumsum(p, axis=1) - p[:, :1]


def hgrn2_gates(z, lb):
    z = z.astype(f32)
    lb = lb.astype(f32).reshape(HG_HEADS, 1, HG_DK)
    pos = lb > 0.0
    safe_lb = jnp.where(pos, lb, 1.0)
    ls = jax.nn.log_sigmoid(z)
    log_f = jnp.where(pos, jnp.logaddexp(jnp.log(safe_lb), jnp.log1p(-lb) + ls), ls)
    k = (1.0 - lb) * jax.nn.sigmoid(-z)
    return k, log_f


def hgrn2_branch(pc, px, lb_f, lb_b, norm_w, ctx_out):
    def prep(p):
        q_raw, ff_raw, fb_raw, v_raw, g = p
        q, v = to_heads(q_raw, HG_HEADS), to_heads(v_raw, HG_HEADS)
        kf, lgf = hgrn2_gates(to_heads(ff_raw, HG_HEADS), lb_f)
        kb, lgb = hgrn2_gates(to_heads(fb_raw, HG_HEADS), lb_b)
        fwd = (q, kf, v, lgf)
        bwd = tuple(jnp.flip(t, 2) for t in (q, kb, v, lgb))
        return fwd, bwd, g

    cf, cb, gc = prep(pc)
    xf, xb, gx = prep(px)
    s0 = jnp.zeros((pc[0].shape[0], HG_HEADS, HG_DK, HG_DV), f32)
    oc, ox = bidirectional(chunk_gla, cf, cb, xf, xb, s0, 2, ctx_out)
    out_c = gated_head_norm(oc, norm_w, gc) if ctx_out else None
    return out_c, gated_head_norm(ox, norm_w, gx)


def gla_branch(pc, px, gate_w, gate_b, norm_w, ctx_out):
    def prep(p):
        q_raw, k_raw, v_raw, r, lr_f, lr_b = p
        q = to_heads(q_raw, GLA_HEADS) * (GLA_DK ** -0.5)
        k, v = to_heads(k_raw, GLA_HEADS), to_heads(v_raw, GLA_HEADS)

        def decay(lr, d):
            return to_heads(jax.nn.log_sigmoid((lr @ gate_w[d] + gate_b[d]).astype(f32)) / GLA_TAU, GLA_HEADS)

        gf, gb = decay(lr_f, 0), decay(lr_b, 1)
        fwd = (q, k, v, gf)
        bwd = tuple(jnp.flip(t, 2) for t in (q, k, v, gb))
        return fwd, bwd, r

    cf, cb, rc = prep(pc)
    xf, xb, rx = prep(px)
    s0 = jnp.zeros((pc[0].shape[0], GLA_HEADS, GLA_DK, GLA_DV), f32)
    oc, ox = bidirectional(chunk_gla, cf, cb, xf, xb, s0, 2, ctx_out)
    out_c = gated_head_norm(oc, norm_w, rc) if ctx_out else None
    return out_c, gated_head_norm(ox, norm_w, rx)


def dw_conv(x, w, b):
    C = x.shape[-1]
    K = w.shape[0]
    y = lax.conv_general_dilated(x, w[:, None, :].astype(x.dtype), window_strides=(1,),
                                 padding=[(K // 2, K // 2)], dimension_numbers=('NWC', 'WIO', 'NWC'),
                                 feature_group_count=C)
    return y + b


def ssd_branch(pc, px, conv_w, conv_b, a_log, dt_bias, d_skip, norm_w, ctx_out):
    a = (-jnp.exp(a_log.astype(f32))).reshape(2, SSD_GROUPS, SSD_RPG)

    def prep(p):
        z, xr, br, cr, dtf, dtb = p
        xbc = jax.nn.silu(dw_conv(jnp.concatenate([xr, br, cr], axis=-1), conv_w, conv_b))
        xs, bm, cm = jnp.split(xbc, [SSD_INNER, SSD_INNER + SSD_GROUPS * SSD_STATE], axis=-1)
        Bn, L, _ = xs.shape
        xs = xs.reshape(Bn, L, SSD_GROUPS, SSD_RPG, SSD_HEADDIM)
        bm = bm.reshape(Bn, L, SSD_GROUPS, SSD_STATE)
        cm = cm.reshape(Bn, L, SSD_GROUPS, SSD_STATE)

        def dt(raw, d):
            return jax.nn.softplus(raw.astype(f32) + dt_bias[d].astype(f32)).reshape(Bn, L, SSD_GROUPS, SSD_RPG)

        fwd = (xs, dt(dtf, 0), bm, cm, a[0])
        bwd = (jnp.flip(xs, 1), jnp.flip(dt(dtb, 1), 1), jnp.flip(bm, 1), jnp.flip(cm, 1), a[1])
        return fwd, bwd, xs, z

    def finish(y, xs, z):
        Bn, L = z.shape[:2]
        y = y.astype(z.dtype) + d_skip.reshape(SSD_GROUPS, SSD_RPG, 1) * xs
        yz = (y.reshape(Bn, L, SSD_INNER) * jax.nn.silu(z)).reshape(Bn, L, SSD_GROUPS, -1)
        return rms_norm(yz, norm_w.reshape(SSD_GROUPS, -1)).reshape(Bn, L, SSD_INNER)

    cf, cb, xs_c, zc = prep(pc)
    xf, xb, xs_x, zx = prep(px)
    s0 = jnp.zeros((pc[0].shape[0], SSD_GROUPS, SSD_RPG, SSD_STATE, SSD_HEADDIM), f32)
    oc, ox = bidirectional(chunk_ssd, cf, cb, xf, xb, s0, 1, ctx_out)
    out_c = finish(oc, xs_c, zc) if ctx_out else None
    return out_c, finish(ox, xs_x, zx)


def merge_branches(o_hg, o_gla, o_ssd, gate_raw, w_hg, w_gla, w_ssd, w_out):
    g_hg, g_gla, g_ssd = jnp.split(jax.nn.sigmoid(gate_raw), N_BRANCH, axis=-1)
    y = g_hg * (o_hg @ w_hg) + g_gla * (o_gla @ w_gla) + g_ssd * (o_ssd @ w_ssd)
    return y @ w_out


def moe_ffn(h, router_w, router_bias, w_gate, w_up, w_down, s_gate, s_up, s_down):
    T, D = h.shape
    scores = jax.nn.sigmoid(h.astype(f32) @ router_w.astype(f32))
    biased = scores + router_bias.astype(f32)
    gscore = lax.top_k(biased.reshape(T, N_GROUPS, N_EXPERTS // N_GROUPS), 2)[0].sum(-1)
    _, gidx = lax.top_k(gscore, TOPK_GROUPS)
    gmask = jnp.any(gidx[..., None] == jnp.arange(N_GROUPS), axis=-2)
    emask = jnp.repeat(gmask, N_EXPERTS // N_GROUPS, axis=-1)
    _, idx = lax.top_k(jnp.where(emask, biased, NEG_MASK), TOP_K)
    wts = jnp.take_along_axis(scores, idx, axis=-1)
    wts = wts / jnp.sum(wts, axis=-1, keepdims=True) * ROUTE_SCALE

    n_assign = T * TOP_K
    flat_e = idx.reshape(-1)
    flat_t = jnp.repeat(jnp.arange(T, dtype=jnp.int32), TOP_K)
    order = jnp.argsort(flat_e)
    se, st, sw = flat_e[order], flat_t[order], wts.reshape(-1)[order]
    counts = jnp.bincount(flat_e, length=N_EXPERTS)
    padded = (counts + MOE_BLOCK - 1) // MOE_BLOCK * MOE_BLOCK
    pad_end = jnp.cumsum(padded)
    pad_start = pad_end - padded
    start = jnp.cumsum(counts) - counts
    dest = pad_start[se] + jnp.arange(n_assign, dtype=jnp.int32) - start[se]
    n_blocks = (n_assign + MOE_BLOCK - 1) // MOE_BLOCK + N_EXPERTS
    npad = n_blocks * MOE_BLOCK
    buf_t = jnp.zeros((npad,), jnp.int32).at[dest].set(st)
    buf_w = jnp.zeros((npad,), f32).at[dest].set(sw)
    block_e = jnp.minimum(jnp.searchsorted(pad_end, jnp.arange(n_blocks, dtype=jnp.int32) * MOE_BLOCK,
                                           side='right'), N_EXPERTS - 1)

    def body(out, blk):
        tok, wt, e = blk
        xb = h[tok]
        y = (jax.nn.silu(xb @ w_gate[e]) * (xb @ w_up[e])) @ w_down[e]
        return out.at[tok].add(y * wt.astype(h.dtype)[:, None]), None

    routed, _ = lax.scan(body, jnp.zeros((T, D), h.dtype),
                         (buf_t.reshape(n_blocks, MOE_BLOCK), buf_w.reshape(n_blocks, MOE_BLOCK), block_e))
    shared = (jax.nn.silu(h @ s_gate) * (h @ s_up)) @ s_down
    return routed + shared


def setup_inputs(seed: int = 0) -> dict:
    key = jax.random.key(seed)
    ks = iter(jax.random.split(key, 48))

    def nrm(shape, scale):
        return scale * jax.random.normal(next(ks), shape, f32)

    def gain(shape):
        return 1.0 + 0.05 * jax.random.normal(next(ks), shape, f32)

    D = D_MODEL
    dt = jnp.exp(jax.random.uniform(next(ks), (2, DEPTH, SSD_HEADS), f32, math.log(1e-3), math.log(1e-1)))
    return {
        "x": nrm((BATCH, SEQ, D), 1.0),
        "c": nrm((BATCH, D), 1.0),
        "ctx": nrm((BATCH, CTX_LEN, D), 1.0),
        "c_ctx": nrm((D,), 1.0),
        "norm_mix_w": gain((DEPTH, D)),
        "norm_ffn_w": gain((DEPTH, D)),
        "final_norm_w": gain((D,)),
        "ada_w": nrm((DEPTH, D, N_MOD * D), 0.3 * D ** -0.5),
        "ada_b": nrm((DEPTH, N_MOD * D), 0.02),
        "w_in": nrm((DEPTH, D, IN_TOTAL), D ** -0.5),
        "hg_lb_logits": nrm((2, DEPTH, HG_WIDTH), 1.0),
        "hg_norm_w": gain((DEPTH, HG_DV)),
        "gla_gate_w": nrm((2, DEPTH, GLA_RANK, GLA_HEADS * GLA_DK), GLA_RANK ** -0.5),
        "gla_gate_b": nrm((2, DEPTH, GLA_HEADS * GLA_DK), 0.1),
        "gla_norm_w": gain((DEPTH, GLA_DV)),
        "ssd_conv_w": nrm((DEPTH, SSD_CONV, SSD_CONV_CH), SSD_CONV ** -0.5),
        "ssd_conv_b": nrm((DEPTH, SSD_CONV_CH), 0.02),
        "ssd_a_log": jnp.log(jax.random.uniform(next(ks), (2, DEPTH, SSD_HEADS), f32, 1.0, 16.0)),
        "ssd_dt_bias": dt + jnp.log(-jnp.expm1(-dt)),
        "ssd_d": gain((DEPTH, SSD_HEADS)),
        "ssd_norm_w": gain((DEPTH, SSD_INNER)),
        "w_br_hg": nrm((DEPTH, HG_HEADS * HG_DV, D), (HG_HEADS * HG_DV) ** -0.5),
        "w_br_gla": nrm((DEPTH, GLA_HEADS * GLA_DV, D), (GLA_HEADS * GLA_DV) ** -0.5),
        "w_br_ssd": nrm((DEPTH, SSD_INNER, D), SSD_INNER ** -0.5),
        "w_out": nrm((DEPTH, D, D), D ** -0.5),
        "router_w": nrm((DEPTH, D, N_EXPERTS), D ** -0.5),
        "router_bias": nrm((DEPTH, N_EXPERTS), 0.01),
        "exp_w_gate": nrm((DEPTH, N_EXPERTS, D, EXPERT_FF), D ** -0.5),
        "exp_w_up": nrm((DEPTH, N_EXPERTS, D, EXPERT_FF), D ** -0.5),
        "exp_w_down": nrm((DEPTH, N_EXPERTS, EXPERT_FF, D), EXPERT_FF ** -0.5),
        "sh_w_gate": nrm((DEPTH, D, SHARED_FF), D ** -0.5),
        "sh_w_up": nrm((DEPTH, D, SHARED_FF), D ** -0.5),
        "sh_w_down": nrm((DEPTH, SHARED_FF, D), SHARED_FF ** -0.5),
    }


def reference(x, c, ctx, c_ctx, norm_mix_w, norm_ffn_w, final_norm_w, ada_w, ada_b, w_in,
              hg_lb_logits, hg_norm_w, gla_gate_w, gla_gate_b, gla_norm_w, ssd_conv_w, ssd_conv_b,
              ssd_a_log, ssd_dt_bias, ssd_d, ssd_norm_w, w_br_hg, w_br_gla, w_br_ssd, w_out,
              router_w, router_bias, exp_w_gate, exp_w_up, exp_w_down, sh_w_gate, sh_w_up, sh_w_down):
    Bn, L, D = x.shape
    rows = L // GRID_W
    x = x + grid_sincos(rows, GRID_W, D).astype(x.dtype)[None]
    s = ctx
    lb = hgrn2_lower_bounds(hg_lb_logits)
    for l in range(DEPTH):
        ctx_out = l < DEPTH - 1
        mod_x = jnp.split(jax.nn.silu(c) @ ada_w[l] + ada_b[l], N_MOD, axis=-1)
        mod_c = jnp.split(jax.nn.silu(c_ctx) @ ada_w[l] + ada_b[l], N_MOD, axis=-1)

        hx = rms_norm(x, norm_mix_w[l]) * (1 + mod_x[1][:, None]) + mod_x[0][:, None]
        hc = rms_norm(s, norm_mix_w[l]) * (1 + mod_c[1]) + mod_c[0]
        px = jnp.split(hx @ w_in[l], IN_OFFSETS, axis=-1)
        pc = jnp.split(hc @ w_in[l], IN_OFFSETS, axis=-1)
        oc_hg, ox_hg = hgrn2_branch(pc[0:5], px[0:5], lb[0, l], lb[1, l], hg_norm_w[l], ctx_out)
        oc_gla, ox_gla = gla_branch(pc[5:11], px[5:11], gla_gate_w[:, l], gla_gate_b[:, l], gla_norm_w[l], ctx_out)
        oc_ssd, ox_ssd = ssd_branch(pc[11:17], px[11:17], ssd_conv_w[l], ssd_conv_b[l], ssd_a_log[:, l],
                                    ssd_dt_bias[:, l], ssd_d[l], ssd_norm_w[l], ctx_out)
        x = x + mod_x[2][:, None] * merge_branches(ox_hg, ox_gla, ox_ssd, px[17],
                                                   w_br_hg[l], w_br_gla[l], w_br_ssd[l], w_out[l])

        hx = rms_norm(x, norm_ffn_w[l]) * (1 + mod_x[4][:, None]) + mod_x[3][:, None]
        x = x + mod_x[5][:, None] * moe_ffn(hx.reshape(-1, D), router_w[l], router_bias[l], exp_w_gate[l],
                                            exp_w_up[l], exp_w_down[l], sh_w_gate[l], sh_w_up[l],
                                            sh_w_down[l]).reshape(x.shape)
        if ctx_out:
            s = s + mod_c[2] * merge_branches(oc_hg, oc_gla, oc_ssd, pc[17],
                                              w_br_hg[l], w_br_gla[l], w_br_ssd[l], w_out[l])
            hc = rms_norm(s, norm_ffn_w[l]) * (1 + mod_c[4]) + mod_c[3]
            s = s + mod_c[5] * moe_ffn(hc.reshape(-1, D), router_w[l], router_bias[l], exp_w_gate[l],
                                       exp_w_up[l], exp_w_down[l], sh_w_gate[l], sh_w_up[l],
                                       sh_w_down[l]).reshape(s.shape)
    return rms_norm(x, final_norm_w)
```

```python
import functools
import math

import numpy as np
import jax
import jax.numpy as jnp
from jax import lax
from jax.experimental import pallas as pl
from jax.experimental.pallas import tpu as pltpu

f32 = jnp.float32
bf16 = jnp.bfloat16
HIGHEST = lax.Precision.HIGHEST

EPS = 1e-6
POS_BASE = 10000.0
GRID_W = 64
N_MOD = 6
HG_HEADS = 4
HG_DK = 128
HG_WIDTH = HG_HEADS * HG_DK
GLA_HEADS = 4
GLA_DK = 64
GLA_DV = 128
GLA_RANK = 16
GLA_TAU = 16.0
SSD_HEADS = 8
SSD_HEADDIM = 64
SSD_INNER = SSD_HEADS * SSD_HEADDIM
SSD_GROUPS = 2
SSD_STATE = 128
SSD_CONV = 5
N_BRANCH = 3
N_EXPERTS = 64
TOP_K = 8
N_GROUPS = 8
TOPK_GROUPS = 4
ROUTE_SCALE = 2.5
MOE_BLOCK = 256
NEG_MASK = -1e4

LANES = 128
SUBLANES = 8
VMEM_LIMIT = 56 * 1024 * 1024
EXP_CLAMP = 80.0

P_GATE = 0
P_HG = 3072
P_GLA = 5632
P_SSD = 7168
P_TOTAL = 8704
S_LRF, S_LRB, S_DTF, S_DTB = 0, 16, 32, 40


def _cparams(sem):
    return pltpu.CompilerParams(dimension_semantics=sem, vmem_limit_bytes=VMEM_LIMIT)


def _split_hi_lo(x):
    hi = x.astype(bf16)
    lo = (x - hi.astype(f32)).astype(bf16)
    return hi, lo


def _dot(a, b):
    return jnp.dot(a, b, preferred_element_type=f32)


def _dot_nt(a, b):
    return lax.dot_general(a, b, (((1,), (1,)), ((), ())), preferred_element_type=f32)


def _dot_tn(a, b):
    return lax.dot_general(a, b, (((0,), (0,)), ((), ())), preferred_element_type=f32)


def _dot2(m, x):
    hi, lo = _split_hi_lo(x)
    return _dot(m, hi) + _dot(m, lo)


def _dot2_l(x, m):
    hi, lo = _split_hi_lo(x)
    return _dot(hi, m) + _dot(lo, m)


def _silu(x):
    return x * jax.nn.sigmoid(x)


def _log_sigmoid(z):
    return jnp.minimum(z, 0.0) - jnp.log(1.0 + jnp.exp(-jnp.abs(z)))


def _softplus(z):
    return jnp.maximum(z, 0.0) + jnp.log(1.0 + jnp.exp(-jnp.abs(z)))


def _scan_mask(c, rev):
    r = lax.broadcasted_iota(jnp.int32, (c, c), 0)
    s = lax.broadcasted_iota(jnp.int32, (c, c), 1)
    return (r <= s) if rev else (r >= s)


def _mods_kernel(c_ref, w_ref, b_ref, o_ref):
    o_ref[...] = jnp.dot(_silu(c_ref[...]), w_ref[...], precision=HIGHEST,
                         preferred_element_type=f32) + b_ref[...]


def _mods(c_all, ada_w, ada_b):
    depth, d, n = ada_w.shape
    tn = 1024
    return pl.pallas_call(
        _mods_kernel,
        grid=(depth, n // tn),
        in_specs=[pl.BlockSpec((SUBLANES, d), lambda l, j: (0, 0)),
                  pl.BlockSpec((None, d, tn), lambda l, j: (l, 0, j)),
                  pl.BlockSpec((None, 1, tn), lambda l, j: (l, 0, j))],
        out_specs=pl.BlockSpec((None, SUBLANES, tn), lambda l, j: (l, 0, j)),
        out_shape=jax.ShapeDtypeStruct((depth, SUBLANES, n), f32),
        compiler_params=_cparams(("parallel", "parallel")),
        name="mods",
    )(c_all, ada_w, ada_b.reshape(depth, 1, n))


def _embed_kernel(ctx_ref, x_ref, pe_ref, o_ref, *, nct):
    i = pl.program_id(0)

    @pl.when(i < nct)
    def _():
        o_ref[...] = ctx_ref[...]

    @pl.when(i >= nct)
    def _():
        o_ref[...] = x_ref[...] + pe_ref[...]


def _embed(ctx2, x2, pe, te):
    nctx, d = ctx2.shape
    nlat = x2.shape[0]
    nct, npe = nctx // te, pe.shape[0] // te
    return pl.pallas_call(
        functools.partial(_embed_kernel, nct=nct),
        grid=((nctx + nlat) // te,),
        in_specs=[pl.BlockSpec((te, d), lambda i: (jnp.minimum(i, nct - 1), 0)),
                  pl.BlockSpec((te, d), lambda i: (jnp.maximum(i - nct, 0), 0)),
                  pl.BlockSpec((te, d), lambda i: (jnp.maximum(i - nct, 0) % npe, 0))],
        out_specs=pl.BlockSpec((te, d), lambda i: (i, 0)),
        out_shape=jax.ShapeDtypeStruct((nctx + nlat, d), f32),
        compiler_params=_cparams(("parallel",)),
        name="embed",
    )(ctx2, x2, pe)


def _norm_rows(x, w):
    return x * lax.rsqrt(jnp.mean(x * x, axis=-1, keepdims=True) + EPS) * w


def _norm_mod_kernel(x_ref, w_ref, sh_ref, sc_ref, h_ref):
    y = _norm_rows(x_ref[...], w_ref[...])
    h_ref[...] = (y * (1.0 + sc_ref[...]) + sh_ref[...]).astype(bf16)


def _norm_mod_router_kernel(x_ref, w_ref, sh_ref, sc_ref, rw_ref, h_ref, s_ref):
    y = _norm_rows(x_ref[...], w_ref[...])
    h = y * (1.0 + sc_ref[...]) + sh_ref[...]
    h_ref[...] = h.astype(bf16)
    s_ref[...] = jax.nn.sigmoid(jnp.dot(h, rw_ref[...], precision=HIGHEST, preferred_element_type=f32))


def _mod_block(d, tm, geo):
    nctx, seq, nb = geo
    return pl.BlockSpec((None, 1, d), lambda i: (jnp.where(i * tm < nctx, nb, (i * tm - nctx) // seq), 0, 0))


def _norm_mod(xs, w, shift, scale, geo, tm, router_w=None):
    t, d = xs.shape
    in_specs = [pl.BlockSpec((tm, d), lambda i: (i, 0)),
                pl.BlockSpec((1, d), lambda i: (0, 0)),
                _mod_block(d, tm, geo), _mod_block(d, tm, geo)]
    args = [xs, w.reshape(1, d), shift, scale]
    if router_w is None:
        return pl.pallas_call(
            _norm_mod_kernel, grid=(t // tm,), in_specs=in_specs,
            out_specs=pl.BlockSpec((tm, d), lambda i: (i, 0)),
            out_shape=jax.ShapeDtypeStruct((t, d), bf16),
            compiler_params=_cparams(("parallel",)), name="norm_mod",
        )(*args)
    ne = router_w.shape[1]
    return pl.pallas_call(
        _norm_mod_router_kernel, grid=(t // tm,),
        in_specs=in_specs + [pl.BlockSpec((d, ne), lambda i: (0, 0))],
        out_specs=[pl.BlockSpec((tm, d), lambda i: (i, 0)), pl.BlockSpec((tm, ne), lambda i: (i, 0))],
        out_shape=[jax.ShapeDtypeStruct((t, d), bf16), jax.ShapeDtypeStruct((t, ne), f32)],
        compiler_params=_cparams(("parallel",)), name="norm_mod_router",
    )(*args, router_w)


def _matmul_kernel(a_ref, w_ref, o_ref):
    o_ref[...] = _dot(a_ref[...], w_ref[...]).astype(o_ref.dtype)


def _matmul(a, w, tm, tn, out_dtype):
    m, k = a.shape
    n = w.shape[1]
    return pl.pallas_call(
        _matmul_kernel, grid=(m // tm, n // tn),
        in_specs=[pl.BlockSpec((tm, k), lambda i, j: (i, 0)),
                  pl.BlockSpec((k, tn), lambda i, j: (0, j))],
        out_specs=pl.BlockSpec((tm, tn), lambda i, j: (i, j)),
        out_shape=jax.ShapeDtypeStruct((m, n), out_dtype),
        compiler_params=_cparams(("parallel", "parallel")), name="in_proj",
    )(a, w)


def _small_proj_kernel(a_ref, w_ref, wt_ref, o_ref, ot_ref):
    a = a_ref[...]
    o_ref[...] = _dot(a, w_ref[...])
    ot_ref[...] = _dot_nt(wt_ref[...], a)


def _small_proj(a, w, tm):
    m, k = a.shape
    n = w.shape[1]
    return pl.pallas_call(
        _small_proj_kernel, grid=(m // tm,),
        in_specs=[pl.BlockSpec((tm, k), lambda i: (i, 0)),
                  pl.BlockSpec((k, n), lambda i: (0, 0)),
                  pl.BlockSpec((n, k), lambda i: (0, 0))],
        out_specs=[pl.BlockSpec((tm, n), lambda i: (i, 0)), pl.BlockSpec((n, tm), lambda i: (0, i))],
        out_shape=[jax.ShapeDtypeStruct((m, n), f32), jax.ShapeDtypeStruct((n, m), f32)],
        compiler_params=_cparams(("parallel",)), name="small_proj",
    )(a, w, w.T)


def _chunk_blocks(geo, c, rev):
    nctx, seq, nb = geo
    lc = nctx // nb
    ncc, nlc = lc // c, seq // c

    def blk(b, i):
        j_ctx = (ncc - 1 - i) if rev else i
        j_lat = (nlc - 1 - (i - ncc)) if rev else (i - ncc)
        return jnp.where(i < ncc, b * ncc + j_ctx, nctx // c + b * nlc + j_lat)

    return blk, ncc + nlc


def _decay_terms(q, k, g, mask, rev):
    c = q.shape[0]
    b = _dot2(mask.astype(bf16), g)
    last = 0 if rev else c - 1
    b_tot = b[last:last + 1, :]
    b_mid = b[c // 2:c // 2 + 1, :]
    qd = q * jnp.exp(jnp.minimum(b - b_mid, EXP_CLAMP))
    kd = k * jnp.exp(jnp.minimum(b_mid - b, EXP_CLAMP))
    qe = q * jnp.exp(b)
    ke = k * jnp.exp(b_tot - b)
    return qd.astype(bf16), kd.astype(bf16), qe.astype(bf16), ke.astype(bf16), jnp.exp(b_tot)


def _hgrn2_gates(z, lb):
    e = jnp.exp(-jnp.abs(z))
    inv = 1.0 / (1.0 + e)
    sig_pos = jnp.where(z >= 0, 1.0, e) * inv
    sig_neg = jnp.where(z >= 0, e, 1.0) * inv
    ls = jnp.minimum(z, 0.0) - jnp.log(1.0 + e)
    log_f = jnp.where(lb > 0.0, jnp.log(jnp.where(lb > 0.0, lb + (1.0 - lb) * sig_pos, 1.0)), ls)
    return (1.0 - lb) * sig_neg, log_f


def _hgrn2_kernel(qf_ref, zf_ref, vf_ref, qb_ref, zb_ref, vb_ref, lbf_ref, lbb_ref,
                  of_ref, ob_ref, sf_ref, sb_ref):
    @pl.when(pl.program_id(1) == 0)
    def _():
        sf_ref[...] = jnp.zeros_like(sf_ref)
        sb_ref[...] = jnp.zeros_like(sb_ref)

    c = qf_ref.shape[0]
    for rev, q_ref, z_ref, v_ref, lb_ref, o_ref, s_ref in (
            (False, qf_ref, zf_ref, vf_ref, lbf_ref, of_ref, sf_ref),
            (True, qb_ref, zb_ref, vb_ref, lbb_ref, ob_ref, sb_ref)):
        mask = _scan_mask(c, rev)
        q = q_ref[...].astype(f32)
        k, g = _hgrn2_gates(z_ref[...].astype(f32), lb_ref[...])
        v = v_ref[...]
        qd, kd, qe, ke, dtot = _decay_terms(q, k, g, mask, rev)
        for h in range(HG_HEADS):
            sl = slice(h * HG_DK, (h + 1) * HG_DK)
            st = s_ref[h]
            att = jnp.where(mask, _dot_nt(qd[:, sl], kd[:, sl]), 0.0)
            o = _dot(att.astype(bf16), v[:, sl]) + _dot_nt(qe[:, sl], st.astype(bf16))
            o_ref[:, sl] = o.astype(o_ref.dtype)
            s_ref[h] = st * dtot[:, sl] + _dot_tn(v[:, sl], ke[:, sl])


def _hgrn2_scan(p, lb_f, lb_b, geo, c):
    t = p.shape[0]
    nb = geo[2]
    blk_f, steps = _chunk_blocks(geo, c, False)
    blk_b, _ = _chunk_blocks(geo, c, True)
    w = HG_WIDTH
    c0 = P_HG // w

    def spec(blk, col):
        return pl.BlockSpec((c, w), lambda b, i: (blk(b, i), col))

    vec = pl.BlockSpec((1, w), lambda b, i: (0, 0))
    return pl.pallas_call(
        _hgrn2_kernel, grid=(nb, steps),
        in_specs=[spec(blk_f, c0), spec(blk_f, c0 + 1), spec(blk_f, c0 + 3),
                  spec(blk_b, c0), spec(blk_b, c0 + 2), spec(blk_b, c0 + 3), vec, vec],
        out_specs=[spec(blk_f, 0), spec(blk_b, 0)],
        out_shape=[jax.ShapeDtypeStruct((t, w), bf16)] * 2,
        scratch_shapes=[pltpu.VMEM((HG_HEADS, HG_DK, HG_DK), f32)] * 2,
        compiler_params=_cparams(("parallel", "arbitrary")), name="hgrn2_scan",
    )(p, p, p, p, p, p, lb_f.reshape(1, w), lb_b.reshape(1, w))


def _gla_kernel(qf_ref, kf_ref, vf_ref, sf_in_ref, qb_ref, kb_ref, vb_ref, sb_in_ref,
                gwf_ref, gwb_ref, gbf_ref, gbb_ref, of_ref, ob_ref, sf_ref, sb_ref):
    @pl.when(pl.program_id(1) == 0)
    def _():
        sf_ref[...] = jnp.zeros_like(sf_ref)
        sb_ref[...] = jnp.zeros_like(sb_ref)

    c = qf_ref.shape[0]
    npair = GLA_HEADS // 2
    lane = lax.broadcasted_iota(jnp.int32, (1, LANES), 1)
    vrow = lax.broadcasted_iota(jnp.int32, (2 * GLA_DV, LANES), 0)
    vcol = lax.broadcasted_iota(jnp.int32, (2 * GLA_DV, LANES), 1)
    block_diag = (vrow // GLA_DV) == (vcol // GLA_DK)
    for rev, q_ref, k_ref, v_ref, sm_ref, gw_ref, gb_ref, o_ref, s_ref in (
            (False, qf_ref, kf_ref, vf_ref, sf_in_ref, gwf_ref, gbf_ref, of_ref, sf_ref),
            (True, qb_ref, kb_ref, vb_ref, sb_in_ref, gwb_ref, gbb_ref, ob_ref, sb_ref)):
        mask = _scan_mask(c, rev)
        q = q_ref[...].astype(f32) * (GLA_DK ** -0.5)
        k = k_ref[...].astype(f32)
        v = v_ref[...]
        pre = jnp.dot(sm_ref[...], gw_ref[...], precision=HIGHEST, preferred_element_type=f32) + gb_ref[...]
        g = _log_sigmoid(pre) / GLA_TAU
        qd, kd, qe, ke, dtot = _decay_terms(q, k, g, mask, rev)
        for p in range(npair):
            sl = slice(p * LANES, (p + 1) * LANES)
            vsl = slice(p * 2 * GLA_DV, (p + 1) * 2 * GLA_DV)
            st = s_ref[p]
            o_inter = _dot_nt(qe[:, sl], st.astype(bf16))
            for hh in range(2):
                head = (lane // GLA_DK) == hh
                qh = jnp.where(head, qd[:, sl], jnp.zeros_like(qd[:, sl]))
                att = jnp.where(mask, _dot_nt(qh, kd[:, sl]), 0.0)
                osl = slice((2 * p + hh) * GLA_DV, (2 * p + hh + 1) * GLA_DV)
                o = _dot(att.astype(bf16), v[:, osl]) + o_inter[:, hh * GLA_DV:(hh + 1) * GLA_DV]
                o_ref[:, osl] = o.astype(o_ref.dtype)
            upd = st * dtot[:, sl] + _dot_tn(v[:, vsl], ke[:, sl])
            s_ref[p] = jnp.where(block_diag, upd, 0.0)


def _gla_scan(p, psmall, gate_w, gate_b, geo, c):
    t = p.shape[0]
    nb = geo[2]
    blk_f, steps = _chunk_blocks(geo, c, False)
    blk_b, _ = _chunk_blocks(geo, c, True)
    wk, wv = GLA_HEADS * GLA_DK, GLA_HEADS * GLA_DV

    def spec(blk, width, col):
        return pl.BlockSpec((c, width), lambda b, i: (blk(b, i), col))

    def const(shape):
        return pl.BlockSpec(shape, lambda b, i: (0, 0))

    gwf = jnp.zeros((LANES, wk), f32).at[S_LRF:S_LRF + GLA_RANK].set(gate_w[0])
    gwb = jnp.zeros((LANES, wk), f32).at[S_LRB:S_LRB + GLA_RANK].set(gate_w[1])
    qcol, kcol, vcol = P_GLA // wk, P_GLA // wk + 1, (P_GLA + 2 * wk) // wv
    return pl.pallas_call(
        _gla_kernel, grid=(nb, steps),
        in_specs=[spec(blk_f, wk, qcol), spec(blk_f, wk, kcol), spec(blk_f, wv, vcol), spec(blk_f, LANES, 0),
                  spec(blk_b, wk, qcol), spec(blk_b, wk, kcol), spec(blk_b, wv, vcol), spec(blk_b, LANES, 0),
                  const((LANES, wk)), const((LANES, wk)), const((1, wk)), const((1, wk))],
        out_specs=[spec(blk_f, wv, 0), spec(blk_b, wv, 0)],
        out_shape=[jax.ShapeDtypeStruct((t, wv), bf16)] * 2,
        scratch_shapes=[pltpu.VMEM((GLA_HEADS // 2, 2 * GLA_DV, 2 * GLA_DK), f32)] * 2,
        compiler_params=_cparams(("parallel", "arbitrary")), name="gla_scan",
    )(p, p, p, psmall, p, p, p, psmall, gwf, gwb, gate_b[0].reshape(1, wk), gate_b[1].reshape(1, wk))


HALO = SUBLANES


def _conv_kernel(prev_ref, cur_ref, next_ref, w_ref, b_ref, o_ref, ext_ref, *, geo, tc):
    nctx, seq, nb = geo
    r0 = pl.program_id(0) * tc
    in_ctx = r0 < nctx
    seg_len = jnp.where(in_ctx, nctx // nb, seq)
    pos = jnp.where(in_ctx, r0 % (nctx // nb), (r0 - nctx) % seq)
    first = pos == 0
    last = pos + tc == seg_len
    ext_ref[0:HALO, :] = jnp.where(first, 0.0, prev_ref[...].astype(f32))
    ext_ref[HALO:HALO + tc, :] = cur_ref[...].astype(f32)
    ext_ref[HALO + tc:HALO + tc + HALO, :] = jnp.where(last, 0.0, next_ref[...].astype(f32))
    acc = jnp.zeros(o_ref.shape, f32) + b_ref[...]
    for j in range(SSD_CONV):
        off = HALO - SSD_CONV // 2 + j
        acc = acc + ext_ref[off:off + tc, :] * w_ref[j:j + 1, :]
    o_ref[...] = _silu(acc).astype(o_ref.dtype)


def _ssd_conv(p, conv_w, conv_b, geo, tc):
    t = p.shape[0]
    nch = conv_w.shape[1]
    tw = 512
    col0 = (P_SSD + SSD_INNER) // tw
    hb = tc // HALO
    nhb = t // HALO
    return pl.pallas_call(
        functools.partial(_conv_kernel, geo=geo, tc=tc),
        grid=(t // tc, nch // tw),
        in_specs=[pl.BlockSpec((HALO, tw), lambda i, j: (jnp.maximum(i * hb - 1, 0), col0 + j)),
                  pl.BlockSpec((tc, tw), lambda i, j: (i, col0 + j)),
                  pl.BlockSpec((HALO, tw), lambda i, j: (jnp.minimum((i + 1) * hb, nhb - 1), col0 + j)),
                  pl.BlockSpec((SSD_CONV, tw), lambda i, j: (0, j)),
                  pl.BlockSpec((1, tw), lambda i, j: (0, j))],
        out_specs=pl.BlockSpec((tc, tw), lambda i, j: (i, j)),
        out_shape=jax.ShapeDtypeStruct((t, nch), bf16),
        scratch_shapes=[pltpu.VMEM((tc + 2 * HALO, tw), f32)],
        compiler_params=_cparams(("parallel", "parallel")), name="ssd_conv",
    )(p, p, p, conv_w, conv_b.reshape(1, nch))


def _ssd_kernel(xf_ref, bf_ref, cf_ref, smf_ref, smtf_ref, xb_ref, bb_ref, cb_ref, smb_ref, smtb_ref,
                arow_ref, brow_ref, acol_ref, bcol_ref, ef_ref, eb_ref,
                yf_ref, yb_ref, sf_ref, sb_ref):
    @pl.when(pl.program_id(1) == 0)
    def _():
        sf_ref[...] = jnp.zeros_like(sf_ref)
        sb_ref[...] = jnp.zeros_like(sb_ref)

    c = xf_ref.shape[0]
    gw = SSD_INNER // SSD_GROUPS
    lane = lax.broadcasted_iota(jnp.int32, (1, LANES), 1)
    for d, rev, x_ref, bm_ref, cm_ref, sm_ref, smt_ref, e_ref, y_ref, s_ref, lane0 in (
            (0, False, xf_ref, bf_ref, cf_ref, smf_ref, smtf_ref, ef_ref, yf_ref, sf_ref, S_DTF),
            (1, True, xb_ref, bb_ref, cb_ref, smb_ref, smtb_ref, eb_ref, yb_ref, sb_ref, S_DTB)):
        mask = _scan_mask(c, rev)
        mask_bf = mask.astype(bf16)
        expand = e_ref[...]
        dt = _softplus(sm_ref[...] + brow_ref[d:d + 1, :])
        cum = _dot2(mask_bf, dt * arow_ref[d:d + 1, :])
        last = 0 if rev else c - 1
        cum_end = cum[last:last + 1, :]
        dt_t = _softplus(smt_ref[...] + bcol_ref[d])
        cum_t = _dot2_l(dt_t * acol_ref[d], _scan_mask(c, not rev).astype(bf16))

        x = x_ref[...].astype(f32)
        xdt = x * _dot2_l(dt, expand)
        w = (xdt * _dot2_l(jnp.exp(cum_end - cum), expand)).astype(bf16)
        xdt = xdt.astype(bf16)
        dec_t = _dot2_l(jnp.exp(cum), expand)
        dec_end = _dot2_l(jnp.broadcast_to(jnp.exp(cum_end), (SUBLANES, LANES)), expand)[0:1, :]
        bm = bm_ref[...]
        cm = cm_ref[...]
        for g in range(SSD_GROUPS):
            gsl = slice(g * SSD_STATE, (g + 1) * SSD_STATE)
            hsl = slice(g * gw, (g + 1) * gw)
            st = s_ref[:, hsl]
            scores = _dot_nt(cm[:, gsl], bm[:, gsl])
            y_inter = _dot(cm[:, gsl], st.astype(bf16)) * dec_t[:, hsl]
            for tile in range(gw // LANES):
                tsl = slice(g * gw + tile * LANES, g * gw + (tile + 1) * LANES)
                xt = xdt[:, tsl]
                acc = y_inter[:, tile * LANES:(tile + 1) * LANES]
                for hh in range(LANES // SSD_HEADDIM):
                    h = (g * gw + tile * LANES) // SSD_HEADDIM + hh
                    diff = cum[:, lane0 + h:lane0 + h + 1] - cum_t[h:h + 1, :]
                    seg = jnp.where(mask, jnp.exp(jnp.minimum(diff, 0.0)), 0.0)
                    xh = jnp.where((lane // SSD_HEADDIM) == hh, xt, jnp.zeros_like(xt))
                    acc = acc + _dot((scores * seg).astype(bf16), xh)
                y_ref[:, tsl] = acc.astype(y_ref.dtype)
            s_ref[:, hsl] = st * dec_end[:, hsl] + _dot_tn(bm[:, gsl], w[:, hsl])


def _ssd_scan(xbc, psmall, psmall_t, a_log, dt_bias, geo, c):
    t = xbc.shape[0]
    nb = geo[2]
    blk_f, steps = _chunk_blocks(geo, c, False)
    blk_b, _ = _chunk_blocks(geo, c, True)
    wi, wn = SSD_INNER, SSD_GROUPS * SSD_STATE

    def spec(blk, width, col):
        return pl.BlockSpec((c, width), lambda b, i: (blk(b, i), col))

    def spec_t(blk, row):
        return pl.BlockSpec((SUBLANES, c), lambda b, i: (row, blk(b, i)))

    def const(shape):
        return pl.BlockSpec(shape, lambda b, i: (0,) * len(shape))

    a = -jnp.exp(a_log.astype(f32))
    arow = jnp.zeros((2, LANES), f32).at[0, S_DTF:S_DTF + SSD_HEADS].set(a[0]).at[1, S_DTB:S_DTB + SSD_HEADS].set(a[1])
    brow = jnp.zeros((2, LANES), f32).at[0, S_DTF:S_DTF + SSD_HEADS].set(dt_bias[0]).at[1, S_DTB:S_DTB + SSD_HEADS].set(dt_bias[1])
    head_of_lane = np.arange(wi) // SSD_HEADDIM
    ef = jnp.asarray((np.arange(LANES)[:, None] == S_DTF + head_of_lane[None, :]), bf16)
    eb = jnp.asarray((np.arange(LANES)[:, None] == S_DTB + head_of_lane[None, :]), bf16)
    xcol, bcol, ccol = 0, wi // wn, wi // wn + 1
    return pl.pallas_call(
        _ssd_kernel, grid=(nb, steps),
        in_specs=[spec(blk_f, wi, xcol), spec(blk_f, wn, bcol), spec(blk_f, wn, ccol), spec(blk_f, LANES, 0),
                  spec_t(blk_f, S_DTF // SUBLANES),
                  spec(blk_b, wi, xcol), spec(blk_b, wn, bcol), spec(blk_b, wn, ccol), spec(blk_b, LANES, 0),
                  spec_t(blk_b, S_DTB // SUBLANES),
                  const((2, LANES)), const((2, LANES)), const((2, SSD_HEADS, 1)), const((2, SSD_HEADS, 1)),
                  const((LANES, wi)), const((LANES, wi))],
        out_specs=[spec(blk_f, wi, 0), spec(blk_b, wi, 0)],
        out_shape=[jax.ShapeDtypeStruct((t, wi), bf16)] * 2,
        scratch_shapes=[pltpu.VMEM((SSD_STATE, wi), f32)] * 2,
        compiler_params=_cparams(("parallel", "arbitrary")), name="ssd_scan",
    )(xbc, xbc, xbc, psmall, psmall_t, xbc, xbc, xbc, psmall, psmall_t,
      arow, brow, a.reshape(2, SSD_HEADS, 1), dt_bias.astype(f32).reshape(2, SSD_HEADS, 1), ef, eb)


def _group_norm(o, w, width):
    parts = []
    for j in range(o.shape[1] // width):
        blk = o[:, j * width:(j + 1) * width]
        parts.append(blk * lax.rsqrt(jnp.mean(blk * blk, axis=-1, keepdims=True) + EPS))
    return jnp.concatenate(parts, axis=1) * w


def _merge_kernel(x_ref, hgf_ref, hgb_ref, hgg_ref, glf_ref, glb_ref, glg_ref,
                  sdf_ref, sdb_ref, sdx_ref, sdz_ref, gate_ref,
                  hgw_ref, glw_ref, sdd_ref, sdw_ref, wh_ref, wg_ref, ws_ref, wo_ref, m2_ref, o_ref):
    o_hg = _group_norm(hgf_ref[...].astype(f32) + hgb_ref[...].astype(f32), hgw_ref[...], HG_DK)
    o_hg = o_hg * _silu(hgg_ref[...].astype(f32))
    o_gl = _group_norm(glf_ref[...].astype(f32) + glb_ref[...].astype(f32), glw_ref[...], GLA_DV)
    o_gl = o_gl * _silu(glg_ref[...].astype(f32))
    y = sdf_ref[...].astype(f32) + sdb_ref[...].astype(f32) + sdd_ref[...] * sdx_ref[...].astype(f32)
    o_sd = _group_norm(y * _silu(sdz_ref[...].astype(f32)), sdw_ref[...], SSD_INNER // SSD_GROUPS)
    d = x_ref.shape[1]
    gate = jax.nn.sigmoid(gate_ref[...].astype(f32))
    y = (gate[:, 0:d] * _dot(o_hg.astype(bf16), wh_ref[...])
         + gate[:, d:2 * d] * _dot(o_gl.astype(bf16), wg_ref[...])
         + gate[:, 2 * d:3 * d] * _dot(o_sd.astype(bf16), ws_ref[...]))
    o_ref[...] = x_ref[...] + m2_ref[...] * _dot(y.astype(bf16), wo_ref[...])


def _merge(xs, p, hg_f, hg_b, gl_f, gl_b, sd_f, sd_b, xbc, hg_norm_w, gla_norm_w, ssd_d, ssd_norm_w,
           w_hg, w_gla, w_ssd, w_out, mod2, geo, tm):
    t, d = xs.shape
    w = 512

    def rows(width, col):
        return pl.BlockSpec((tm, width), lambda i: (i, col))

    def const(shape):
        return pl.BlockSpec(shape, lambda i: (0, 0))

    tile4 = lambda v: jnp.tile(v, w // v.shape[0]).reshape(1, w)
    return pl.pallas_call(
        _merge_kernel, grid=(t // tm,),
        in_specs=[rows(d, 0),
                  rows(w, 0), rows(w, 0), rows(w, (P_HG + 4 * w) // w),
                  rows(w, 0), rows(w, 0), rows(w, (P_GLA + 2 * w) // w),
                  rows(w, 0), rows(w, 0), rows(w, 0), rows(w, P_SSD // w),
                  rows(3 * d, P_GATE // (3 * d)),
                  const((1, w)), const((1, w)), const((1, w)), const((1, w)),
                  const((w, d)), const((w, d)), const((w, d)), const((d, d)),
                  _mod_block(d, tm, geo)],
        out_specs=rows(d, 0),
        out_shape=jax.ShapeDtypeStruct((t, d), f32),
        compiler_params=_cparams(("parallel",)), name="merge",
    )(xs, hg_f, hg_b, p, gl_f, gl_b, p, sd_f, sd_b, xbc, p, p,
      tile4(hg_norm_w), tile4(gla_norm_w), jnp.repeat(ssd_d, SSD_HEADDIM).reshape(1, w), ssd_norm_w.reshape(1, w),
      w_hg, w_gla, w_ssd, w_out, mod2)


def _expert_kernel(be_ref, x_ref, wg_ref, wu_ref, wd_ref, o_ref):
    x = x_ref[...]
    a = _silu(_dot(x, wg_ref[...])) * _dot(x, wu_ref[...])
    o_ref[...] = _dot(a.astype(bf16), wd_ref[...]).astype(o_ref.dtype)


def _experts(xg, block_e, w_gate, w_up, w_down):
    npad, d = xg.shape
    ff = w_gate.shape[2]
    nblk = npad // MOE_BLOCK
    grid_spec = pltpu.PrefetchScalarGridSpec(
        num_scalar_prefetch=1, grid=(nblk,),
        in_specs=[pl.BlockSpec((MOE_BLOCK, d), lambda i, be: (i, 0)),
                  pl.BlockSpec((None, d, ff), lambda i, be: (be[i], 0, 0)),
                  pl.BlockSpec((None, d, ff), lambda i, be: (be[i], 0, 0)),
                  pl.BlockSpec((None, ff, d), lambda i, be: (be[i], 0, 0))],
        out_specs=pl.BlockSpec((MOE_BLOCK, d), lambda i, be: (i, 0)))
    return pl.pallas_call(
        _expert_kernel, grid_spec=grid_spec,
        out_shape=jax.ShapeDtypeStruct((npad, d), bf16),
        compiler_params=_cparams(("arbitrary",)), name="experts",
    )(block_e, xg, w_gate, w_up, w_down)


def _ffn_out_kernel(x_ref, h_ref, r_ref, sg_ref, su_ref, sd_ref, m5_ref, o_ref):
    h = h_ref[...]
    a = _silu(_dot(h, sg_ref[...])) * _dot(h, su_ref[...])
    shared = _dot(a.astype(bf16), sd_ref[...])
    o_ref[...] = x_ref[...] + m5_ref[...] * (r_ref[...] + shared)


def _ffn_out(xs, h, routed, s_gate, s_up, s_down, mod5, geo, tm):
    t, d = xs.shape
    ff = s_gate.shape[1]
    rows = pl.BlockSpec((tm, d), lambda i: (i, 0))
    return pl.pallas_call(
        _ffn_out_kernel, grid=(t // tm,),
        in_specs=[rows, rows, rows,
                  pl.BlockSpec((d, ff), lambda i: (0, 0)), pl.BlockSpec((d, ff), lambda i: (0, 0)),
                  pl.BlockSpec((ff, d), lambda i: (0, 0)), _mod_block(d, tm, geo)],
        out_specs=rows,
        out_shape=jax.ShapeDtypeStruct((t, d), f32),
        compiler_params=_cparams(("parallel",)), name="ffn_out",
    )(xs, h, routed, s_gate, s_up, s_down, mod5)


def _route(scores, router_bias):
    t = scores.shape[0]
    biased = scores + router_bias.astype(f32)
    gscore = lax.top_k(biased.reshape(t, N_GROUPS, N_EXPERTS // N_GROUPS), 2)[0].sum(-1)
    _, gidx = lax.top_k(gscore, TOPK_GROUPS)
    gmask = jnp.any(gidx[..., None] == jnp.arange(N_GROUPS), axis=-2)
    emask = jnp.repeat(gmask, N_EXPERTS // N_GROUPS, axis=-1)
    _, idx = lax.top_k(jnp.where(emask, biased, NEG_MASK), TOP_K)
    wts = jnp.take_along_axis(scores, idx, axis=-1)
    wts = wts / jnp.sum(wts, axis=-1, keepdims=True) * ROUTE_SCALE
    return idx, wts


def _dispatch(idx):
    t = idx.shape[0]
    n_assign = t * TOP_K
    flat_e = idx.reshape(-1)
    order = jnp.argsort(flat_e)
    se = flat_e[order]
    counts = jnp.bincount(flat_e, length=N_EXPERTS)
    padded = (counts + MOE_BLOCK - 1) // MOE_BLOCK * MOE_BLOCK
    pad_end = jnp.cumsum(padded)
    pad_start = pad_end - padded
    start = jnp.cumsum(counts) - counts
    dest = (pad_start[se] + jnp.arange(n_assign, dtype=jnp.int32) - start[se]).astype(jnp.int32)
    n_blocks = (n_assign + MOE_BLOCK - 1) // MOE_BLOCK + N_EXPERTS
    npad = n_blocks * MOE_BLOCK
    buf_t = jnp.zeros((npad,), jnp.int32).at[dest].set((order // TOP_K).astype(jnp.int32))
    slot = jnp.zeros((n_assign,), jnp.int32).at[order].set(dest)
    block_e = jnp.minimum(jnp.searchsorted(pad_end, jnp.arange(n_blocks, dtype=jnp.int32) * MOE_BLOCK,
                                           side='right'), N_EXPERTS - 1).astype(jnp.int32)
    return buf_t, slot.reshape(t, TOP_K), block_e


def _moe_routed(h, scores, router_bias, w_gate, w_up, w_down):
    idx, wts = _route(scores, router_bias)
    buf_t, slot, block_e = _dispatch(idx)
    xg = jnp.take(h, buf_t, axis=0)
    yg = _experts(xg, block_e, w_gate, w_up, w_down)
    y = jnp.take(yg, slot.reshape(-1), axis=0).reshape(slot.shape + (h.shape[1],))
    return jnp.sum(y.astype(f32) * wts[..., None], axis=1)


def _final_norm_kernel(x_ref, w_ref, o_ref):
    o_ref[...] = _norm_rows(x_ref[...], w_ref[...])


def _final_norm(xs, w, nctx, tm):
    t, d = xs.shape
    off = nctx // tm
    return pl.pallas_call(
        _final_norm_kernel, grid=((t - nctx) // tm,),
        in_specs=[pl.BlockSpec((tm, d), lambda i: (i + off, 0)), pl.BlockSpec((1, d), lambda i: (0, 0))],
        out_specs=pl.BlockSpec((tm, d), lambda i: (i, 0)),
        out_shape=jax.ShapeDtypeStruct((t - nctx, d), f32),
        compiler_params=_cparams(("parallel",)), name="final_norm",
    )(xs, w.reshape(1, d))


def _grid_sincos(rows, cols, dim):
    quarter = dim // 4
    omega = 1.0 / (POS_BASE ** (jnp.arange(quarter, dtype=f32) / quarter))

    def axis_embed(n):
        ang = jnp.arange(n, dtype=f32)[:, None] * omega
        return jnp.concatenate([jnp.sin(ang), jnp.cos(ang)], axis=-1)

    er, ec = axis_embed(rows), axis_embed(cols)
    pe = jnp.concatenate([jnp.broadcast_to(er[:, None], (rows, cols, dim // 2)),
                          jnp.broadcast_to(ec[None], (rows, cols, dim // 2))], axis=-1)
    return pe.reshape(rows * cols, dim)


def _lower_bounds(logits):
    p = jax.nn.softmax(logits.astype(f32), axis=1)
    return jnp.cumsum(p, axis=1) - p[:, :1]


def _split_w_in(w_in):
    a0 = 5 * HG_WIDTH + 2 * GLA_HEADS * GLA_DK + 2 * GLA_HEADS * GLA_DV
    a1 = a0 + 2 * GLA_RANK
    b0 = a1 + 2 * SSD_INNER + 2 * SSD_GROUPS * SSD_STATE
    b1 = b0 + 2 * SSD_HEADS
    main = jnp.concatenate([w_in[..., b1:], w_in[..., :a0], w_in[..., a1:b0]], axis=-1).astype(bf16)
    small = jnp.concatenate([w_in[..., a0:a1], w_in[..., b0:b1]], axis=-1)
    small = jnp.pad(small, ((0, 0), (0, 0), (0, LANES - small.shape[-1]))).astype(bf16)
    return main, small


def kernel(x, c, ctx, c_ctx, norm_mix_w, norm_ffn_w, final_norm_w, ada_w, ada_b, w_in,
           hg_lb_logits, hg_norm_w, gla_gate_w, gla_gate_b, gla_norm_w, ssd_conv_w, ssd_conv_b,
           ssd_a_log, ssd_dt_bias, ssd_d, ssd_norm_w, w_br_hg, w_br_gla, w_br_ssd, w_out,
           router_w, router_bias, exp_w_gate, exp_w_up, exp_w_down, sh_w_gate, sh_w_up, sh_w_down):
    nb, seq, d = x.shape
    lc = ctx.shape[1]
    depth = w_in.shape[0]
    nctx = nb * lc
    geo = (nctx, seq, nb)
    tm = min(512, lc)
    c_vec = min(64, lc)
    c_ssd = min(128, lc)

    c_all = jnp.zeros((SUBLANES, d), f32).at[:nb].set(c).at[nb].set(c_ctx)
    mods = _mods(c_all, ada_w, ada_b)
    mods = mods.reshape(depth, SUBLANES, N_MOD, 1, d).transpose(0, 2, 1, 3, 4)

    pe = _grid_sincos(seq // GRID_W, GRID_W, d)
    xs = _embed(ctx.reshape(nctx, d), x.reshape(nb * seq, d), pe, tm)

    lb = _lower_bounds(hg_lb_logits)
    w_main, w_small = _split_w_in(w_in)
    w_hg, w_gla, w_ssd, w_o = (w.astype(bf16) for w in (w_br_hg, w_br_gla, w_br_ssd, w_out))
    e_gate, e_up, e_down = (w.astype(bf16) for w in (exp_w_gate, exp_w_up, exp_w_down))
    s_gate, s_up, s_down = (w.astype(bf16) for w in (sh_w_gate, sh_w_up, sh_w_down))

    for l in range(depth):
        m = mods[l]
        h = _norm_mod(xs, norm_mix_w[l], m[0], m[1], geo, tm)
        p = _matmul(h, w_main[l], tm, 512, bf16)
        psmall, psmall_t = _small_proj(h, w_small[l], tm)
        hg_f, hg_b = _hgrn2_scan(p, lb[0, l], lb[1, l], geo, c_vec)
        gl_f, gl_b = _gla_scan(p, psmall, gla_gate_w[:, l], gla_gate_b[:, l], geo, c_vec)
        xbc = _ssd_conv(p, ssd_conv_w[l], ssd_conv_b[l], geo, min(256, lc))
        sd_f, sd_b = _ssd_scan(xbc, psmall, psmall_t, ssd_a_log[:, l], ssd_dt_bias[:, l], geo, c_ssd)
        xs = _merge(xs, p, hg_f, hg_b, gl_f, gl_b, sd_f, sd_b, xbc, hg_norm_w[l], gla_norm_w[l], ssd_d[l],
                    ssd_norm_w[l], w_hg[l], w_gla[l], w_ssd[l], w_o[l], m[2], geo, min(256, lc))
        h2, scores = _norm_mod(xs, norm_ffn_w[l], m[3], m[4], geo, tm, router_w=router_w[l])
        routed = _moe_routed(h2, scores, router_bias[l], e_gate[l], e_up[l], e_down[l])
        xs = _ffn_out(xs, h2, routed, s_gate[l], s_up[l], s_down[l], m[5], geo, tm)

    out = _final_norm(xs, final_norm_w, nctx, tm)
    return out.reshape(nb, seq, d)
```

```python
import functools
import math

import numpy as np
import jax
import jax.numpy as jnp
from jax import lax
from jax.experimental import pallas as pl
from jax.experimental.pallas import tpu as pltpu

f32 = jnp.float32
bf16 = jnp.bfloat16
HIGHEST = lax.Precision.HIGHEST

EPS = 1e-6
POS_BASE = 10000.0
GRID_W = 64
N_MOD = 6
HG_HEADS = 4
HG_DK = 128
HG_WIDTH = HG_HEADS * HG_DK
GLA_HEADS = 4
GLA_DK = 64
GLA_DV = 128
GLA_RANK = 16
GLA_TAU = 16.0
SSD_HEADS = 8
SSD_HEADDIM = 64
SSD_INNER = SSD_HEADS * SSD_HEADDIM
SSD_GROUPS = 2
SSD_STATE = 128
SSD_CONV = 5
N_BRANCH = 3
N_EXPERTS = 64
TOP_K = 8
N_GROUPS = 8
TOPK_GROUPS = 4
ROUTE_SCALE = 2.5
MOE_BLOCK = 256
NEG_MASK = -1e4

LANES = 128
SUBLANES = 8
VMEM_LIMIT = 56 * 1024 * 1024
EXP_CLAMP = 80.0

P_GATE = 0
P_HG = 3072
P_GLA = 5632
P_SSD = 7168
P_TOTAL = 8704
S_LRF, S_LRB, S_DTF, S_DTB = 0, 16, 32, 40


def _cparams(sem):
    return pltpu.CompilerParams(dimension_semantics=sem, vmem_limit_bytes=VMEM_LIMIT)


def _split_hi_lo(x):
    hi = x.astype(bf16)
    lo = (x - hi.astype(f32)).astype(bf16)
    return hi, lo


def _dot(a, b):
    return jnp.dot(a, b, preferred_element_type=f32)


def _dot_nt(a, b):
    return lax.dot_general(a, b, (((1,), (1,)), ((), ())), preferred_element_type=f32)


def _dot_tn(a, b):
    return lax.dot_general(a, b, (((0,), (0,)), ((), ())), preferred_element_type=f32)


def _dot2(m, x):
    hi, lo = _split_hi_lo(x)
    return _dot(m, hi) + _dot(m, lo)


def _dot2_l(x, m):
    hi, lo = _split_hi_lo(x)
    return _dot(hi, m) + _dot(lo, m)


def _silu(x):
    return x * jax.nn.sigmoid(x)


def _log_sigmoid(z):
    return jnp.minimum(z, 0.0) - jnp.log(1.0 + jnp.exp(-jnp.abs(z)))


def _softplus(z):
    return jnp.maximum(z, 0.0) + jnp.log(1.0 + jnp.exp(-jnp.abs(z)))


def _scan_mask(c, rev):
    r = lax.broadcasted_iota(jnp.int32, (c, c), 0)
    s = lax.broadcasted_iota(jnp.int32, (c, c), 1)
    return (r <= s) if rev else (r >= s)


def _mods_kernel(c_ref, w_ref, b_ref, o_ref):
    o_ref[...] = jnp.dot(_silu(c_ref[...]), w_ref[...], precision=HIGHEST,
                         preferred_element_type=f32) + b_ref[...]


def _mods(c_all, ada_w, ada_b):
    depth, d, n = ada_w.shape
    tn = 1024
    return pl.pallas_call(
        _mods_kernel,
        grid=(depth, n // tn),
        in_specs=[pl.BlockSpec((SUBLANES, d), lambda l, j: (0, 0)),
                  pl.BlockSpec((None, d, tn), lambda l, j: (l, 0, j)),
                  pl.BlockSpec((None, 1, tn), lambda l, j: (l, 0, j))],
        out_specs=pl.BlockSpec((None, SUBLANES, tn), lambda l, j: (l, 0, j)),
        out_shape=jax.ShapeDtypeStruct((depth, SUBLANES, n), f32),
        compiler_params=_cparams(("parallel", "parallel")),
        name="mods",
    )(c_all, ada_w, ada_b.reshape(depth, 1, n))


def _embed_kernel(ctx_ref, x_ref, pe_ref, o_ref, *, nct):
    i = pl.program_id(0)

    @pl.when(i < nct)
    def _():
        o_ref[...] = ctx_ref[...]

    @pl.when(i >= nct)
    def _():
        o_ref[...] = x_ref[...] + pe_ref[...]


def _embed(ctx2, x2, pe, te):
    nctx, d = ctx2.shape
    nlat = x2.shape[0]
    nct, npe = nctx // te, pe.shape[0] // te
    return pl.pallas_call(
        functools.partial(_embed_kernel, nct=nct),
        grid=((nctx + nlat) // te,),
        in_specs=[pl.BlockSpec((te, d), lambda i: (jnp.minimum(i, nct - 1), 0)),
                  pl.BlockSpec((te, d), lambda i: (jnp.maximum(i - nct, 0), 0)),
                  pl.BlockSpec((te, d), lambda i: (jnp.maximum(i - nct, 0) % npe, 0))],
        out_specs=pl.BlockSpec((te, d), lambda i: (i, 0)),
        out_shape=jax.ShapeDtypeStruct((nctx + nlat, d), f32),
        compiler_params=_cparams(("parallel",)),
        name="embed",
    )(ctx2, x2, pe)


def _norm_rows(x, w):
    return x * lax.rsqrt(jnp.mean(x * x, axis=-1, keepdims=True) + EPS) * w


def _norm_mod_kernel(x_ref, w_ref, sh_ref, sc_ref, h_ref):
    y = _norm_rows(x_ref[...], w_ref[...])
    h_ref[...] = (y * (1.0 + sc_ref[...]) + sh_ref[...]).astype(bf16)


def _beats(a, b, a_first):
    return (a >= b) if a_first else (a > b)


def _router_kernel(x_ref, w_ref, sh_ref, sc_ref, rwt_ref, rb_ref,
                   h_ref, e_ref, wt_ref, rank_ref, wrow_ref, cnt_ref):
    @pl.when(pl.program_id(0) == 0)
    def _():
        cnt_ref[...] = jnp.zeros_like(cnt_ref)

    y = _norm_rows(x_ref[...], w_ref[...])
    h = y * (1.0 + sc_ref[...]) + sh_ref[...]
    h_ref[...] = h.astype(bf16)
    tm = h.shape[0]
    logits = lax.dot_general(rwt_ref[...], h, (((1,), (1,)), ((), ())), precision=HIGHEST,
                             preferred_element_type=f32)
    scores = jax.nn.sigmoid(logits)
    biased = scores + rb_ref[...]

    gsz = N_EXPERTS // N_GROUPS
    miota = lax.broadcasted_iota(jnp.int32, (gsz, tm), 0)
    blocks, gscore = [], []
    for g in range(N_GROUPS):
        blk = biased[g * gsz:(g + 1) * gsz, :]
        m1 = jnp.max(blk, axis=0, keepdims=True)
        i1 = jnp.min(jnp.where(blk == m1, miota, gsz), axis=0, keepdims=True)
        m2 = jnp.max(jnp.where(miota == i1, -jnp.inf, blk), axis=0, keepdims=True)
        blocks.append(blk)
        gscore.append(m1 + m2)
    vals = []
    for g in range(N_GROUPS):
        ahead = jnp.zeros((1, tm), jnp.int32)
        for o in range(N_GROUPS):
            if o != g:
                ahead = ahead + _beats(gscore[o], gscore[g], o < g).astype(jnp.int32)
        vals.append(jnp.where(ahead < TOPK_GROUPS, blocks[g], NEG_MASK))
    vals = jnp.concatenate(vals, axis=0)

    eiota = lax.broadcasted_iota(jnp.int32, (N_EXPERTS, tm), 0)
    member = jnp.zeros((N_EXPERTS, tm), f32)
    chosen, weights = [], []
    for k in range(TOP_K):
        m = jnp.max(vals, axis=0, keepdims=True)
        ei = jnp.min(jnp.where(vals == m, eiota, N_EXPERTS), axis=0, keepdims=True)
        sel = eiota == ei
        chosen.append(ei)
        weights.append(jnp.sum(jnp.where(sel, scores, 0.0), axis=0, keepdims=True))
        member = member + sel.astype(f32)
        vals = jnp.where(sel, -jnp.inf, vals)
    wsum = weights[0]
    for k in range(1, TOP_K):
        wsum = wsum + weights[k]

    r = lax.broadcasted_iota(jnp.int32, (tm, tm), 0)
    s = lax.broadcasted_iota(jnp.int32, (tm, tm), 1)
    member_bf = member.astype(bf16)
    rank_all = _dot(member_bf, (r < s).astype(bf16)) + cnt_ref[:, 0:1]
    for k in range(TOP_K):
        e_ref[k:k + 1, :] = chosen[k]
        wt_ref[k:k + 1, :] = weights[k] / wsum * ROUTE_SCALE
        rank_ref[k:k + 1, :] = jnp.sum(jnp.where(eiota == chosen[k], rank_all, 0.0), axis=0,
                                       keepdims=True).astype(jnp.int32)
    cnt_ref[...] = cnt_ref[...] + _dot(member_bf, jnp.ones((tm, LANES), bf16))
    eye = (lax.broadcasted_iota(jnp.int32, (TOP_K, LANES), 0)
           == lax.broadcasted_iota(jnp.int32, (TOP_K, LANES), 1)).astype(bf16)
    w_hi, w_lo = _split_hi_lo(wt_ref[...])
    wrow_ref[...] = _dot_tn(w_hi, eye) + _dot_tn(w_lo, eye)


def _mod_block(d, tm, geo):
    nctx, seq, nb = geo
    return pl.BlockSpec((None, 1, d), lambda i: (jnp.where(i * tm < nctx, nb, (i * tm - nctx) // seq), 0, 0))


def _norm_mod(xs, w, shift, scale, geo, tm):
    t, d = xs.shape
    return pl.pallas_call(
        _norm_mod_kernel, grid=(t // tm,),
        in_specs=[pl.BlockSpec((tm, d), lambda i: (i, 0)),
                  pl.BlockSpec((1, d), lambda i: (0, 0)),
                  _mod_block(d, tm, geo), _mod_block(d, tm, geo)],
        out_specs=pl.BlockSpec((tm, d), lambda i: (i, 0)),
        out_shape=jax.ShapeDtypeStruct((t, d), bf16),
        compiler_params=_cparams(("parallel",)), name="norm_mod",
    )(xs, w.reshape(1, d), shift, scale)


def _router(xs, w, shift, scale, router_w, router_bias, geo, tm):
    t, d = xs.shape
    ne = router_w.shape[1]
    kt = pl.BlockSpec((TOP_K, tm), lambda i: (0, i))
    return pl.pallas_call(
        _router_kernel, grid=(t // tm,),
        in_specs=[pl.BlockSpec((tm, d), lambda i: (i, 0)),
                  pl.BlockSpec((1, d), lambda i: (0, 0)),
                  _mod_block(d, tm, geo), _mod_block(d, tm, geo),
                  pl.BlockSpec((ne, d), lambda i: (0, 0)),
                  pl.BlockSpec((ne, 1), lambda i: (0, 0))],
        out_specs=[pl.BlockSpec((tm, d), lambda i: (i, 0)), kt, kt, kt,
                   pl.BlockSpec((tm, LANES), lambda i: (i, 0)),
                   pl.BlockSpec((ne, LANES), lambda i: (0, 0))],
        out_shape=[jax.ShapeDtypeStruct((t, d), bf16),
                   jax.ShapeDtypeStruct((TOP_K, t), jnp.int32),
                   jax.ShapeDtypeStruct((TOP_K, t), f32),
                   jax.ShapeDtypeStruct((TOP_K, t), jnp.int32),
                   jax.ShapeDtypeStruct((t, LANES), f32),
                   jax.ShapeDtypeStruct((ne, LANES), f32)],
        compiler_params=_cparams(("arbitrary",)), name="router",
    )(xs, w.reshape(1, d), shift, scale, router_w.T, router_bias.astype(f32).reshape(ne, 1))


def _matmul_kernel(a_ref, w_ref, o_ref):
    o_ref[...] = _dot(a_ref[...], w_ref[...]).astype(o_ref.dtype)


def _matmul(a, w, tm, tn, out_dtype):
    m, k = a.shape
    n = w.shape[1]
    return pl.pallas_call(
        _matmul_kernel, grid=(m // tm, n // tn),
        in_specs=[pl.BlockSpec((tm, k), lambda i, j: (i, 0)),
                  pl.BlockSpec((k, tn), lambda i, j: (0, j))],
        out_specs=pl.BlockSpec((tm, tn), lambda i, j: (i, j)),
        out_shape=jax.ShapeDtypeStruct((m, n), out_dtype),
        compiler_params=_cparams(("parallel", "parallel")), name="in_proj",
    )(a, w)


def _small_proj_kernel(a_ref, w_ref, wt_ref, o_ref, ot_ref):
    a = a_ref[...]
    o_ref[...] = _dot(a, w_ref[...])
    ot_ref[...] = _dot_nt(wt_ref[...], a)


def _small_proj(a, w, tm):
    m, k = a.shape
    n = w.shape[1]
    return pl.pallas_call(
        _small_proj_kernel, grid=(m // tm,),
        in_specs=[pl.BlockSpec((tm, k), lambda i: (i, 0)),
                  pl.BlockSpec((k, n), lambda i: (0, 0)),
                  pl.BlockSpec((n, k), lambda i: (0, 0))],
        out_specs=[pl.BlockSpec((tm, n), lambda i: (i, 0)), pl.BlockSpec((n, tm), lambda i: (0, i))],
        out_shape=[jax.ShapeDtypeStruct((m, n), f32), jax.ShapeDtypeStruct((n, m), f32)],
        compiler_params=_cparams(("parallel",)), name="small_proj",
    )(a, w, w.T)


def _chunk_blocks(geo, c, rev):
    nctx, seq, nb = geo
    lc = nctx // nb
    ncc, nlc = lc // c, seq // c

    def blk(b, i):
        j_ctx = (ncc - 1 - i) if rev else i
        j_lat = (nlc - 1 - (i - ncc)) if rev else (i - ncc)
        return jnp.where(i < ncc, b * ncc + j_ctx, nctx // c + b * nlc + j_lat)

    return blk, ncc + nlc


def _decay_terms(q, k, g, mask, rev):
    c = q.shape[0]
    b = _dot2(mask.astype(bf16), g)
    last = 0 if rev else c - 1
    b_tot = b[last:last + 1, :]
    b_mid = b[c // 2:c // 2 + 1, :]
    qd = q * jnp.exp(jnp.minimum(b - b_mid, EXP_CLAMP))
    kd = k * jnp.exp(jnp.minimum(b_mid - b, EXP_CLAMP))
    qe = q * jnp.exp(b)
    ke = k * jnp.exp(b_tot - b)
    return qd.astype(bf16), kd.astype(bf16), qe.astype(bf16), ke.astype(bf16), jnp.exp(b_tot)


def _hgrn2_gates(z, lb):
    e = jnp.exp(-jnp.abs(z))
    inv = 1.0 / (1.0 + e)
    sig_pos = jnp.where(z >= 0, 1.0, e) * inv
    sig_neg = jnp.where(z >= 0, e, 1.0) * inv
    ls = jnp.minimum(z, 0.0) - jnp.log(1.0 + e)
    log_f = jnp.where(lb > 0.0, jnp.log(jnp.where(lb > 0.0, lb + (1.0 - lb) * sig_pos, 1.0)), ls)
    return (1.0 - lb) * sig_neg, log_f


def _hgrn2_kernel(qf_ref, zf_ref, vf_ref, qb_ref, zb_ref, vb_ref, lbf_ref, lbb_ref,
                  of_ref, ob_ref, sf_ref, sb_ref):
    @pl.when(pl.program_id(1) == 0)
    def _():
        sf_ref[...] = jnp.zeros_like(sf_ref)
        sb_ref[...] = jnp.zeros_like(sb_ref)

    c = qf_ref.shape[0]
    for rev, q_ref, z_ref, v_ref, lb_ref, o_ref, s_ref in (
            (False, qf_ref, zf_ref, vf_ref, lbf_ref, of_ref, sf_ref),
            (True, qb_ref, zb_ref, vb_ref, lbb_ref, ob_ref, sb_ref)):
        mask = _scan_mask(c, rev)
        q = q_ref[...].astype(f32)
        k, g = _hgrn2_gates(z_ref[...].astype(f32), lb_ref[...])
        v = v_ref[...]
        qd, kd, qe, ke, dtot = _decay_terms(q, k, g, mask, rev)
        for h in range(HG_HEADS):
            sl = slice(h * HG_DK, (h + 1) * HG_DK)
            st = s_ref[h]
            att = jnp.where(mask, _dot_nt(qd[:, sl], kd[:, sl]), 0.0)
            o = _dot(att.astype(bf16), v[:, sl]) + _dot_nt(qe[:, sl], st.astype(bf16))
            o_ref[:, sl] = o.astype(o_ref.dtype)
            s_ref[h] = st * dtot[:, sl] + _dot_tn(v[:, sl], ke[:, sl])


def _hgrn2_scan(p, lb_f, lb_b, geo, c):
    t = p.shape[0]
    nb = geo[2]
    blk_f, steps = _chunk_blocks(geo, c, False)
    blk_b, _ = _chunk_blocks(geo, c, True)
    w = HG_WIDTH
    c0 = P_HG // w

    def spec(blk, col):
        return pl.BlockSpec((c, w), lambda b, i: (blk(b, i), col))

    vec = pl.BlockSpec((1, w), lambda b, i: (0, 0))
    return pl.pallas_call(
        _hgrn2_kernel, grid=(nb, steps),
        in_specs=[spec(blk_f, c0), spec(blk_f, c0 + 1), spec(blk_f, c0 + 3),
                  spec(blk_b, c0), spec(blk_b, c0 + 2), spec(blk_b, c0 + 3), vec, vec],
        out_specs=[spec(blk_f, 0), spec(blk_b, 0)],
        out_shape=[jax.ShapeDtypeStruct((t, w), bf16)] * 2,
        scratch_shapes=[pltpu.VMEM((HG_HEADS, HG_DK, HG_DK), f32)] * 2,
        compiler_params=_cparams(("parallel", "arbitrary")), name="hgrn2_scan",
    )(p, p, p, p, p, p, lb_f.reshape(1, w), lb_b.reshape(1, w))


def _gla_kernel(qf_ref, kf_ref, vf_ref, sf_in_ref, qb_ref, kb_ref, vb_ref, sb_in_ref,
                gwf_ref, gwb_ref, gbf_ref, gbb_ref, of_ref, ob_ref, sf_ref, sb_ref):
    @pl.when(pl.program_id(1) == 0)
    def _():
        sf_ref[...] = jnp.zeros_like(sf_ref)
        sb_ref[...] = jnp.zeros_like(sb_ref)

    c = qf_ref.shape[0]
    npair = GLA_HEADS // 2
    lane = lax.broadcasted_iota(jnp.int32, (1, LANES), 1)
    vrow = lax.broadcasted_iota(jnp.int32, (2 * GLA_DV, LANES), 0)
    vcol = lax.broadcasted_iota(jnp.int32, (2 * GLA_DV, LANES), 1)
    block_diag = (vrow // GLA_DV) == (vcol // GLA_DK)
    for rev, q_ref, k_ref, v_ref, sm_ref, gw_ref, gb_ref, o_ref, s_ref in (
            (False, qf_ref, kf_ref, vf_ref, sf_in_ref, gwf_ref, gbf_ref, of_ref, sf_ref),
            (True, qb_ref, kb_ref, vb_ref, sb_in_ref, gwb_ref, gbb_ref, ob_ref, sb_ref)):
        mask = _scan_mask(c, rev)
        q = q_ref[...].astype(f32) * (GLA_DK ** -0.5)
        k = k_ref[...].astype(f32)
        v = v_ref[...]
        pre = jnp.dot(sm_ref[...], gw_ref[...], precision=HIGHEST, preferred_element_type=f32) + gb_ref[...]
        g = _log_sigmoid(pre) / GLA_TAU
        qd, kd, qe, ke, dtot = _decay_terms(q, k, g, mask, rev)
        for p in range(npair):
            sl = slice(p * LANES, (p + 1) * LANES)
            vsl = slice(p * 2 * GLA_DV, (p + 1) * 2 * GLA_DV)
            st = s_ref[p]
            o_inter = _dot_nt(qe[:, sl], st.astype(bf16))
            for hh in range(2):
                head = (lane // GLA_DK) == hh
                qh = jnp.where(head, qd[:, sl], jnp.zeros_like(qd[:, sl]))
                att = jnp.where(mask, _dot_nt(qh, kd[:, sl]), 0.0)
                osl = slice((2 * p + hh) * GLA_DV, (2 * p + hh + 1) * GLA_DV)
                o = _dot(att.astype(bf16), v[:, osl]) + o_inter[:, hh * GLA_DV:(hh + 1) * GLA_DV]
                o_ref[:, osl] = o.astype(o_ref.dtype)
            upd = st * dtot[:, sl] + _dot_tn(v[:, vsl], ke[:, sl])
            s_ref[p] = jnp.where(block_diag, upd, 0.0)


def _gla_scan(p, psmall, gate_w, gate_b, geo, c):
    t = p.shape[0]
    nb = geo[2]
    blk_f, steps = _chunk_blocks(geo, c, False)
    blk_b, _ = _chunk_blocks(geo, c, True)
    wk, wv = GLA_HEADS * GLA_DK, GLA_HEADS * GLA_DV

    def spec(blk, width, col):
        return pl.BlockSpec((c, width), lambda b, i: (blk(b, i), col))

    def const(shape):
        return pl.BlockSpec(shape, lambda b, i: (0, 0))

    gwf = jnp.zeros((LANES, wk), f32).at[S_LRF:S_LRF + GLA_RANK].set(gate_w[0])
    gwb = jnp.zeros((LANES, wk), f32).at[S_LRB:S_LRB + GLA_RANK].set(gate_w[1])
    qcol, kcol, vcol = P_GLA // wk, P_GLA // wk + 1, (P_GLA + 2 * wk) // wv
    return pl.pallas_call(
        _gla_kernel, grid=(nb, steps),
        in_specs=[spec(blk_f, wk, qcol), spec(blk_f, wk, kcol), spec(blk_f, wv, vcol), spec(blk_f, LANES, 0),
                  spec(blk_b, wk, qcol), spec(blk_b, wk, kcol), spec(blk_b, wv, vcol), spec(blk_b, LANES, 0),
                  const((LANES, wk)), const((LANES, wk)), const((1, wk)), const((1, wk))],
        out_specs=[spec(blk_f, wv, 0), spec(blk_b, wv, 0)],
        out_shape=[jax.ShapeDtypeStruct((t, wv), bf16)] * 2,
        scratch_shapes=[pltpu.VMEM((GLA_HEADS // 2, 2 * GLA_DV, 2 * GLA_DK), f32)] * 2,
        compiler_params=_cparams(("parallel", "arbitrary")), name="gla_scan",
    )(p, p, p, psmall, p, p, p, psmall, gwf, gwb, gate_b[0].reshape(1, wk), gate_b[1].reshape(1, wk))


HALO = SUBLANES


def _conv_kernel(prev_ref, cur_ref, next_ref, w_ref, b_ref, o_ref, ext_ref, *, geo, tc):
    nctx, seq, nb = geo
    r0 = pl.program_id(0) * tc
    in_ctx = r0 < nctx
    seg_len = jnp.where(in_ctx, nctx // nb, seq)
    pos = jnp.where(in_ctx, r0 % (nctx // nb), (r0 - nctx) % seq)
    first = pos == 0
    last = pos + tc == seg_len
    ext_ref[0:HALO, :] = jnp.where(first, 0.0, prev_ref[...].astype(f32))
    ext_ref[HALO:HALO + tc, :] = cur_ref[...].astype(f32)
    ext_ref[HALO + tc:HALO + tc + HALO, :] = jnp.where(last, 0.0, next_ref[...].astype(f32))
    acc = jnp.zeros(o_ref.shape, f32) + b_ref[...]
    for j in range(SSD_CONV):
        off = HALO - SSD_CONV // 2 + j
        acc = acc + ext_ref[off:off + tc, :] * w_ref[j:j + 1, :]
    o_ref[...] = _silu(acc).astype(o_ref.dtype)


def _ssd_conv(p, conv_w, conv_b, geo, tc):
    t = p.shape[0]
    nch = conv_w.shape[1]
    tw = 512
    col0 = (P_SSD + SSD_INNER) // tw
    hb = tc // HALO
    nhb = t // HALO
    return pl.pallas_call(
        functools.partial(_conv_kernel, geo=geo, tc=tc),
        grid=(t // tc, nch // tw),
        in_specs=[pl.BlockSpec((HALO, tw), lambda i, j: (jnp.maximum(i * hb - 1, 0), col0 + j)),
                  pl.BlockSpec((tc, tw), lambda i, j: (i, col0 + j)),
                  pl.BlockSpec((HALO, tw), lambda i, j: (jnp.minimum((i + 1) * hb, nhb - 1), col0 + j)),
                  pl.BlockSpec((SSD_CONV, tw), lambda i, j: (0, j)),
                  pl.BlockSpec((1, tw), lambda i, j: (0, j))],
        out_specs=pl.BlockSpec((tc, tw), lambda i, j: (i, j)),
        out_shape=jax.ShapeDtypeStruct((t, nch), bf16),
        scratch_shapes=[pltpu.VMEM((tc + 2 * HALO, tw), f32)],
        compiler_params=_cparams(("parallel", "parallel")), name="ssd_conv",
    )(p, p, p, conv_w, conv_b.reshape(1, nch))


def _ssd_kernel(xf_ref, bf_ref, cf_ref, smf_ref, smtf_ref, xb_ref, bb_ref, cb_ref, smb_ref, smtb_ref,
                arow_ref, brow_ref, acol_ref, bcol_ref, ef_ref, eb_ref,
                yf_ref, yb_ref, sf_ref, sb_ref):
    @pl.when(pl.program_id(1) == 0)
    def _():
        sf_ref[...] = jnp.zeros_like(sf_ref)
        sb_ref[...] = jnp.zeros_like(sb_ref)

    c = xf_ref.shape[0]
    gw = SSD_INNER // SSD_GROUPS
    lane = lax.broadcasted_iota(jnp.int32, (1, LANES), 1)
    for d, rev, x_ref, bm_ref, cm_ref, sm_ref, smt_ref, e_ref, y_ref, s_ref, lane0 in (
            (0, False, xf_ref, bf_ref, cf_ref, smf_ref, smtf_ref, ef_ref, yf_ref, sf_ref, S_DTF),
            (1, True, xb_ref, bb_ref, cb_ref, smb_ref, smtb_ref, eb_ref, yb_ref, sb_ref, S_DTB)):
        mask = _scan_mask(c, rev)
        mask_bf = mask.astype(bf16)
        expand = e_ref[...]
        dt = _softplus(sm_ref[...] + brow_ref[d:d + 1, :])
        cum = _dot2(mask_bf, dt * arow_ref[d:d + 1, :])
        last = 0 if rev else c - 1
        cum_end = cum[last:last + 1, :]
        dt_t = _softplus(smt_ref[...] + bcol_ref[d])
        cum_t = _dot2_l(dt_t * acol_ref[d], _scan_mask(c, not rev).astype(bf16))

        x = x_ref[...].astype(f32)
        xdt = x * _dot2_l(dt, expand)
        w = (xdt * _dot2_l(jnp.exp(cum_end - cum), expand)).astype(bf16)
        xdt = xdt.astype(bf16)
        dec_t = _dot2_l(jnp.exp(cum), expand)
        dec_end = _dot2_l(jnp.broadcast_to(jnp.exp(cum_end), (SUBLANES, LANES)), expand)[0:1, :]
        bm = bm_ref[...]
        cm = cm_ref[...]
        for g in range(SSD_GROUPS):
            gsl = slice(g * SSD_STATE, (g + 1) * SSD_STATE)
            hsl = slice(g * gw, (g + 1) * gw)
            st = s_ref[:, hsl]
            scores = _dot_nt(cm[:, gsl], bm[:, gsl])
            y_inter = _dot(cm[:, gsl], st.astype(bf16)) * dec_t[:, hsl]
            for tile in range(gw // LANES):
                tsl = slice(g * gw + tile * LANES, g * gw + (tile + 1) * LANES)
                xt = xdt[:, tsl]
                acc = y_inter[:, tile * LANES:(tile + 1) * LANES]
                for hh in range(LANES // SSD_HEADDIM):
                    h = (g * gw + tile * LANES) // SSD_HEADDIM + hh
                    diff = cum[:, lane0 + h:lane0 + h + 1] - cum_t[h:h + 1, :]
                    seg = jnp.where(mask, jnp.exp(jnp.minimum(diff, 0.0)), 0.0)
                    xh = jnp.where((lane // SSD_HEADDIM) == hh, xt, jnp.zeros_like(xt))
                    acc = acc + _dot((scores * seg).astype(bf16), xh)
                y_ref[:, tsl] = acc.astype(y_ref.dtype)
            s_ref[:, hsl] = st * dec_end[:, hsl] + _dot_tn(bm[:, gsl], w[:, hsl])


def _ssd_scan(xbc, psmall, psmall_t, a_log, dt_bias, geo, c):
    t = xbc.shape[0]
    nb = geo[2]
    blk_f, steps = _chunk_blocks(geo, c, False)
    blk_b, _ = _chunk_blocks(geo, c, True)
    wi, wn = SSD_INNER, SSD_GROUPS * SSD_STATE

    def spec(blk, width, col):
        return pl.BlockSpec((c, width), lambda b, i: (blk(b, i), col))

    def spec_t(blk, row):
        return pl.BlockSpec((SUBLANES, c), lambda b, i: (row, blk(b, i)))

    def const(shape):
        return pl.BlockSpec(shape, lambda b, i: (0,) * len(shape))

    a = -jnp.exp(a_log.astype(f32))
    arow = jnp.zeros((2, LANES), f32).at[0, S_DTF:S_DTF + SSD_HEADS].set(a[0]).at[1, S_DTB:S_DTB + SSD_HEADS].set(a[1])
    brow = jnp.zeros((2, LANES), f32).at[0, S_DTF:S_DTF + SSD_HEADS].set(dt_bias[0]).at[1, S_DTB:S_DTB + SSD_HEADS].set(dt_bias[1])
    head_of_lane = np.arange(wi) // SSD_HEADDIM
    ef = jnp.asarray((np.arange(LANES)[:, None] == S_DTF + head_of_lane[None, :]), bf16)
    eb = jnp.asarray((np.arange(LANES)[:, None] == S_DTB + head_of_lane[None, :]), bf16)
    xcol, bcol, ccol = 0, wi // wn, wi // wn + 1
    return pl.pallas_call(
        _ssd_kernel, grid=(nb, steps),
        in_specs=[spec(blk_f, wi, xcol), spec(blk_f, wn, bcol), spec(blk_f, wn, ccol), spec(blk_f, LANES, 0),
                  spec_t(blk_f, S_DTF // SUBLANES),
                  spec(blk_b, wi, xcol), spec(blk_b, wn, bcol), spec(blk_b, wn, ccol), spec(blk_b, LANES, 0),
                  spec_t(blk_b, S_DTB // SUBLANES),
                  const((2, LANES)), const((2, LANES)), const((2, SSD_HEADS, 1)), const((2, SSD_HEADS, 1)),
                  const((LANES, wi)), const((LANES, wi))],
        out_specs=[spec(blk_f, wi, 0), spec(blk_b, wi, 0)],
        out_shape=[jax.ShapeDtypeStruct((t, wi), bf16)] * 2,
        scratch_shapes=[pltpu.VMEM((SSD_STATE, wi), f32)] * 2,
        compiler_params=_cparams(("parallel", "arbitrary")), name="ssd_scan",
    )(xbc, xbc, xbc, psmall, psmall_t, xbc, xbc, xbc, psmall, psmall_t,
      arow, brow, a.reshape(2, SSD_HEADS, 1), dt_bias.astype(f32).reshape(2, SSD_HEADS, 1), ef, eb)


def _group_norm(o, w, width):
    parts = []
    for j in range(o.shape[1] // width):
        blk = o[:, j * width:(j + 1) * width]
        parts.append(blk * lax.rsqrt(jnp.mean(blk * blk, axis=-1, keepdims=True) + EPS))
    return jnp.concatenate(parts, axis=1) * w


def _merge_kernel(x_ref, hgf_ref, hgb_ref, hgg_ref, glf_ref, glb_ref, glg_ref,
                  sdf_ref, sdb_ref, sdx_ref, sdz_ref, gate_ref,
                  hgw_ref, glw_ref, sdd_ref, sdw_ref, wh_ref, wg_ref, ws_ref, wo_ref, m2_ref, o_ref):
    o_hg = _group_norm(hgf_ref[...].astype(f32) + hgb_ref[...].astype(f32), hgw_ref[...], HG_DK)
    o_hg = o_hg * _silu(hgg_ref[...].astype(f32))
    o_gl = _group_norm(glf_ref[...].astype(f32) + glb_ref[...].astype(f32), glw_ref[...], GLA_DV)
    o_gl = o_gl * _silu(glg_ref[...].astype(f32))
    y = sdf_ref[...].astype(f32) + sdb_ref[...].astype(f32) + sdd_ref[...] * sdx_ref[...].astype(f32)
    o_sd = _group_norm(y * _silu(sdz_ref[...].astype(f32)), sdw_ref[...], SSD_INNER // SSD_GROUPS)
    d = x_ref.shape[1]
    gate = jax.nn.sigmoid(gate_ref[...].astype(f32))
    y = (gate[:, 0:d] * _dot(o_hg.astype(bf16), wh_ref[...])
         + gate[:, d:2 * d] * _dot(o_gl.astype(bf16), wg_ref[...])
         + gate[:, 2 * d:3 * d] * _dot(o_sd.astype(bf16), ws_ref[...]))
    o_ref[...] = x_ref[...] + m2_ref[...] * _dot(y.astype(bf16), wo_ref[...])


def _merge(xs, p, hg_f, hg_b, gl_f, gl_b, sd_f, sd_b, xbc, hg_norm_w, gla_norm_w, ssd_d, ssd_norm_w,
           w_hg, w_gla, w_ssd, w_out, mod2, geo, tm):
    t, d = xs.shape
    w = 512

    def rows(width, col):
        return pl.BlockSpec((tm, width), lambda i: (i, col))

    def const(shape):
        return pl.BlockSpec(shape, lambda i: (0, 0))

    tile4 = lambda v: jnp.tile(v, w // v.shape[0]).reshape(1, w)
    return pl.pallas_call(
        _merge_kernel, grid=(t // tm,),
        in_specs=[rows(d, 0),
                  rows(w, 0), rows(w, 0), rows(w, (P_HG + 4 * w) // w),
                  rows(w, 0), rows(w, 0), rows(w, (P_GLA + 2 * w) // w),
                  rows(w, 0), rows(w, 0), rows(w, 0), rows(w, P_SSD // w),
                  rows(3 * d, P_GATE // (3 * d)),
                  const((1, w)), const((1, w)), const((1, w)), const((1, w)),
                  const((w, d)), const((w, d)), const((w, d)), const((d, d)),
                  _mod_block(d, tm, geo)],
        out_specs=rows(d, 0),
        out_shape=jax.ShapeDtypeStruct((t, d), f32),
        compiler_params=_cparams(("parallel",)), name="merge",
    )(xs, hg_f, hg_b, p, gl_f, gl_b, p, sd_f, sd_b, xbc, p, p,
      tile4(hg_norm_w), tile4(gla_norm_w), jnp.repeat(ssd_d, SSD_HEADDIM).reshape(1, w), ssd_norm_w.reshape(1, w),
      w_hg, w_gla, w_ssd, w_out, mod2)


def _expert_kernel(be_ref, x_ref, wg_ref, wu_ref, wd_ref, o_ref):
    x = x_ref[...]
    a = _silu(_dot(x, wg_ref[...])) * _dot(x, wu_ref[...])
    o_ref[...] = _dot(a.astype(bf16), wd_ref[...]).astype(o_ref.dtype)


def _experts(xg, block_e, w_gate, w_up, w_down):
    npad, d = xg.shape
    ff = w_gate.shape[2]
    nblk = npad // MOE_BLOCK
    grid_spec = pltpu.PrefetchScalarGridSpec(
        num_scalar_prefetch=1, grid=(nblk,),
        in_specs=[pl.BlockSpec((MOE_BLOCK, d), lambda i, be: (i, 0)),
                  pl.BlockSpec((None, d, ff), lambda i, be: (be[i], 0, 0)),
                  pl.BlockSpec((None, d, ff), lambda i, be: (be[i], 0, 0)),
                  pl.BlockSpec((None, ff, d), lambda i, be: (be[i], 0, 0))],
        out_specs=pl.BlockSpec((MOE_BLOCK, d), lambda i, be: (i, 0)))
    return pl.pallas_call(
        _expert_kernel, grid_spec=grid_spec,
        out_shape=jax.ShapeDtypeStruct((npad, d), bf16),
        compiler_params=_cparams(("arbitrary",)), name="experts",
    )(block_e, xg, w_gate, w_up, w_down)


def _ffn_out_kernel(x_ref, h_ref, y_ref, wr_ref, sg_ref, su_ref, sd_ref, m5_ref, o_ref):
    h = h_ref[...]
    a = _silu(_dot(h, sg_ref[...])) * _dot(h, su_ref[...])
    acc = _dot(a.astype(bf16), sd_ref[...])
    wr = wr_ref[...]
    for k in range(TOP_K):
        acc = acc + y_ref[k].astype(f32) * wr[:, k:k + 1]
    o_ref[...] = x_ref[...] + m5_ref[...] * acc


def _ffn_out(xs, h, yk, wrow, s_gate, s_up, s_down, mod5, geo, tm):
    t, d = xs.shape
    ff = s_gate.shape[1]
    rows = pl.BlockSpec((tm, d), lambda i: (i, 0))
    return pl.pallas_call(
        _ffn_out_kernel, grid=(t // tm,),
        in_specs=[rows, rows, pl.BlockSpec((TOP_K, tm, d), lambda i: (0, i, 0)),
                  pl.BlockSpec((tm, LANES), lambda i: (i, 0)),
                  pl.BlockSpec((d, ff), lambda i: (0, 0)), pl.BlockSpec((d, ff), lambda i: (0, 0)),
                  pl.BlockSpec((ff, d), lambda i: (0, 0)), _mod_block(d, tm, geo)],
        out_specs=rows,
        out_shape=jax.ShapeDtypeStruct((t, d), f32),
        compiler_params=_cparams(("parallel",)), name="ffn_out",
    )(xs, h, yk, wrow, s_gate, s_up, s_down, mod5)


def _dispatch(experts, rank, counts):
    t = experts.shape[1]
    n_assign = t * TOP_K
    counts = counts[:, 0].astype(jnp.int32)
    padded = (counts + MOE_BLOCK - 1) // MOE_BLOCK * MOE_BLOCK
    pad_end = jnp.cumsum(padded)
    pad_start = pad_end - padded
    onehot = experts[..., None] == jnp.arange(N_EXPERTS, dtype=jnp.int32)
    slot = rank + jnp.sum(jnp.where(onehot, pad_start, 0), axis=-1)
    n_blocks = (n_assign + MOE_BLOCK - 1) // MOE_BLOCK + N_EXPERTS
    token = jnp.broadcast_to(jnp.arange(t, dtype=jnp.int32), (TOP_K, t))
    buf_t = jnp.zeros((n_blocks * MOE_BLOCK,), jnp.int32).at[slot.reshape(-1)].set(token.reshape(-1))
    first_row = jnp.arange(n_blocks, dtype=jnp.int32) * MOE_BLOCK
    block_e = jnp.minimum(jnp.sum(pad_end[None, :] <= first_row[:, None], axis=1), N_EXPERTS - 1)
    return slot, buf_t, block_e.astype(jnp.int32)


def _final_norm_kernel(x_ref, w_ref, o_ref):
    o_ref[...] = _norm_rows(x_ref[...], w_ref[...])


def _final_norm(xs, w, nctx, tm):
    t, d = xs.shape
    off = nctx // tm
    return pl.pallas_call(
        _final_norm_kernel, grid=((t - nctx) // tm,),
        in_specs=[pl.BlockSpec((tm, d), lambda i: (i + off, 0)), pl.BlockSpec((1, d), lambda i: (0, 0))],
        out_specs=pl.BlockSpec((tm, d), lambda i: (i, 0)),
        out_shape=jax.ShapeDtypeStruct((t - nctx, d), f32),
        compiler_params=_cparams(("parallel",)), name="final_norm",
    )(xs, w.reshape(1, d))


def _grid_sincos(rows, cols, dim):
    quarter = dim // 4
    omega = 1.0 / (POS_BASE ** (jnp.arange(quarter, dtype=f32) / quarter))

    def axis_embed(n):
        ang = jnp.arange(n, dtype=f32)[:, None] * omega
        return jnp.concatenate([jnp.sin(ang), jnp.cos(ang)], axis=-1)

    er, ec = axis_embed(rows), axis_embed(cols)
    pe = jnp.concatenate([jnp.broadcast_to(er[:, None], (rows, cols, dim // 2)),
                          jnp.broadcast_to(ec[None], (rows, cols, dim // 2))], axis=-1)
    return pe.reshape(rows * cols, dim)


def _lower_bounds(logits):
    p = jax.nn.softmax(logits.astype(f32), axis=1)
    return jnp.cumsum(p, axis=1) - p[:, :1]


def _split_w_in(w_in):
    a0 = 5 * HG_WIDTH + 2 * GLA_HEADS * GLA_DK + 2 * GLA_HEADS * GLA_DV
    a1 = a0 + 2 * GLA_RANK
    b0 = a1 + 2 * SSD_INNER + 2 * SSD_GROUPS * SSD_STATE
    b1 = b0 + 2 * SSD_HEADS
    main = jnp.concatenate([w_in[..., b1:], w_in[..., :a0], w_in[..., a1:b0]], axis=-1).astype(bf16)
    small = jnp.concatenate([w_in[..., a0:a1], w_in[..., b0:b1]], axis=-1)
    small = jnp.pad(small, ((0, 0), (0, 0), (0, LANES - small.shape[-1]))).astype(bf16)
    return main, small


def kernel(x, c, ctx, c_ctx, norm_mix_w, norm_ffn_w, final_norm_w, ada_w, ada_b, w_in,
           hg_lb_logits, hg_norm_w, gla_gate_w, gla_gate_b, gla_norm_w, ssd_conv_w, ssd_conv_b,
           ssd_a_log, ssd_dt_bias, ssd_d, ssd_norm_w, w_br_hg, w_br_gla, w_br_ssd, w_out,
           router_w, router_bias, exp_w_gate, exp_w_up, exp_w_down, sh_w_gate, sh_w_up, sh_w_down):
    nb, seq, d = x.shape
    lc = ctx.shape[1]
    depth = w_in.shape[0]
    nctx = nb * lc
    geo = (nctx, seq, nb)
    tm = min(512, lc)
    c_vec = min(64, lc)
    c_ssd = min(128, lc)

    c_all = jnp.zeros((SUBLANES, d), f32).at[:nb].set(c).at[nb].set(c_ctx)
    mods = _mods(c_all, ada_w, ada_b)
    mods = mods.reshape(depth, SUBLANES, N_MOD, 1, d).transpose(0, 2, 1, 3, 4)

    pe = _grid_sincos(seq // GRID_W, GRID_W, d)
    xs = _embed(ctx.reshape(nctx, d), x.reshape(nb * seq, d), pe, tm)

    lb = _lower_bounds(hg_lb_logits)
    w_main, w_small = _split_w_in(w_in)
    w_hg, w_gla, w_ssd, w_o = (w.astype(bf16) for w in (w_br_hg, w_br_gla, w_br_ssd, w_out))
    e_gate, e_up, e_down = (w.astype(bf16) for w in (exp_w_gate, exp_w_up, exp_w_down))
    s_gate, s_up, s_down = (w.astype(bf16) for w in (sh_w_gate, sh_w_up, sh_w_down))

    for l in range(depth):
        m = mods[l]
        h = _norm_mod(xs, norm_mix_w[l], m[0], m[1], geo, tm)
        p = _matmul(h, w_main[l], tm, 512, bf16)
        psmall, psmall_t = _small_proj(h, w_small[l], tm)
        hg_f, hg_b = _hgrn2_scan(p, lb[0, l], lb[1, l], geo, c_vec)
        gl_f, gl_b = _gla_scan(p, psmall, gla_gate_w[:, l], gla_gate_b[:, l], geo, c_vec)
        xbc = _ssd_conv(p, ssd_conv_w[l], ssd_conv_b[l], geo, min(256, lc))
        sd_f, sd_b = _ssd_scan(xbc, psmall, psmall_t, ssd_a_log[:, l], ssd_dt_bias[:, l], geo, c_ssd)
        xs = _merge(xs, p, hg_f, hg_b, gl_f, gl_b, sd_f, sd_b, xbc, hg_norm_w[l], gla_norm_w[l], ssd_d[l],
                    ssd_norm_w[l], w_hg[l], w_gla[l], w_ssd[l], w_o[l], m[2], geo, min(256, lc))
        h2, experts, _, rank, wrow, counts = _router(xs, norm_ffn_w[l], m[3], m[4], router_w[l],
                                                     router_bias[l], geo, tm)
        slot, buf_t, block_e = _dispatch(experts, rank, counts)
        xg = jnp.take(h2, buf_t, axis=0)
        yg = _experts(xg, block_e, e_gate[l], e_up[l], e_down[l])
        yk = jnp.take(yg, slot.reshape(-1), axis=0).reshape(TOP_K, -1, d)
        xs = _ffn_out(xs, h2, yk, wrow, s_gate[l], s_up[l], s_down[l], m[5], geo, min(256, lc))

    out = _final_norm(xs, final_norm_w, nctx, tm)
    return out.reshape(nb, seq, d)
```

```python
import functools
import math

import numpy as np
import jax
import jax.numpy as jnp
from jax import lax
from jax.experimental import pallas as pl
from jax.experimental.pallas import tpu as pltpu

f32 = jnp.float32
bf16 = jnp.bfloat16
HIGHEST = lax.Precision.HIGHEST

EPS = 1e-6
POS_BASE = 10000.0
GRID_W = 64
N_MOD = 6
HG_HEADS = 4
HG_DK = 128
HG_WIDTH = HG_HEADS * HG_DK
GLA_HEADS = 4
GLA_DK = 64
GLA_DV = 128
GLA_RANK = 16
GLA_TAU = 16.0
SSD_HEADS = 8
SSD_HEADDIM = 64
SSD_INNER = SSD_HEADS * SSD_HEADDIM
SSD_GROUPS = 2
SSD_STATE = 128
SSD_CONV = 5
N_BRANCH = 3
N_EXPERTS = 64
TOP_K = 8
N_GROUPS = 8
TOPK_GROUPS = 4
ROUTE_SCALE = 2.5
MOE_BLOCK = 256
NEG_MASK = -1e4

LANES = 128
SUBLANES = 8
VMEM_LIMIT = 56 * 1024 * 1024
EXP_CLAMP = 80.0

P_GATE = 0
P_HG = 3072
P_GLA = 5632
P_SSD = 7168
P_TOTAL = 8704
S_LRF, S_LRB, S_DTF, S_DTB = 0, 16, 32, 40


def _cparams(sem):
    return pltpu.CompilerParams(dimension_semantics=sem, vmem_limit_bytes=VMEM_LIMIT)


def _split_hi_lo(x):
    hi = x.astype(bf16)
    lo = (x - hi.astype(f32)).astype(bf16)
    return hi, lo


def _dot(a, b):
    return jnp.dot(a, b, preferred_element_type=f32)


def _dot_nt(a, b):
    return lax.dot_general(a, b, (((1,), (1,)), ((), ())), preferred_element_type=f32)


def _dot_tn(a, b):
    return lax.dot_general(a, b, (((0,), (0,)), ((), ())), preferred_element_type=f32)


def _dot2(m, x):
    hi, lo = _split_hi_lo(x)
    return _dot(m, hi) + _dot(m, lo)


def _dot2_l(x, m):
    hi, lo = _split_hi_lo(x)
    return _dot(hi, m) + _dot(lo, m)


def _silu(x):
    return x * jax.nn.sigmoid(x)


def _log_sigmoid(z):
    return jnp.minimum(z, 0.0) - jnp.log(1.0 + jnp.exp(-jnp.abs(z)))


def _softplus(z):
    return jnp.maximum(z, 0.0) + jnp.log(1.0 + jnp.exp(-jnp.abs(z)))


def _scan_mask(c, rev):
    r = lax.broadcasted_iota(jnp.int32, (c, c), 0)
    s = lax.broadcasted_iota(jnp.int32, (c, c), 1)
    return (r <= s) if rev else (r >= s)


def _mods_kernel(c_ref, w_ref, b_ref, o_ref):
    o_ref[...] = jnp.dot(_silu(c_ref[...]), w_ref[...], precision=HIGHEST,
                         preferred_element_type=f32) + b_ref[...]


def _mods(c_all, ada_w, ada_b):
    depth, d, n = ada_w.shape
    tn = 1024
    return pl.pallas_call(
        _mods_kernel,
        grid=(depth, n // tn),
        in_specs=[pl.BlockSpec((SUBLANES, d), lambda l, j: (0, 0)),
                  pl.BlockSpec((None, d, tn), lambda l, j: (l, 0, j)),
                  pl.BlockSpec((None, 1, tn), lambda l, j: (l, 0, j))],
        out_specs=pl.BlockSpec((None, SUBLANES, tn), lambda l, j: (l, 0, j)),
        out_shape=jax.ShapeDtypeStruct((depth, SUBLANES, n), f32),
        compiler_params=_cparams(("parallel", "parallel")),
        name="mods",
    )(c_all, ada_w, ada_b.reshape(depth, 1, n))


def _embed_kernel(ctx_ref, x_ref, pe_ref, o_ref, *, nct):
    i = pl.program_id(0)

    @pl.when(i < nct)
    def _():
        o_ref[...] = ctx_ref[...]

    @pl.when(i >= nct)
    def _():
        o_ref[...] = x_ref[...] + pe_ref[...]


def _embed(ctx2, x2, pe, te):
    nctx, d = ctx2.shape
    nlat = x2.shape[0]
    nct, npe = nctx // te, pe.shape[0] // te
    return pl.pallas_call(
        functools.partial(_embed_kernel, nct=nct),
        grid=((nctx + nlat) // te,),
        in_specs=[pl.BlockSpec((te, d), lambda i: (jnp.minimum(i, nct - 1), 0)),
                  pl.BlockSpec((te, d), lambda i: (jnp.maximum(i - nct, 0), 0)),
                  pl.BlockSpec((te, d), lambda i: (jnp.maximum(i - nct, 0) % npe, 0))],
        out_specs=pl.BlockSpec((te, d), lambda i: (i, 0)),
        out_shape=jax.ShapeDtypeStruct((nctx + nlat, d), f32),
        compiler_params=_cparams(("parallel",)),
        name="embed",
    )(ctx2, x2, pe)


def _norm_rows(x, w):
    return x * lax.rsqrt(jnp.mean(x * x, axis=-1, keepdims=True) + EPS) * w


def _beats(a, b, a_first):
    return (a >= b) if a_first else (a > b)


def _router_kernel(x_ref, w_ref, sh_ref, sc_ref, rwt_ref, rb_ref,
                   h_ref, e_ref, wt_ref, rank_ref, wrow_ref, cnt_ref):
    @pl.when(pl.program_id(0) == 0)
    def _():
        cnt_ref[...] = jnp.zeros_like(cnt_ref)

    y = _norm_rows(x_ref[...], w_ref[...])
    h = y * (1.0 + sc_ref[...]) + sh_ref[...]
    h_ref[...] = h.astype(bf16)
    tm = h.shape[0]
    logits = lax.dot_general(rwt_ref[...], h, (((1,), (1,)), ((), ())), precision=HIGHEST,
                             preferred_element_type=f32)
    scores = jax.nn.sigmoid(logits)
    biased = scores + rb_ref[...]

    gsz = N_EXPERTS // N_GROUPS
    miota = lax.broadcasted_iota(jnp.int32, (gsz, tm), 0)
    blocks, gscore = [], []
    for g in range(N_GROUPS):
        blk = biased[g * gsz:(g + 1) * gsz, :]
        m1 = jnp.max(blk, axis=0, keepdims=True)
        i1 = jnp.min(jnp.where(blk == m1, miota, gsz), axis=0, keepdims=True)
        m2 = jnp.max(jnp.where(miota == i1, -jnp.inf, blk), axis=0, keepdims=True)
        blocks.append(blk)
        gscore.append(m1 + m2)
    vals = []
    for g in range(N_GROUPS):
        ahead = jnp.zeros((1, tm), jnp.int32)
        for o in range(N_GROUPS):
            if o != g:
                ahead = ahead + _beats(gscore[o], gscore[g], o < g).astype(jnp.int32)
        vals.append(jnp.where(ahead < TOPK_GROUPS, blocks[g], NEG_MASK))
    vals = jnp.concatenate(vals, axis=0)

    eiota = lax.broadcasted_iota(jnp.int32, (N_EXPERTS, tm), 0)
    member = jnp.zeros((N_EXPERTS, tm), f32)
    chosen, weights = [], []
    for k in range(TOP_K):
        m = jnp.max(vals, axis=0, keepdims=True)
        ei = jnp.min(jnp.where(vals == m, eiota, N_EXPERTS), axis=0, keepdims=True)
        sel = eiota == ei
        chosen.append(ei)
        weights.append(jnp.sum(jnp.where(sel, scores, 0.0), axis=0, keepdims=True))
        member = member + sel.astype(f32)
        vals = jnp.where(sel, -jnp.inf, vals)
    wsum = weights[0]
    for k in range(1, TOP_K):
        wsum = wsum + weights[k]

    r = lax.broadcasted_iota(jnp.int32, (tm, tm), 0)
    s = lax.broadcasted_iota(jnp.int32, (tm, tm), 1)
    member_bf = member.astype(bf16)
    rank_all = _dot(member_bf, (r < s).astype(bf16)) + cnt_ref[:, 0:1]
    for k in range(TOP_K):
        e_ref[k:k + 1, :] = chosen[k]
        wt_ref[k:k + 1, :] = weights[k] / wsum * ROUTE_SCALE
        rank_ref[k:k + 1, :] = jnp.sum(jnp.where(eiota == chosen[k], rank_all, 0.0), axis=0,
                                       keepdims=True).astype(jnp.int32)
    cnt_ref[...] = cnt_ref[...] + _dot(member_bf, jnp.ones((tm, LANES), bf16))
    eye = (lax.broadcasted_iota(jnp.int32, (TOP_K, LANES), 0)
           == lax.broadcasted_iota(jnp.int32, (TOP_K, LANES), 1)).astype(bf16)
    w_hi, w_lo = _split_hi_lo(wt_ref[...])
    wrow_ref[...] = _dot_tn(w_hi, eye) + _dot_tn(w_lo, eye)


def _mod_block(d, tm, geo):
    nctx, seq, nb = geo
    return pl.BlockSpec((None, 1, d), lambda i: (jnp.where(i * tm < nctx, nb, (i * tm - nctx) // seq), 0, 0))


def _router(xs, w, shift, scale, router_w, router_bias, geo, tm):
    t, d = xs.shape
    ne = router_w.shape[1]
    kt = pl.BlockSpec((TOP_K, tm), lambda i: (0, i))
    return pl.pallas_call(
        _router_kernel, grid=(t // tm,),
        in_specs=[pl.BlockSpec((tm, d), lambda i: (i, 0)),
                  pl.BlockSpec((1, d), lambda i: (0, 0)),
                  _mod_block(d, tm, geo), _mod_block(d, tm, geo),
                  pl.BlockSpec((ne, d), lambda i: (0, 0)),
                  pl.BlockSpec((ne, 1), lambda i: (0, 0))],
        out_specs=[pl.BlockSpec((tm, d), lambda i: (i, 0)), kt, kt, kt,
                   pl.BlockSpec((tm, LANES), lambda i: (i, 0)),
                   pl.BlockSpec((ne, LANES), lambda i: (0, 0))],
        out_shape=[jax.ShapeDtypeStruct((t, d), bf16),
                   jax.ShapeDtypeStruct((TOP_K, t), jnp.int32),
                   jax.ShapeDtypeStruct((TOP_K, t), f32),
                   jax.ShapeDtypeStruct((TOP_K, t), jnp.int32),
                   jax.ShapeDtypeStruct((t, LANES), f32),
                   jax.ShapeDtypeStruct((ne, LANES), f32)],
        compiler_params=_cparams(("arbitrary",)), name="router",
    )(xs, w.reshape(1, d), shift, scale, router_w.T, router_bias.astype(f32).reshape(ne, 1))


PROJ_TN = 512


def _in_proj_kernel(x_ref, nw_ref, sh_ref, sc_ref, w_ref, ws_ref, wst_ref, p_ref, ps_ref, pst_ref):
    y = _norm_rows(x_ref[...], nw_ref[...])
    h = (y * (1.0 + sc_ref[...]) + sh_ref[...]).astype(bf16)
    for j in range(p_ref.shape[1] // PROJ_TN):
        sl = slice(j * PROJ_TN, (j + 1) * PROJ_TN)
        p_ref[:, sl] = _dot(h, w_ref[:, sl]).astype(p_ref.dtype)
    ps_ref[...] = _dot(h, ws_ref[...])
    pst_ref[...] = _dot_nt(wst_ref[...], h)


def _in_proj(xs, nw, shift, scale, w_main, w_small, geo, tm):
    t, d = xs.shape
    n = w_main.shape[1]
    ns = w_small.shape[1]

    def resident(shape):
        return pl.BlockSpec(shape, lambda i: (0, 0), pipeline_mode=pl.Buffered(1))

    return pl.pallas_call(
        _in_proj_kernel, grid=(t // tm,),
        in_specs=[pl.BlockSpec((tm, d), lambda i: (i, 0)),
                  pl.BlockSpec((1, d), lambda i: (0, 0)),
                  _mod_block(d, tm, geo), _mod_block(d, tm, geo),
                  resident((d, n)), resident((d, ns)), resident((ns, d))],
        out_specs=[pl.BlockSpec((tm, n), lambda i: (i, 0)),
                   pl.BlockSpec((tm, ns), lambda i: (i, 0)),
                   pl.BlockSpec((ns, tm), lambda i: (0, i))],
        out_shape=[jax.ShapeDtypeStruct((t, n), bf16),
                   jax.ShapeDtypeStruct((t, ns), f32),
                   jax.ShapeDtypeStruct((ns, t), f32)],
        compiler_params=_cparams(("parallel",)), name="in_proj",
    )(xs, nw.reshape(1, d), shift, scale, w_main, w_small, w_small.T)


def _chunk_blocks(geo, c, rev):
    nctx, seq, nb = geo
    lc = nctx // nb
    ncc, nlc = lc // c, seq // c

    def blk(b, i):
        j_ctx = (ncc - 1 - i) if rev else i
        j_lat = (nlc - 1 - (i - ncc)) if rev else (i - ncc)
        return jnp.where(i < ncc, b * ncc + j_ctx, nctx // c + b * nlc + j_lat)

    return blk, ncc + nlc


def _sub_chunks(n_rows, c, rev):
    n = n_rows // c
    order = range(n - 1, -1, -1) if rev else range(n)
    return [slice(j * c, (j + 1) * c) for j in order]


def _decay_terms(q, k, g, mask, rev):
    c = q.shape[0]
    b = _dot2(mask.astype(bf16), g)
    last = 0 if rev else c - 1
    b_tot = b[last:last + 1, :]
    b_mid = b[c // 2:c // 2 + 1, :]
    qd = q * jnp.exp(jnp.minimum(b - b_mid, EXP_CLAMP))
    kd = k * jnp.exp(jnp.minimum(b_mid - b, EXP_CLAMP))
    qe = q * jnp.exp(b)
    ke = k * jnp.exp(b_tot - b)
    return qd.astype(bf16), kd.astype(bf16), qe.astype(bf16), ke.astype(bf16), jnp.exp(b_tot)


def _hgrn2_gates(z, lb):
    e = jnp.exp(-jnp.abs(z))
    inv = 1.0 / (1.0 + e)
    sig_pos = jnp.where(z >= 0, 1.0, e) * inv
    sig_neg = jnp.where(z >= 0, e, 1.0) * inv
    ls = jnp.minimum(z, 0.0) - jnp.log(1.0 + e)
    log_f = jnp.where(lb > 0.0, jnp.log(jnp.where(lb > 0.0, lb + (1.0 - lb) * sig_pos, 1.0)), ls)
    return (1.0 - lb) * sig_neg, log_f


def _hgrn2_kernel(qf_ref, zf_ref, vf_ref, qb_ref, zb_ref, vb_ref, lbf_ref, lbb_ref,
                  of_ref, ob_ref, sf_ref, sb_ref, *, c):
    @pl.when(pl.program_id(1) == 0)
    def _():
        sf_ref[...] = jnp.zeros_like(sf_ref)
        sb_ref[...] = jnp.zeros_like(sb_ref)

    for rev, q_ref, z_ref, v_ref, lb_ref, o_ref, s_ref in (
            (False, qf_ref, zf_ref, vf_ref, lbf_ref, of_ref, sf_ref),
            (True, qb_ref, zb_ref, vb_ref, lbb_ref, ob_ref, sb_ref)):
        mask = _scan_mask(c, rev)
        for rows in _sub_chunks(q_ref.shape[0], c, rev):
            q = q_ref[rows, :].astype(f32)
            k, g = _hgrn2_gates(z_ref[rows, :].astype(f32), lb_ref[...])
            v = v_ref[rows, :]
            qd, kd, qe, ke, dtot = _decay_terms(q, k, g, mask, rev)
            for h in range(HG_HEADS):
                sl = slice(h * HG_DK, (h + 1) * HG_DK)
                st = s_ref[h]
                att = jnp.where(mask, _dot_nt(qd[:, sl], kd[:, sl]), 0.0)
                o = _dot(att.astype(bf16), v[:, sl]) + _dot_nt(qe[:, sl], st.astype(bf16))
                o_ref[rows, sl] = o.astype(o_ref.dtype)
                s_ref[h] = st * dtot[:, sl] + _dot_tn(v[:, sl], ke[:, sl])


def _hgrn2_scan(p, lb_f, lb_b, geo, c, r):
    t = p.shape[0]
    nb = geo[2]
    blk_f, steps = _chunk_blocks(geo, r, False)
    blk_b, _ = _chunk_blocks(geo, r, True)
    w = HG_WIDTH
    c0 = P_HG // w

    def spec(blk, col):
        return pl.BlockSpec((r, w), lambda b, i: (blk(b, i), col))

    vec = pl.BlockSpec((1, w), lambda b, i: (0, 0))
    return pl.pallas_call(
        functools.partial(_hgrn2_kernel, c=c), grid=(nb, steps),
        in_specs=[spec(blk_f, c0), spec(blk_f, c0 + 1), spec(blk_f, c0 + 3),
                  spec(blk_b, c0), spec(blk_b, c0 + 2), spec(blk_b, c0 + 3), vec, vec],
        out_specs=[spec(blk_f, 0), spec(blk_b, 0)],
        out_shape=[jax.ShapeDtypeStruct((t, w), bf16)] * 2,
        scratch_shapes=[pltpu.VMEM((HG_HEADS, HG_DK, HG_DK), f32)] * 2,
        compiler_params=_cparams(("parallel", "arbitrary")), name="hgrn2_scan",
    )(p, p, p, p, p, p, lb_f.reshape(1, w), lb_b.reshape(1, w))


def _gla_kernel(qf_ref, kf_ref, vf_ref, sf_in_ref, qb_ref, kb_ref, vb_ref, sb_in_ref,
                gwf_ref, gwb_ref, gbf_ref, gbb_ref, of_ref, ob_ref, sf_ref, sb_ref, *, c):
    @pl.when(pl.program_id(1) == 0)
    def _():
        sf_ref[...] = jnp.zeros_like(sf_ref)
        sb_ref[...] = jnp.zeros_like(sb_ref)

    npair = GLA_HEADS // 2
    lane = lax.broadcasted_iota(jnp.int32, (1, LANES), 1)
    vrow = lax.broadcasted_iota(jnp.int32, (2 * GLA_DV, LANES), 0)
    vcol = lax.broadcasted_iota(jnp.int32, (2 * GLA_DV, LANES), 1)
    block_diag = (vrow // GLA_DV) == (vcol // GLA_DK)
    for rev, q_ref, k_ref, v_ref, sm_ref, gw_ref, gb_ref, o_ref, s_ref in (
            (False, qf_ref, kf_ref, vf_ref, sf_in_ref, gwf_ref, gbf_ref, of_ref, sf_ref),
            (True, qb_ref, kb_ref, vb_ref, sb_in_ref, gwb_ref, gbb_ref, ob_ref, sb_ref)):
        mask = _scan_mask(c, rev)
        for rows in _sub_chunks(q_ref.shape[0], c, rev):
            q = q_ref[rows, :].astype(f32) * (GLA_DK ** -0.5)
            k = k_ref[rows, :].astype(f32)
            v = v_ref[rows, :]
            pre = jnp.dot(sm_ref[rows, :], gw_ref[...], precision=HIGHEST,
                          preferred_element_type=f32) + gb_ref[...]
            g = _log_sigmoid(pre) / GLA_TAU
            qd, kd, qe, ke, dtot = _decay_terms(q, k, g, mask, rev)
            for p in range(npair):
                sl = slice(p * LANES, (p + 1) * LANES)
                vsl = slice(p * 2 * GLA_DV, (p + 1) * 2 * GLA_DV)
                st = s_ref[p]
                o_inter = _dot_nt(qe[:, sl], st.astype(bf16))
                for hh in range(2):
                    head = (lane // GLA_DK) == hh
                    qh = jnp.where(head, qd[:, sl], jnp.zeros_like(qd[:, sl]))
                    att = jnp.where(mask, _dot_nt(qh, kd[:, sl]), 0.0)
                    osl = slice((2 * p + hh) * GLA_DV, (2 * p + hh + 1) * GLA_DV)
                    o = _dot(att.astype(bf16), v[:, osl]) + o_inter[:, hh * GLA_DV:(hh + 1) * GLA_DV]
                    o_ref[rows, osl] = o.astype(o_ref.dtype)
                upd = st * dtot[:, sl] + _dot_tn(v[:, vsl], ke[:, sl])
                s_ref[p] = jnp.where(block_diag, upd, 0.0)


def _gla_scan(p, psmall, gate_w, gate_b, geo, c, r):
    t = p.shape[0]
    nb = geo[2]
    blk_f, steps = _chunk_blocks(geo, r, False)
    blk_b, _ = _chunk_blocks(geo, r, True)
    wk, wv = GLA_HEADS * GLA_DK, GLA_HEADS * GLA_DV

    def spec(blk, width, col):
        return pl.BlockSpec((r, width), lambda b, i: (blk(b, i), col))

    def const(shape):
        return pl.BlockSpec(shape, lambda b, i: (0, 0))

    gwf = jnp.zeros((LANES, wk), f32).at[S_LRF:S_LRF + GLA_RANK].set(gate_w[0])
    gwb = jnp.zeros((LANES, wk), f32).at[S_LRB:S_LRB + GLA_RANK].set(gate_w[1])
    qcol, kcol, vcol = P_GLA // wk, P_GLA // wk + 1, (P_GLA + 2 * wk) // wv
    return pl.pallas_call(
        functools.partial(_gla_kernel, c=c), grid=(nb, steps),
        in_specs=[spec(blk_f, wk, qcol), spec(blk_f, wk, kcol), spec(blk_f, wv, vcol), spec(blk_f, LANES, 0),
                  spec(blk_b, wk, qcol), spec(blk_b, wk, kcol), spec(blk_b, wv, vcol), spec(blk_b, LANES, 0),
                  const((LANES, wk)), const((LANES, wk)), const((1, wk)), const((1, wk))],
        out_specs=[spec(blk_f, wv, 0), spec(blk_b, wv, 0)],
        out_shape=[jax.ShapeDtypeStruct((t, wv), bf16)] * 2,
        scratch_shapes=[pltpu.VMEM((GLA_HEADS // 2, 2 * GLA_DV, 2 * GLA_DK), f32)] * 2,
        compiler_params=_cparams(("parallel", "arbitrary")), name="gla_scan",
    )(p, p, p, psmall, p, p, p, psmall, gwf, gwb, gate_b[0].reshape(1, wk), gate_b[1].reshape(1, wk))


HALO = SUBLANES


def _conv_kernel(prev_ref, cur_ref, next_ref, w_ref, b_ref, o_ref, ext_ref, *, geo, tc):
    nctx, seq, nb = geo
    r0 = pl.program_id(0) * tc
    in_ctx = r0 < nctx
    seg_len = jnp.where(in_ctx, nctx // nb, seq)
    pos = jnp.where(in_ctx, r0 % (nctx // nb), (r0 - nctx) % seq)
    first = pos == 0
    last = pos + tc == seg_len
    ext_ref[0:HALO, :] = jnp.where(first, 0.0, prev_ref[...].astype(f32))
    ext_ref[HALO:HALO + tc, :] = cur_ref[...].astype(f32)
    ext_ref[HALO + tc:HALO + tc + HALO, :] = jnp.where(last, 0.0, next_ref[...].astype(f32))
    acc = jnp.zeros(o_ref.shape, f32) + b_ref[...]
    for j in range(SSD_CONV):
        off = HALO - SSD_CONV // 2 + j
        acc = acc + ext_ref[off:off + tc, :] * w_ref[j:j + 1, :]
    o_ref[...] = _silu(acc).astype(o_ref.dtype)


def _ssd_conv(p, conv_w, conv_b, geo, tc):
    t = p.shape[0]
    nch = conv_w.shape[1]
    tw = 512
    col0 = (P_SSD + SSD_INNER) // tw
    hb = tc // HALO
    nhb = t // HALO
    return pl.pallas_call(
        functools.partial(_conv_kernel, geo=geo, tc=tc),
        grid=(t // tc, nch // tw),
        in_specs=[pl.BlockSpec((HALO, tw), lambda i, j: (jnp.maximum(i * hb - 1, 0), col0 + j)),
                  pl.BlockSpec((tc, tw), lambda i, j: (i, col0 + j)),
                  pl.BlockSpec((HALO, tw), lambda i, j: (jnp.minimum((i + 1) * hb, nhb - 1), col0 + j)),
                  pl.BlockSpec((SSD_CONV, tw), lambda i, j: (0, j)),
                  pl.BlockSpec((1, tw), lambda i, j: (0, j))],
        out_specs=pl.BlockSpec((tc, tw), lambda i, j: (i, j)),
        out_shape=jax.ShapeDtypeStruct((t, nch), bf16),
        scratch_shapes=[pltpu.VMEM((tc + 2 * HALO, tw), f32)],
        compiler_params=_cparams(("parallel", "parallel")), name="ssd_conv",
    )(p, p, p, conv_w, conv_b.reshape(1, nch))


def _ssd_kernel(xf_ref, bf_ref, cf_ref, smf_ref, smtf_ref, xb_ref, bb_ref, cb_ref, smb_ref, smtb_ref,
                arow_ref, brow_ref, acol_ref, bcol_ref, ef_ref, eb_ref,
                yf_ref, yb_ref, sf_ref, sb_ref, *, c):
    @pl.when(pl.program_id(1) == 0)
    def _():
        sf_ref[...] = jnp.zeros_like(sf_ref)
        sb_ref[...] = jnp.zeros_like(sb_ref)

    gw = SSD_INNER // SSD_GROUPS
    lane = lax.broadcasted_iota(jnp.int32, (1, LANES), 1)
    for d, rev, x_ref, bm_ref, cm_ref, sm_ref, smt_ref, e_ref, y_ref, s_ref, lane0 in (
            (0, False, xf_ref, bf_ref, cf_ref, smf_ref, smtf_ref, ef_ref, yf_ref, sf_ref, S_DTF),
            (1, True, xb_ref, bb_ref, cb_ref, smb_ref, smtb_ref, eb_ref, yb_ref, sb_ref, S_DTB)):
        mask = _scan_mask(c, rev)
        mask_bf = mask.astype(bf16)
        mask_t_bf = _scan_mask(c, not rev).astype(bf16)
        expand = e_ref[...]
        for rows in _sub_chunks(x_ref.shape[0], c, rev):
            dt = _softplus(sm_ref[rows, :] + brow_ref[d:d + 1, :])
            cum = _dot2(mask_bf, dt * arow_ref[d:d + 1, :])
            last = 0 if rev else c - 1
            cum_end = cum[last:last + 1, :]
            dt_t = _softplus(smt_ref[:, rows] + bcol_ref[d])
            cum_t = _dot2_l(dt_t * acol_ref[d], mask_t_bf)

            x = x_ref[rows, :].astype(f32)
            xdt = x * _dot2_l(dt, expand)
            w = (xdt * _dot2_l(jnp.exp(cum_end - cum), expand)).astype(bf16)
            xdt = xdt.astype(bf16)
            dec_t = _dot2_l(jnp.exp(cum), expand)
            dec_end = _dot2_l(jnp.broadcast_to(jnp.exp(cum_end), (SUBLANES, LANES)), expand)[0:1, :]
            bm = bm_ref[rows, :]
            cm = cm_ref[rows, :]
            for g in range(SSD_GROUPS):
                gsl = slice(g * SSD_STATE, (g + 1) * SSD_STATE)
                hsl = slice(g * gw, (g + 1) * gw)
                st = s_ref[:, hsl]
                scores = _dot_nt(cm[:, gsl], bm[:, gsl])
                y_inter = _dot(cm[:, gsl], st.astype(bf16)) * dec_t[:, hsl]
                for tile in range(gw // LANES):
                    tsl = slice(g * gw + tile * LANES, g * gw + (tile + 1) * LANES)
                    xt = xdt[:, tsl]
                    acc = y_inter[:, tile * LANES:(tile + 1) * LANES]
                    for hh in range(LANES // SSD_HEADDIM):
                        h = (g * gw + tile * LANES) // SSD_HEADDIM + hh
                        diff = cum[:, lane0 + h:lane0 + h + 1] - cum_t[h:h + 1, :]
                        seg = jnp.where(mask, jnp.exp(jnp.minimum(diff, 0.0)), 0.0)
                        xh = jnp.where((lane // SSD_HEADDIM) == hh, xt, jnp.zeros_like(xt))
                        acc = acc + _dot((scores * seg).astype(bf16), xh)
                    y_ref[rows, tsl] = acc.astype(y_ref.dtype)
                s_ref[:, hsl] = st * dec_end[:, hsl] + _dot_tn(bm[:, gsl], w[:, hsl])


def _ssd_scan(xbc, psmall, psmall_t, a_log, dt_bias, geo, c, r):
    t = xbc.shape[0]
    nb = geo[2]
    blk_f, steps = _chunk_blocks(geo, r, False)
    blk_b, _ = _chunk_blocks(geo, r, True)
    wi, wn = SSD_INNER, SSD_GROUPS * SSD_STATE

    def spec(blk, width, col):
        return pl.BlockSpec((r, width), lambda b, i: (blk(b, i), col))

    def spec_t(blk, row):
        return pl.BlockSpec((SUBLANES, r), lambda b, i: (row, blk(b, i)))

    def const(shape):
        return pl.BlockSpec(shape, lambda b, i: (0,) * len(shape))

    a = -jnp.exp(a_log.astype(f32))
    arow = jnp.zeros((2, LANES), f32).at[0, S_DTF:S_DTF + SSD_HEADS].set(a[0]).at[1, S_DTB:S_DTB + SSD_HEADS].set(a[1])
    brow = jnp.zeros((2, LANES), f32).at[0, S_DTF:S_DTF + SSD_HEADS].set(dt_bias[0]).at[1, S_DTB:S_DTB + SSD_HEADS].set(dt_bias[1])
    head_of_lane = np.arange(wi) // SSD_HEADDIM
    ef = jnp.asarray((np.arange(LANES)[:, None] == S_DTF + head_of_lane[None, :]), bf16)
    eb = jnp.asarray((np.arange(LANES)[:, None] == S_DTB + head_of_lane[None, :]), bf16)
    xcol, bcol, ccol = 0, wi // wn, wi // wn + 1
    return pl.pallas_call(
        functools.partial(_ssd_kernel, c=c), grid=(nb, steps),
        in_specs=[spec(blk_f, wi, xcol), spec(blk_f, wn, bcol), spec(blk_f, wn, ccol), spec(blk_f, LANES, 0),
                  spec_t(blk_f, S_DTF // SUBLANES),
                  spec(blk_b, wi, xcol), spec(blk_b, wn, bcol), spec(blk_b, wn, ccol), spec(blk_b, LANES, 0),
                  spec_t(blk_b, S_DTB // SUBLANES),
                  const((2, LANES)), const((2, LANES)), const((2, SSD_HEADS, 1)), const((2, SSD_HEADS, 1)),
                  const((LANES, wi)), const((LANES, wi))],
        out_specs=[spec(blk_f, wi, 0), spec(blk_b, wi, 0)],
        out_shape=[jax.ShapeDtypeStruct((t, wi), bf16)] * 2,
        scratch_shapes=[pltpu.VMEM((SSD_STATE, wi), f32)] * 2,
        compiler_params=_cparams(("parallel", "arbitrary")), name="ssd_scan",
    )(xbc, xbc, xbc, psmall, psmall_t, xbc, xbc, xbc, psmall, psmall_t,
      arow, brow, a.reshape(2, SSD_HEADS, 1), dt_bias.astype(f32).reshape(2, SSD_HEADS, 1), ef, eb)


def _group_norm(o, w, width):
    parts = []
    for j in range(o.shape[1] // width):
        blk = o[:, j * width:(j + 1) * width]
        parts.append(blk * lax.rsqrt(jnp.mean(blk * blk, axis=-1, keepdims=True) + EPS))
    return jnp.concatenate(parts, axis=1) * w


def _merge_kernel(x_ref, hgf_ref, hgb_ref, hgg_ref, glf_ref, glb_ref, glg_ref,
                  sdf_ref, sdb_ref, sdx_ref, sdz_ref, gate_ref,
                  hgw_ref, glw_ref, sdd_ref, sdw_ref, wh_ref, wg_ref, ws_ref, wo_ref, m2_ref, o_ref):
    o_hg = _group_norm(hgf_ref[...].astype(f32) + hgb_ref[...].astype(f32), hgw_ref[...], HG_DK)
    o_hg = o_hg * _silu(hgg_ref[...].astype(f32))
    o_gl = _group_norm(glf_ref[...].astype(f32) + glb_ref[...].astype(f32), glw_ref[...], GLA_DV)
    o_gl = o_gl * _silu(glg_ref[...].astype(f32))
    y = sdf_ref[...].astype(f32) + sdb_ref[...].astype(f32) + sdd_ref[...] * sdx_ref[...].astype(f32)
    o_sd = _group_norm(y * _silu(sdz_ref[...].astype(f32)), sdw_ref[...], SSD_INNER // SSD_GROUPS)
    d = x_ref.shape[1]
    gate = jax.nn.sigmoid(gate_ref[...].astype(f32))
    y = (gate[:, 0:d] * _dot(o_hg.astype(bf16), wh_ref[...])
         + gate[:, d:2 * d] * _dot(o_gl.astype(bf16), wg_ref[...])
         + gate[:, 2 * d:3 * d] * _dot(o_sd.astype(bf16), ws_ref[...]))
    o_ref[...] = x_ref[...] + m2_ref[...] * _dot(y.astype(bf16), wo_ref[...])


def _merge(xs, p, hg_f, hg_b, gl_f, gl_b, sd_f, sd_b, xbc, hg_norm_w, gla_norm_w, ssd_d, ssd_norm_w,
           w_hg, w_gla, w_ssd, w_out, mod2, geo, tm):
    t, d = xs.shape
    w = 512

    def rows(width, col):
        return pl.BlockSpec((tm, width), lambda i: (i, col))

    def const(shape):
        return pl.BlockSpec(shape, lambda i: (0, 0))

    tile4 = lambda v: jnp.tile(v, w // v.shape[0]).reshape(1, w)
    return pl.pallas_call(
        _merge_kernel, grid=(t // tm,),
        in_specs=[rows(d, 0),
                  rows(w, 0), rows(w, 0), rows(w, (P_HG + 4 * w) // w),
                  rows(w, 0), rows(w, 0), rows(w, (P_GLA + 2 * w) // w),
                  rows(w, 0), rows(w, 0), rows(w, 0), rows(w, P_SSD // w),
                  rows(3 * d, P_GATE // (3 * d)),
                  const((1, w)), const((1, w)), const((1, w)), const((1, w)),
                  const((w, d)), const((w, d)), const((w, d)), const((d, d)),
                  _mod_block(d, tm, geo)],
        out_specs=rows(d, 0),
        out_shape=jax.ShapeDtypeStruct((t, d), f32),
        compiler_params=_cparams(("parallel",)), name="merge",
    )(xs, hg_f, hg_b, p, gl_f, gl_b, p, sd_f, sd_b, xbc, p, p,
      tile4(hg_norm_w), tile4(gla_norm_w), jnp.repeat(ssd_d, SSD_HEADDIM).reshape(1, w), ssd_norm_w.reshape(1, w),
      w_hg, w_gla, w_ssd, w_out, mod2)


MOE_SUB = 4


def _expert_kernel(be_ref, x_ref, *refs):
    o_ref = refs[-1]
    for s in range(MOE_SUB):
        wg_ref, wu_ref, wd_ref = refs[3 * s:3 * s + 3]
        rows = slice(s * MOE_BLOCK, (s + 1) * MOE_BLOCK)
        x = x_ref[rows, :]
        a = _silu(_dot(x, wg_ref[...])) * _dot(x, wu_ref[...])
        o_ref[rows, :] = _dot(a.astype(bf16), wd_ref[...]).astype(o_ref.dtype)


def _experts(xg, block_e, w_gate, w_up, w_down):
    npad, d = xg.shape
    ff = w_gate.shape[2]
    rows = MOE_SUB * MOE_BLOCK
    w_specs = []
    for s in range(MOE_SUB):
        pick = lambda i, be, s=s: (be[i * MOE_SUB + s], 0, 0)
        w_specs += [pl.BlockSpec((None, d, ff), pick), pl.BlockSpec((None, d, ff), pick),
                    pl.BlockSpec((None, ff, d), pick)]
    grid_spec = pltpu.PrefetchScalarGridSpec(
        num_scalar_prefetch=1, grid=(npad // rows,),
        in_specs=[pl.BlockSpec((rows, d), lambda i, be: (i, 0))] + w_specs,
        out_specs=pl.BlockSpec((rows, d), lambda i, be: (i, 0)))
    return pl.pallas_call(
        _expert_kernel, grid_spec=grid_spec,
        out_shape=jax.ShapeDtypeStruct((npad, d), bf16),
        compiler_params=_cparams(("arbitrary",)), name="experts",
    )(block_e, xg, *([w_gate, w_up, w_down] * MOE_SUB))


def _slots_kernel(ps_ref, e_ref, r_ref, o_ref):
    e = e_ref[...]
    acc = r_ref[...]
    for j in range(N_EXPERTS):
        acc = acc + jnp.where(e == j, ps_ref[j], 0)
    o_ref[...] = acc


def _slots(experts, rank, pad_start, tl):
    k, t = experts.shape
    spec = pl.BlockSpec((k, tl), lambda i, ps: (0, i))
    return pl.pallas_call(
        _slots_kernel,
        grid_spec=pltpu.PrefetchScalarGridSpec(num_scalar_prefetch=1, grid=(t // tl,),
                                               in_specs=[spec, spec], out_specs=spec),
        out_shape=jax.ShapeDtypeStruct((k, t), jnp.int32),
        compiler_params=_cparams(("parallel",)), name="slots",
    )(pad_start, experts, rank)


def _ffn_out_kernel(x_ref, h_ref, y_ref, wr_ref, sg_ref, su_ref, sd_ref, m5_ref, o_ref):
    h = h_ref[...]
    a = _silu(_dot(h, sg_ref[...])) * _dot(h, su_ref[...])
    acc = _dot(a.astype(bf16), sd_ref[...])
    wr = wr_ref[...]
    for k in range(TOP_K):
        acc = acc + y_ref[k].astype(f32) * wr[:, k:k + 1]
    o_ref[...] = x_ref[...] + m5_ref[...] * acc


def _ffn_out(xs, h, yk, wrow, s_gate, s_up, s_down, mod5, geo, tm):
    t, d = xs.shape
    ff = s_gate.shape[1]
    rows = pl.BlockSpec((tm, d), lambda i: (i, 0))
    return pl.pallas_call(
        _ffn_out_kernel, grid=(t // tm,),
        in_specs=[rows, rows, pl.BlockSpec((TOP_K, tm, d), lambda i: (0, i, 0)),
                  pl.BlockSpec((tm, LANES), lambda i: (i, 0)),
                  pl.BlockSpec((d, ff), lambda i: (0, 0)), pl.BlockSpec((d, ff), lambda i: (0, 0)),
                  pl.BlockSpec((ff, d), lambda i: (0, 0)), _mod_block(d, tm, geo)],
        out_specs=rows,
        out_shape=jax.ShapeDtypeStruct((t, d), f32),
        compiler_params=_cparams(("parallel",)), name="ffn_out",
    )(xs, h, yk, wrow, s_gate, s_up, s_down, mod5)


def _dispatch(experts, rank, counts, tl):
    t = experts.shape[1]
    n_assign = t * TOP_K
    counts = counts[:, 0].astype(jnp.int32)
    padded = (counts + MOE_BLOCK - 1) // MOE_BLOCK * MOE_BLOCK
    pad_end = jnp.cumsum(padded)
    pad_start = (pad_end - padded).astype(jnp.int32)
    slot = _slots(experts, rank, pad_start, tl)
    n_blocks = (n_assign + MOE_BLOCK - 1) // MOE_BLOCK + N_EXPERTS
    n_blocks = (n_blocks + MOE_SUB - 1) // MOE_SUB * MOE_SUB
    token = jnp.broadcast_to(jnp.arange(t, dtype=jnp.int32), (TOP_K, t))
    buf_t = jnp.zeros((n_blocks * MOE_BLOCK,), jnp.int32).at[slot.reshape(-1)].add(
        token.reshape(-1), unique_indices=True)
    first_row = jnp.arange(n_blocks, dtype=jnp.int32) * MOE_BLOCK
    block_e = jnp.minimum(jnp.sum(pad_end[None, :] <= first_row[:, None], axis=1), N_EXPERTS - 1)
    return slot, buf_t, block_e.astype(jnp.int32)


def _final_norm_kernel(x_ref, w_ref, o_ref):
    o_ref[...] = _norm_rows(x_ref[...], w_ref[...])


def _final_norm(xs, w, nctx, tm):
    t, d = xs.shape
    off = nctx // tm
    return pl.pallas_call(
        _final_norm_kernel, grid=((t - nctx) // tm,),
        in_specs=[pl.BlockSpec((tm, d), lambda i: (i + off, 0)), pl.BlockSpec((1, d), lambda i: (0, 0))],
        out_specs=pl.BlockSpec((tm, d), lambda i: (i, 0)),
        out_shape=jax.ShapeDtypeStruct((t - nctx, d), f32),
        compiler_params=_cparams(("parallel",)), name="final_norm",
    )(xs, w.reshape(1, d))


def _grid_sincos(rows, cols, dim):
    quarter = dim // 4
    omega = 1.0 / (POS_BASE ** (jnp.arange(quarter, dtype=f32) / quarter))

    def axis_embed(n):
        ang = jnp.arange(n, dtype=f32)[:, None] * omega
        return jnp.concatenate([jnp.sin(ang), jnp.cos(ang)], axis=-1)

    er, ec = axis_embed(rows), axis_embed(cols)
    pe = jnp.concatenate([jnp.broadcast_to(er[:, None], (rows, cols, dim // 2)),
                          jnp.broadcast_to(ec[None], (rows, cols, dim // 2))], axis=-1)
    return pe.reshape(rows * cols, dim)


def _lower_bounds(logits):
    p = jax.nn.softmax(logits.astype(f32), axis=1)
    return jnp.cumsum(p, axis=1) - p[:, :1]


def _split_w_in(w_in):
    a0 = 5 * HG_WIDTH + 2 * GLA_HEADS * GLA_DK + 2 * GLA_HEADS * GLA_DV
    a1 = a0 + 2 * GLA_RANK
    b0 = a1 + 2 * SSD_INNER + 2 * SSD_GROUPS * SSD_STATE
    b1 = b0 + 2 * SSD_HEADS
    main = jnp.concatenate([w_in[..., b1:], w_in[..., :a0], w_in[..., a1:b0]], axis=-1).astype(bf16)
    small = jnp.concatenate([w_in[..., a0:a1], w_in[..., b0:b1]], axis=-1)
    small = jnp.pad(small, ((0, 0), (0, 0), (0, LANES - small.shape[-1]))).astype(bf16)
    return main, small


def kernel(x, c, ctx, c_ctx, norm_mix_w, norm_ffn_w, final_norm_w, ada_w, ada_b, w_in,
           hg_lb_logits, hg_norm_w, gla_gate_w, gla_gate_b, gla_norm_w, ssd_conv_w, ssd_conv_b,
           ssd_a_log, ssd_dt_bias, ssd_d, ssd_norm_w, w_br_hg, w_br_gla, w_br_ssd, w_out,
           router_w, router_bias, exp_w_gate, exp_w_up, exp_w_down, sh_w_gate, sh_w_up, sh_w_down):
    nb, seq, d = x.shape
    lc = ctx.shape[1]
    depth = w_in.shape[0]
    nctx = nb * lc
    geo = (nctx, seq, nb)
    tm = min(512, lc)
    r_scan = min(256, lc)
    c_hg = min(64, lc)
    c_gla = min(128, lc)
    c_ssd = min(128, lc)

    c_all = jnp.zeros((SUBLANES, d), f32).at[:nb].set(c).at[nb].set(c_ctx)
    mods = _mods(c_all, ada_w, ada_b)
    mods = mods.reshape(depth, SUBLANES, N_MOD, 1, d).transpose(0, 2, 1, 3, 4)

    pe = _grid_sincos(seq // GRID_W, GRID_W, d)
    xs = _embed(ctx.reshape(nctx, d), x.reshape(nb * seq, d), pe, tm)

    lb = _lower_bounds(hg_lb_logits)
    w_main, w_small = _split_w_in(w_in)
    w_hg, w_gla, w_ssd, w_o = (w.astype(bf16) for w in (w_br_hg, w_br_gla, w_br_ssd, w_out))
    e_gate, e_up, e_down = (w.astype(bf16) for w in (exp_w_gate, exp_w_up, exp_w_down))
    s_gate, s_up, s_down = (w.astype(bf16) for w in (sh_w_gate, sh_w_up, sh_w_down))

    for l in range(depth):
        m = mods[l]
        p, psmall, psmall_t = _in_proj(xs, norm_mix_w[l], m[0], m[1], w_main[l], w_small[l], geo, tm)
        hg_f, hg_b = _hgrn2_scan(p, lb[0, l], lb[1, l], geo, c_hg, r_scan)
        gl_f, gl_b = _gla_scan(p, psmall, gla_gate_w[:, l], gla_gate_b[:, l], geo, c_gla, r_scan)
        xbc = _ssd_conv(p, ssd_conv_w[l], ssd_conv_b[l], geo, min(256, lc))
        sd_f, sd_b = _ssd_scan(xbc, psmall, psmall_t, ssd_a_log[:, l], ssd_dt_bias[:, l], geo, c_ssd, r_scan)
        xs = _merge(xs, p, hg_f, hg_b, gl_f, gl_b, sd_f, sd_b, xbc, hg_norm_w[l], gla_norm_w[l], ssd_d[l],
                    ssd_norm_w[l], w_hg[l], w_gla[l], w_ssd[l], w_o[l], m[2], geo, min(256, lc))
        h2, experts, _, rank, wrow, counts = _router(xs, norm_ffn_w[l], m[3], m[4], router_w[l],
                                                     router_bias[l], geo, tm)
        slot, buf_t, block_e = _dispatch(experts, rank, counts, tm)
        xg = jnp.take(h2, buf_t, axis=0)
        yg = _experts(xg, block_e, e_gate[l], e_up[l], e_down[l])
        yk = jnp.take(yg, slot.reshape(-1), axis=0).reshape(TOP_K, -1, d)
        xs = _ffn_out(xs, h2, yk, wrow, s_gate[l], s_up[l], s_down[l], m[5], geo, min(256, lc))

    out = _final_norm(xs, final_norm_w, nctx, tm)
    return out.reshape(nb, seq, d)
```

```python
import functools
import math

import numpy as np
import jax
import jax.numpy as jnp
from jax import lax
from jax.experimental import pallas as pl
from jax.experimental.pallas import tpu as pltpu

f32 = jnp.float32
bf16 = jnp.bfloat16
HIGHEST = lax.Precision.HIGHEST

EPS = 1e-6
POS_BASE = 10000.0
GRID_W = 64
N_MOD = 6
HG_HEADS = 4
HG_DK = 128
HG_WIDTH = HG_HEADS * HG_DK
GLA_HEADS = 4
GLA_DK = 64
GLA_DV = 128
GLA_RANK = 16
GLA_TAU = 16.0
SSD_HEADS = 8
SSD_HEADDIM = 64
SSD_INNER = SSD_HEADS * SSD_HEADDIM
SSD_GROUPS = 2
SSD_STATE = 128
SSD_CONV = 5
N_BRANCH = 3
N_EXPERTS = 64
TOP_K = 8
N_GROUPS = 8
TOPK_GROUPS = 4
ROUTE_SCALE = 2.5
MOE_BLOCK = 256
NEG_MASK = -1e4

LANES = 128
SUBLANES = 8
VMEM_LIMIT = 56 * 1024 * 1024
EXP_CLAMP = 80.0

P_GATE = 0
P_HG = 3072
P_GLA = 5632
P_SSD = 7168
P_TOTAL = 8704
S_LRF, S_LRB, S_DTF, S_DTB = 0, 16, 32, 40


def _cparams(sem):
    return pltpu.CompilerParams(dimension_semantics=sem, vmem_limit_bytes=VMEM_LIMIT)


def _split_hi_lo(x):
    hi = x.astype(bf16)
    lo = (x - hi.astype(f32)).astype(bf16)
    return hi, lo


def _dot(a, b):
    return jnp.dot(a, b, preferred_element_type=f32)


def _dot_nt(a, b):
    return lax.dot_general(a, b, (((1,), (1,)), ((), ())), preferred_element_type=f32)


def _dot_tn(a, b):
    return lax.dot_general(a, b, (((0,), (0,)), ((), ())), preferred_element_type=f32)


def _dot2(m, x):
    hi, lo = _split_hi_lo(x)
    return _dot(m, hi) + _dot(m, lo)


def _dot2_l(x, m):
    hi, lo = _split_hi_lo(x)
    return _dot(hi, m) + _dot(lo, m)


def _silu(x):
    return x * jax.nn.sigmoid(x)


def _log_sigmoid(z):
    return jnp.minimum(z, 0.0) - jnp.log(1.0 + jnp.exp(-jnp.abs(z)))


def _softplus(z):
    return jnp.maximum(z, 0.0) + jnp.log(1.0 + jnp.exp(-jnp.abs(z)))


def _scan_mask(c, rev):
    r = lax.broadcasted_iota(jnp.int32, (c, c), 0)
    s = lax.broadcasted_iota(jnp.int32, (c, c), 1)
    return (r <= s) if rev else (r >= s)


def _mods_kernel(c_ref, w_ref, b_ref, o_ref):
    o_ref[...] = jnp.dot(_silu(c_ref[...]), w_ref[...], precision=HIGHEST,
                         preferred_element_type=f32) + b_ref[...]


def _mods(c_all, ada_w, ada_b):
    depth, d, n = ada_w.shape
    tn = 1024
    return pl.pallas_call(
        _mods_kernel,
        grid=(depth, n // tn),
        in_specs=[pl.BlockSpec((SUBLANES, d), lambda l, j: (0, 0)),
                  pl.BlockSpec((None, d, tn), lambda l, j: (l, 0, j)),
                  pl.BlockSpec((None, 1, tn), lambda l, j: (l, 0, j))],
        out_specs=pl.BlockSpec((None, SUBLANES, tn), lambda l, j: (l, 0, j)),
        out_shape=jax.ShapeDtypeStruct((depth, SUBLANES, n), f32),
        compiler_params=_cparams(("parallel", "parallel")),
        name="mods",
    )(c_all, ada_w, ada_b.reshape(depth, 1, n))


def _embed_kernel(ctx_ref, x_ref, pe_ref, o_ref, *, nct):
    i = pl.program_id(0)

    @pl.when(i < nct)
    def _():
        o_ref[...] = ctx_ref[...]

    @pl.when(i >= nct)
    def _():
        o_ref[...] = x_ref[...] + pe_ref[...]


def _embed(ctx2, x2, pe, te):
    nctx, d = ctx2.shape
    nlat = x2.shape[0]
    nct, npe = nctx // te, pe.shape[0] // te
    return pl.pallas_call(
        functools.partial(_embed_kernel, nct=nct),
        grid=((nctx + nlat) // te,),
        in_specs=[pl.BlockSpec((te, d), lambda i: (jnp.minimum(i, nct - 1), 0)),
                  pl.BlockSpec((te, d), lambda i: (jnp.maximum(i - nct, 0), 0)),
                  pl.BlockSpec((te, d), lambda i: (jnp.maximum(i - nct, 0) % npe, 0))],
        out_specs=pl.BlockSpec((te, d), lambda i: (i, 0)),
        out_shape=jax.ShapeDtypeStruct((nctx + nlat, d), f32),
        compiler_params=_cparams(("parallel",)),
        name="embed",
    )(ctx2, x2, pe)


def _norm_rows(x, w):
    return x * lax.rsqrt(jnp.mean(x * x, axis=-1, keepdims=True) + EPS) * w


def _beats(a, b, a_first):
    return (a >= b) if a_first else (a > b)


def _router_kernel(x_ref, w_ref, sh_ref, sc_ref, rwt_ref, rb_ref,
                   h_ref, e_ref, wt_ref, rank_ref, wrow_ref, cnt_ref):
    @pl.when(pl.program_id(0) == 0)
    def _():
        cnt_ref[...] = jnp.zeros_like(cnt_ref)

    y = _norm_rows(x_ref[...], w_ref[...])
    h = y * (1.0 + sc_ref[...]) + sh_ref[...]
    h_ref[...] = h.astype(bf16)
    tm = h.shape[0]
    logits = lax.dot_general(rwt_ref[...], h, (((1,), (1,)), ((), ())), precision=HIGHEST,
                             preferred_element_type=f32)
    scores = jax.nn.sigmoid(logits)
    biased = scores + rb_ref[...]

    gsz = N_EXPERTS // N_GROUPS
    miota = lax.broadcasted_iota(jnp.int32, (gsz, tm), 0)
    blocks, gscore = [], []
    for g in range(N_GROUPS):
        blk = biased[g * gsz:(g + 1) * gsz, :]
        m1 = jnp.max(blk, axis=0, keepdims=True)
        i1 = jnp.min(jnp.where(blk == m1, miota, gsz), axis=0, keepdims=True)
        m2 = jnp.max(jnp.where(miota == i1, -jnp.inf, blk), axis=0, keepdims=True)
        blocks.append(blk)
        gscore.append(m1 + m2)
    vals = []
    for g in range(N_GROUPS):
        ahead = jnp.zeros((1, tm), jnp.int32)
        for o in range(N_GROUPS):
            if o != g:
                ahead = ahead + _beats(gscore[o], gscore[g], o < g).astype(jnp.int32)
        vals.append(jnp.where(ahead < TOPK_GROUPS, blocks[g], NEG_MASK))
    vals = jnp.concatenate(vals, axis=0)

    eiota = lax.broadcasted_iota(jnp.int32, (N_EXPERTS, tm), 0)
    member = jnp.zeros((N_EXPERTS, tm), f32)
    chosen, weights = [], []
    for k in range(TOP_K):
        m = jnp.max(vals, axis=0, keepdims=True)
        ei = jnp.min(jnp.where(vals == m, eiota, N_EXPERTS), axis=0, keepdims=True)
        sel = eiota == ei
        chosen.append(ei)
        weights.append(jnp.sum(jnp.where(sel, scores, 0.0), axis=0, keepdims=True))
        member = member + sel.astype(f32)
        vals = jnp.where(sel, -jnp.inf, vals)
    wsum = weights[0]
    for k in range(1, TOP_K):
        wsum = wsum + weights[k]

    r = lax.broadcasted_iota(jnp.int32, (tm, tm), 0)
    s = lax.broadcasted_iota(jnp.int32, (tm, tm), 1)
    member_bf = member.astype(bf16)
    rank_all = _dot(member_bf, (r < s).astype(bf16)) + cnt_ref[:, 0:1]
    for k in range(TOP_K):
        e_ref[k:k + 1, :] = chosen[k]
        wt_ref[k:k + 1, :] = weights[k] / wsum * ROUTE_SCALE
        rank_ref[k:k + 1, :] = jnp.sum(jnp.where(eiota == chosen[k], rank_all, 0.0), axis=0,
                                       keepdims=True).astype(jnp.int32)
    cnt_ref[...] = cnt_ref[...] + _dot(member_bf, jnp.ones((tm, LANES), bf16))
    eye = (lax.broadcasted_iota(jnp.int32, (TOP_K, LANES), 0)
           == lax.broadcasted_iota(jnp.int32, (TOP_K, LANES), 1)).astype(bf16)
    w_hi, w_lo = _split_hi_lo(wt_ref[...])
    wrow_ref[...] = _dot_tn(w_hi, eye) + _dot_tn(w_lo, eye)


def _mod_block(d, tm, geo):
    nctx, seq, nb = geo
    return pl.BlockSpec((None, 1, d), lambda i: (jnp.where(i * tm < nctx, nb, (i * tm - nctx) // seq), 0, 0))


def _router(xs, w, shift, scale, router_w, router_bias, geo, tm):
    t, d = xs.shape
    ne = router_w.shape[1]
    kt = pl.BlockSpec((TOP_K, tm), lambda i: (0, i))
    return pl.pallas_call(
        _router_kernel, grid=(t // tm,),
        in_specs=[pl.BlockSpec((tm, d), lambda i: (i, 0)),
                  pl.BlockSpec((1, d), lambda i: (0, 0)),
                  _mod_block(d, tm, geo), _mod_block(d, tm, geo),
                  pl.BlockSpec((ne, d), lambda i: (0, 0)),
                  pl.BlockSpec((ne, 1), lambda i: (0, 0))],
        out_specs=[pl.BlockSpec((tm, d), lambda i: (i, 0)), kt, kt, kt,
                   pl.BlockSpec((tm, LANES), lambda i: (i, 0)),
                   pl.BlockSpec((ne, LANES), lambda i: (0, 0))],
        out_shape=[jax.ShapeDtypeStruct((t, d), bf16),
                   jax.ShapeDtypeStruct((TOP_K, t), jnp.int32),
                   jax.ShapeDtypeStruct((TOP_K, t), f32),
                   jax.ShapeDtypeStruct((TOP_K, t), jnp.int32),
                   jax.ShapeDtypeStruct((t, LANES), f32),
                   jax.ShapeDtypeStruct((ne, LANES), f32)],
        compiler_params=_cparams(("arbitrary",)), name="router",
    )(xs, w.reshape(1, d), shift, scale, router_w.T, router_bias.astype(f32).reshape(ne, 1))


PROJ_TN = 512


def _in_proj_kernel(x_ref, nw_ref, sh_ref, sc_ref, w_ref, ws_ref, wst_ref, p_ref, ps_ref, pst_ref):
    y = _norm_rows(x_ref[...], nw_ref[...])
    h = (y * (1.0 + sc_ref[...]) + sh_ref[...]).astype(bf16)
    for j in range(p_ref.shape[1] // PROJ_TN):
        sl = slice(j * PROJ_TN, (j + 1) * PROJ_TN)
        p_ref[:, sl] = _dot(h, w_ref[:, sl]).astype(p_ref.dtype)
    ps_ref[...] = _dot(h, ws_ref[...])
    pst_ref[...] = _dot_nt(wst_ref[...], h)


def _in_proj(xs, nw, shift, scale, w_main, w_small, geo, tm):
    t, d = xs.shape
    n = w_main.shape[1]
    ns = w_small.shape[1]

    def resident(shape):
        return pl.BlockSpec(shape, lambda i: (0, 0), pipeline_mode=pl.Buffered(1))

    return pl.pallas_call(
        _in_proj_kernel, grid=(t // tm,),
        in_specs=[pl.BlockSpec((tm, d), lambda i: (i, 0)),
                  pl.BlockSpec((1, d), lambda i: (0, 0)),
                  _mod_block(d, tm, geo), _mod_block(d, tm, geo),
                  resident((d, n)), resident((d, ns)), resident((ns, d))],
        out_specs=[pl.BlockSpec((tm, n), lambda i: (i, 0)),
                   pl.BlockSpec((tm, ns), lambda i: (i, 0)),
                   pl.BlockSpec((ns, tm), lambda i: (0, i))],
        out_shape=[jax.ShapeDtypeStruct((t, n), bf16),
                   jax.ShapeDtypeStruct((t, ns), f32),
                   jax.ShapeDtypeStruct((ns, t), f32)],
        compiler_params=_cparams(("parallel",)), name="in_proj",
    )(xs, nw.reshape(1, d), shift, scale, w_main, w_small, w_small.T)


def _chunk_blocks(geo, c, rev):
    nctx, seq, nb = geo
    lc = nctx // nb
    ncc, nlc = lc // c, seq // c

    def blk(b, i):
        j_ctx = (ncc - 1 - i) if rev else i
        j_lat = (nlc - 1 - (i - ncc)) if rev else (i - ncc)
        return jnp.where(i < ncc, b * ncc + j_ctx, nctx // c + b * nlc + j_lat)

    return blk, ncc + nlc


def _sub_chunks(n_rows, c, rev):
    n = n_rows // c
    order = range(n - 1, -1, -1) if rev else range(n)
    return [slice(j * c, (j + 1) * c) for j in order]


def _decay_terms(q, k, g, mask, rev):
    c = q.shape[0]
    b = _dot2(mask.astype(bf16), g)
    last = 0 if rev else c - 1
    b_tot = b[last:last + 1, :]
    b_mid = b[c // 2:c // 2 + 1, :]
    qd = q * jnp.exp(jnp.minimum(b - b_mid, EXP_CLAMP))
    kd = k * jnp.exp(jnp.minimum(b_mid - b, EXP_CLAMP))
    qe = q * jnp.exp(b)
    ke = k * jnp.exp(b_tot - b)
    return qd.astype(bf16), kd.astype(bf16), qe.astype(bf16), ke.astype(bf16), jnp.exp(b_tot)


def _hgrn2_gates(z, lb):
    e = jnp.exp(-jnp.abs(z))
    inv = 1.0 / (1.0 + e)
    sig_pos = jnp.where(z >= 0, 1.0, e) * inv
    sig_neg = jnp.where(z >= 0, e, 1.0) * inv
    ls = jnp.minimum(z, 0.0) - jnp.log(1.0 + e)
    log_f = jnp.where(lb > 0.0, jnp.log(jnp.where(lb > 0.0, lb + (1.0 - lb) * sig_pos, 1.0)), ls)
    return (1.0 - lb) * sig_neg, log_f


def _hgrn2_kernel(qf_ref, zf_ref, vf_ref, qb_ref, zb_ref, vb_ref, lbf_ref, lbb_ref,
                  of_ref, ob_ref, sf_ref, sb_ref, *, c):
    @pl.when(pl.program_id(1) == 0)
    def _():
        sf_ref[...] = jnp.zeros_like(sf_ref)
        sb_ref[...] = jnp.zeros_like(sb_ref)

    for rev, q_ref, z_ref, v_ref, lb_ref, o_ref, s_ref in (
            (False, qf_ref, zf_ref, vf_ref, lbf_ref, of_ref, sf_ref),
            (True, qb_ref, zb_ref, vb_ref, lbb_ref, ob_ref, sb_ref)):
        mask = _scan_mask(c, rev)
        for rows in _sub_chunks(q_ref.shape[0], c, rev):
            q = q_ref[rows, :].astype(f32)
            k, g = _hgrn2_gates(z_ref[rows, :].astype(f32), lb_ref[...])
            v = v_ref[rows, :]
            qd, kd, qe, ke, dtot = _decay_terms(q, k, g, mask, rev)
            for h in range(HG_HEADS):
                sl = slice(h * HG_DK, (h + 1) * HG_DK)
                st = s_ref[h]
                att = jnp.where(mask, _dot_nt(qd[:, sl], kd[:, sl]), 0.0)
                o = _dot(att.astype(bf16), v[:, sl]) + _dot_nt(qe[:, sl], st.astype(bf16))
                o_ref[rows, sl] = o.astype(o_ref.dtype)
                s_ref[h] = st * dtot[:, sl] + _dot_tn(v[:, sl], ke[:, sl])


def _hgrn2_scan(p, lb_f, lb_b, geo, c, r):
    t = p.shape[0]
    nb = geo[2]
    blk_f, steps = _chunk_blocks(geo, r, False)
    blk_b, _ = _chunk_blocks(geo, r, True)
    w = HG_WIDTH
    c0 = P_HG // w

    def spec(blk, col):
        return pl.BlockSpec((r, w), lambda b, i: (blk(b, i), col))

    vec = pl.BlockSpec((1, w), lambda b, i: (0, 0))
    return pl.pallas_call(
        functools.partial(_hgrn2_kernel, c=c), grid=(nb, steps),
        in_specs=[spec(blk_f, c0), spec(blk_f, c0 + 1), spec(blk_f, c0 + 3),
                  spec(blk_b, c0), spec(blk_b, c0 + 2), spec(blk_b, c0 + 3), vec, vec],
        out_specs=[spec(blk_f, 0), spec(blk_b, 0)],
        out_shape=[jax.ShapeDtypeStruct((t, w), bf16)] * 2,
        scratch_shapes=[pltpu.VMEM((HG_HEADS, HG_DK, HG_DK), f32)] * 2,
        compiler_params=_cparams(("parallel", "arbitrary")), name="hgrn2_scan",
    )(p, p, p, p, p, p, lb_f.reshape(1, w), lb_b.reshape(1, w))


def _gla_kernel(qf_ref, kf_ref, vf_ref, sf_in_ref, qb_ref, kb_ref, vb_ref, sb_in_ref,
                gwf_ref, gwb_ref, gbf_ref, gbb_ref, of_ref, ob_ref, sf_ref, sb_ref, *, c):
    @pl.when(pl.program_id(1) == 0)
    def _():
        sf_ref[...] = jnp.zeros_like(sf_ref)
        sb_ref[...] = jnp.zeros_like(sb_ref)

    npair = GLA_HEADS // 2
    lane = lax.broadcasted_iota(jnp.int32, (1, LANES), 1)
    vrow = lax.broadcasted_iota(jnp.int32, (2 * GLA_DV, LANES), 0)
    vcol = lax.broadcasted_iota(jnp.int32, (2 * GLA_DV, LANES), 1)
    block_diag = (vrow // GLA_DV) == (vcol // GLA_DK)
    for rev, q_ref, k_ref, v_ref, sm_ref, gw_ref, gb_ref, o_ref, s_ref in (
            (False, qf_ref, kf_ref, vf_ref, sf_in_ref, gwf_ref, gbf_ref, of_ref, sf_ref),
            (True, qb_ref, kb_ref, vb_ref, sb_in_ref, gwb_ref, gbb_ref, ob_ref, sb_ref)):
        mask = _scan_mask(c, rev)
        for rows in _sub_chunks(q_ref.shape[0], c, rev):
            q = q_ref[rows, :].astype(f32) * (GLA_DK ** -0.5)
            k = k_ref[rows, :].astype(f32)
            v = v_ref[rows, :]
            pre = jnp.dot(sm_ref[rows, :], gw_ref[...], precision=HIGHEST,
                          preferred_element_type=f32) + gb_ref[...]
            g = _log_sigmoid(pre) / GLA_TAU
            qd, kd, qe, ke, dtot = _decay_terms(q, k, g, mask, rev)
            for p in range(npair):
                sl = slice(p * LANES, (p + 1) * LANES)
                vsl = slice(p * 2 * GLA_DV, (p + 1) * 2 * GLA_DV)
                st = s_ref[p]
                o_inter = _dot_nt(qe[:, sl], st.astype(bf16))
                for hh in range(2):
                    head = (lane // GLA_DK) == hh
                    qh = jnp.where(head, qd[:, sl], jnp.zeros_like(qd[:, sl]))
                    att = jnp.where(mask, _dot_nt(qh, kd[:, sl]), 0.0)
                    osl = slice((2 * p + hh) * GLA_DV, (2 * p + hh + 1) * GLA_DV)
                    o = _dot(att.astype(bf16), v[:, osl]) + o_inter[:, hh * GLA_DV:(hh + 1) * GLA_DV]
                    o_ref[rows, osl] = o.astype(o_ref.dtype)
                upd = st * dtot[:, sl] + _dot_tn(v[:, vsl], ke[:, sl])
                s_ref[p] = jnp.where(block_diag, upd, 0.0)


def _gla_scan(p, psmall, gate_w, gate_b, geo, c, r):
    t = p.shape[0]
    nb = geo[2]
    blk_f, steps = _chunk_blocks(geo, r, False)
    blk_b, _ = _chunk_blocks(geo, r, True)
    wk, wv = GLA_HEADS * GLA_DK, GLA_HEADS * GLA_DV

    def spec(blk, width, col):
        return pl.BlockSpec((r, width), lambda b, i: (blk(b, i), col))

    def const(shape):
        return pl.BlockSpec(shape, lambda b, i: (0, 0))

    gwf = jnp.zeros((LANES, wk), f32).at[S_LRF:S_LRF + GLA_RANK].set(gate_w[0])
    gwb = jnp.zeros((LANES, wk), f32).at[S_LRB:S_LRB + GLA_RANK].set(gate_w[1])
    qcol, kcol, vcol = P_GLA // wk, P_GLA // wk + 1, (P_GLA + 2 * wk) // wv
    return pl.pallas_call(
        functools.partial(_gla_kernel, c=c), grid=(nb, steps),
        in_specs=[spec(blk_f, wk, qcol), spec(blk_f, wk, kcol), spec(blk_f, wv, vcol), spec(blk_f, LANES, 0),
                  spec(blk_b, wk, qcol), spec(blk_b, wk, kcol), spec(blk_b, wv, vcol), spec(blk_b, LANES, 0),
                  const((LANES, wk)), const((LANES, wk)), const((1, wk)), const((1, wk))],
        out_specs=[spec(blk_f, wv, 0), spec(blk_b, wv, 0)],
        out_shape=[jax.ShapeDtypeStruct((t, wv), bf16)] * 2,
        scratch_shapes=[pltpu.VMEM((GLA_HEADS // 2, 2 * GLA_DV, 2 * GLA_DK), f32)] * 2,
        compiler_params=_cparams(("parallel", "arbitrary")), name="gla_scan",
    )(p, p, p, psmall, p, p, p, psmall, gwf, gwb, gate_b[0].reshape(1, wk), gate_b[1].reshape(1, wk))


HALO = SUBLANES


def _conv_kernel(prev_ref, cur_ref, next_ref, w_ref, b_ref, o_ref, ext_ref, *, geo, tc):
    nctx, seq, nb = geo
    r0 = pl.program_id(0) * tc
    in_ctx = r0 < nctx
    seg_len = jnp.where(in_ctx, nctx // nb, seq)
    pos = jnp.where(in_ctx, r0 % (nctx // nb), (r0 - nctx) % seq)
    first = pos == 0
    last = pos + tc == seg_len
    ext_ref[0:HALO, :] = jnp.where(first, 0.0, prev_ref[...].astype(f32))
    ext_ref[HALO:HALO + tc, :] = cur_ref[...].astype(f32)
    ext_ref[HALO + tc:HALO + tc + HALO, :] = jnp.where(last, 0.0, next_ref[...].astype(f32))
    acc = jnp.zeros(o_ref.shape, f32) + b_ref[...]
    for j in range(SSD_CONV):
        off = HALO - SSD_CONV // 2 + j
        acc = acc + ext_ref[off:off + tc, :] * w_ref[j:j + 1, :]
    o_ref[...] = _silu(acc).astype(o_ref.dtype)


def _ssd_conv(p, conv_w, conv_b, geo, tc):
    t = p.shape[0]
    nch = conv_w.shape[1]
    tw = 512
    col0 = (P_SSD + SSD_INNER) // tw
    hb = tc // HALO
    nhb = t // HALO
    return pl.pallas_call(
        functools.partial(_conv_kernel, geo=geo, tc=tc),
        grid=(t // tc, nch // tw),
        in_specs=[pl.BlockSpec((HALO, tw), lambda i, j: (jnp.maximum(i * hb - 1, 0), col0 + j)),
                  pl.BlockSpec((tc, tw), lambda i, j: (i, col0 + j)),
                  pl.BlockSpec((HALO, tw), lambda i, j: (jnp.minimum((i + 1) * hb, nhb - 1), col0 + j)),
                  pl.BlockSpec((SSD_CONV, tw), lambda i, j: (0, j)),
                  pl.BlockSpec((1, tw), lambda i, j: (0, j))],
        out_specs=pl.BlockSpec((tc, tw), lambda i, j: (i, j)),
        out_shape=jax.ShapeDtypeStruct((t, nch), bf16),
        scratch_shapes=[pltpu.VMEM((tc + 2 * HALO, tw), f32)],
        compiler_params=_cparams(("parallel", "parallel")), name="ssd_conv",
    )(p, p, p, conv_w, conv_b.reshape(1, nch))


def _ssd_kernel(xf_ref, bf_ref, cf_ref, smf_ref, smtf_ref, xb_ref, bb_ref, cb_ref, smb_ref, smtb_ref,
                arow_ref, brow_ref, acol_ref, bcol_ref, ef_ref, eb_ref,
                yf_ref, yb_ref, sf_ref, sb_ref, *, c):
    @pl.when(pl.program_id(1) == 0)
    def _():
        sf_ref[...] = jnp.zeros_like(sf_ref)
        sb_ref[...] = jnp.zeros_like(sb_ref)

    gw = SSD_INNER // SSD_GROUPS
    lane = lax.broadcasted_iota(jnp.int32, (1, LANES), 1)
    for d, rev, x_ref, bm_ref, cm_ref, sm_ref, smt_ref, e_ref, y_ref, s_ref, lane0 in (
            (0, False, xf_ref, bf_ref, cf_ref, smf_ref, smtf_ref, ef_ref, yf_ref, sf_ref, S_DTF),
            (1, True, xb_ref, bb_ref, cb_ref, smb_ref, smtb_ref, eb_ref, yb_ref, sb_ref, S_DTB)):
        mask = _scan_mask(c, rev)
        mask_bf = mask.astype(bf16)
        mask_t_bf = _scan_mask(c, not rev).astype(bf16)
        expand = e_ref[...]
        for rows in _sub_chunks(x_ref.shape[0], c, rev):
            dt = _softplus(sm_ref[rows, :] + brow_ref[d:d + 1, :])
            cum = _dot2(mask_bf, dt * arow_ref[d:d + 1, :])
            last = 0 if rev else c - 1
            cum_end = cum[last:last + 1, :]
            dt_t = _softplus(smt_ref[:, rows] + bcol_ref[d])
            cum_t = _dot2_l(dt_t * acol_ref[d], mask_t_bf)

            x = x_ref[rows, :].astype(f32)
            xdt = x * _dot2_l(dt, expand)
            w = (xdt * _dot2_l(jnp.exp(cum_end - cum), expand)).astype(bf16)
            xdt = xdt.astype(bf16)
            dec_t = _dot2_l(jnp.exp(cum), expand)
            dec_end = _dot2_l(jnp.broadcast_to(jnp.exp(cum_end), (SUBLANES, LANES)), expand)[0:1, :]
            bm = bm_ref[rows, :]
            cm = cm_ref[rows, :]
            for g in range(SSD_GROUPS):
                gsl = slice(g * SSD_STATE, (g + 1) * SSD_STATE)
                hsl = slice(g * gw, (g + 1) * gw)
                st = s_ref[:, hsl]
                scores = _dot_nt(cm[:, gsl], bm[:, gsl])
                y_inter = _dot(cm[:, gsl], st.astype(bf16)) * dec_t[:, hsl]
                for tile in range(gw // LANES):
                    tsl = slice(g * gw + tile * LANES, g * gw + (tile + 1) * LANES)
                    xt = xdt[:, tsl]
                    acc = y_inter[:, tile * LANES:(tile + 1) * LANES]
                    for hh in range(LANES // SSD_HEADDIM):
                        h = (g * gw + tile * LANES) // SSD_HEADDIM + hh
                        diff = cum[:, lane0 + h:lane0 + h + 1] - cum_t[h:h + 1, :]
                        seg = jnp.where(mask, jnp.exp(jnp.minimum(diff, 0.0)), 0.0)
                        xh = jnp.where((lane // SSD_HEADDIM) == hh, xt, jnp.zeros_like(xt))
                        acc = acc + _dot((scores * seg).astype(bf16), xh)
                    y_ref[rows, tsl] = acc.astype(y_ref.dtype)
                s_ref[:, hsl] = st * dec_end[:, hsl] + _dot_tn(bm[:, gsl], w[:, hsl])


def _ssd_scan(xbc, psmall, psmall_t, a_log, dt_bias, geo, c, r):
    t = xbc.shape[0]
    nb = geo[2]
    blk_f, steps = _chunk_blocks(geo, r, False)
    blk_b, _ = _chunk_blocks(geo, r, True)
    wi, wn = SSD_INNER, SSD_GROUPS * SSD_STATE

    def spec(blk, width, col):
        return pl.BlockSpec((r, width), lambda b, i: (blk(b, i), col))

    def spec_t(blk, row):
        return pl.BlockSpec((SUBLANES, r), lambda b, i: (row, blk(b, i)))

    def const(shape):
        return pl.BlockSpec(shape, lambda b, i: (0,) * len(shape))

    a = -jnp.exp(a_log.astype(f32))
    arow = jnp.zeros((2, LANES), f32).at[0, S_DTF:S_DTF + SSD_HEADS].set(a[0]).at[1, S_DTB:S_DTB + SSD_HEADS].set(a[1])
    brow = jnp.zeros((2, LANES), f32).at[0, S_DTF:S_DTF + SSD_HEADS].set(dt_bias[0]).at[1, S_DTB:S_DTB + SSD_HEADS].set(dt_bias[1])
    head_of_lane = np.arange(wi) // SSD_HEADDIM
    ef = jnp.asarray((np.arange(LANES)[:, None] == S_DTF + head_of_lane[None, :]), bf16)
    eb = jnp.asarray((np.arange(LANES)[:, None] == S_DTB + head_of_lane[None, :]), bf16)
    xcol, bcol, ccol = 0, wi // wn, wi // wn + 1
    return pl.pallas_call(
        functools.partial(_ssd_kernel, c=c), grid=(nb, steps),
        in_specs=[spec(blk_f, wi, xcol), spec(blk_f, wn, bcol), spec(blk_f, wn, ccol), spec(blk_f, LANES, 0),
                  spec_t(blk_f, S_DTF // SUBLANES),
                  spec(blk_b, wi, xcol), spec(blk_b, wn, bcol), spec(blk_b, wn, ccol), spec(blk_b, LANES, 0),
                  spec_t(blk_b, S_DTB // SUBLANES),
                  const((2, LANES)), const((2, LANES)), const((2, SSD_HEADS, 1)), const((2, SSD_HEADS, 1)),
                  const((LANES, wi)), const((LANES, wi))],
        out_specs=[spec(blk_f, wi, 0), spec(blk_b, wi, 0)],
        out_shape=[jax.ShapeDtypeStruct((t, wi), bf16)] * 2,
        scratch_shapes=[pltpu.VMEM((SSD_STATE, wi), f32)] * 2,
        compiler_params=_cparams(("parallel", "arbitrary")), name="ssd_scan",
    )(xbc, xbc, xbc, psmall, psmall_t, xbc, xbc, xbc, psmall, psmall_t,
      arow, brow, a.reshape(2, SSD_HEADS, 1), dt_bias.astype(f32).reshape(2, SSD_HEADS, 1), ef, eb)


def _group_norm(o, w, width):
    parts = []
    for j in range(o.shape[1] // width):
        blk = o[:, j * width:(j + 1) * width]
        parts.append(blk * lax.rsqrt(jnp.mean(blk * blk, axis=-1, keepdims=True) + EPS))
    return jnp.concatenate(parts, axis=1) * w


def _merge_kernel(x_ref, hgf_ref, hgb_ref, hgg_ref, glf_ref, glb_ref, glg_ref,
                  sdf_ref, sdb_ref, sdx_ref, sdz_ref, gate_ref,
                  hgw_ref, glw_ref, sdd_ref, sdw_ref, wh_ref, wg_ref, ws_ref, wo_ref, m2_ref, o_ref):
    o_hg = _group_norm(hgf_ref[...].astype(f32) + hgb_ref[...].astype(f32), hgw_ref[...], HG_DK)
    o_hg = o_hg * _silu(hgg_ref[...].astype(f32))
    o_gl = _group_norm(glf_ref[...].astype(f32) + glb_ref[...].astype(f32), glw_ref[...], GLA_DV)
    o_gl = o_gl * _silu(glg_ref[...].astype(f32))
    y = sdf_ref[...].astype(f32) + sdb_ref[...].astype(f32) + sdd_ref[...] * sdx_ref[...].astype(f32)
    o_sd = _group_norm(y * _silu(sdz_ref[...].astype(f32)), sdw_ref[...], SSD_INNER // SSD_GROUPS)
    d = x_ref.shape[1]
    gate = jax.nn.sigmoid(gate_ref[...].astype(f32))
    y = (gate[:, 0:d] * _dot(o_hg.astype(bf16), wh_ref[...])
         + gate[:, d:2 * d] * _dot(o_gl.astype(bf16), wg_ref[...])
         + gate[:, 2 * d:3 * d] * _dot(o_sd.astype(bf16), ws_ref[...]))
    o_ref[...] = x_ref[...] + m2_ref[...] * _dot(y.astype(bf16), wo_ref[...])


def _merge(xs, p, hg_f, hg_b, gl_f, gl_b, sd_f, sd_b, xbc, hg_norm_w, gla_norm_w, ssd_d, ssd_norm_w,
           w_hg, w_gla, w_ssd, w_out, mod2, geo, tm):
    t, d = xs.shape
    w = 512

    def rows(width, col):
        return pl.BlockSpec((tm, width), lambda i: (i, col))

    def const(shape):
        return pl.BlockSpec(shape, lambda i: (0, 0))

    tile4 = lambda v: jnp.tile(v, w // v.shape[0]).reshape(1, w)
    return pl.pallas_call(
        _merge_kernel, grid=(t // tm,),
        in_specs=[rows(d, 0),
                  rows(w, 0), rows(w, 0), rows(w, (P_HG + 4 * w) // w),
                  rows(w, 0), rows(w, 0), rows(w, (P_GLA + 2 * w) // w),
                  rows(w, 0), rows(w, 0), rows(w, 0), rows(w, P_SSD // w),
                  rows(3 * d, P_GATE // (3 * d)),
                  const((1, w)), const((1, w)), const((1, w)), const((1, w)),
                  const((w, d)), const((w, d)), const((w, d)), const((d, d)),
                  _mod_block(d, tm, geo)],
        out_specs=rows(d, 0),
        out_shape=jax.ShapeDtypeStruct((t, d), f32),
        compiler_params=_cparams(("parallel",)), name="merge",
    )(xs, hg_f, hg_b, p, gl_f, gl_b, p, sd_f, sd_b, xbc, p, p,
      tile4(hg_norm_w), tile4(gla_norm_w), jnp.repeat(ssd_d, SSD_HEADDIM).reshape(1, w), ssd_norm_w.reshape(1, w),
      w_hg, w_gla, w_ssd, w_out, mod2)


MOE_SUB = 4


def _expert_kernel(be_ref, x_ref, *refs):
    o_ref = refs[-1]
    for s in range(MOE_SUB):
        wg_ref, wu_ref, wd_ref = refs[3 * s:3 * s + 3]
        rows = slice(s * MOE_BLOCK, (s + 1) * MOE_BLOCK)
        x = x_ref[rows, :]
        a = _silu(_dot(x, wg_ref[...])) * _dot(x, wu_ref[...])
        o_ref[rows, :] = _dot(a.astype(bf16), wd_ref[...]).astype(o_ref.dtype)


def _experts(xg, block_e, w_gate, w_up, w_down):
    npad, d = xg.shape
    ff = w_gate.shape[2]
    rows = MOE_SUB * MOE_BLOCK
    w_specs = []
    for s in range(MOE_SUB):
        pick = lambda i, be, s=s: (be[i * MOE_SUB + s], 0, 0)
        w_specs += [pl.BlockSpec((None, d, ff), pick), pl.BlockSpec((None, d, ff), pick),
                    pl.BlockSpec((None, ff, d), pick)]
    grid_spec = pltpu.PrefetchScalarGridSpec(
        num_scalar_prefetch=1, grid=(npad // rows,),
        in_specs=[pl.BlockSpec((rows, d), lambda i, be: (i, 0))] + w_specs,
        out_specs=pl.BlockSpec((rows, d), lambda i, be: (i, 0)))
    return pl.pallas_call(
        _expert_kernel, grid_spec=grid_spec,
        out_shape=jax.ShapeDtypeStruct((npad, d), bf16),
        compiler_params=_cparams(("arbitrary",)), name="experts",
    )(block_e, xg, *([w_gate, w_up, w_down] * MOE_SUB))


def _slots_kernel(ps_ref, e_ref, r_ref, o_ref):
    e = e_ref[...]
    acc = r_ref[...]
    for j in range(N_EXPERTS):
        acc = acc + jnp.where(e == j, ps_ref[j], 0)
    o_ref[...] = acc


def _slots(experts, rank, pad_start, tl):
    k, t = experts.shape
    spec = pl.BlockSpec((k, tl), lambda i, ps: (0, i))
    return pl.pallas_call(
        _slots_kernel,
        grid_spec=pltpu.PrefetchScalarGridSpec(num_scalar_prefetch=1, grid=(t // tl,),
                                               in_specs=[spec, spec], out_specs=spec),
        out_shape=jax.ShapeDtypeStruct((k, t), jnp.int32),
        compiler_params=_cparams(("parallel",)), name="slots",
    )(pad_start, experts, rank)


def _ffn_out_kernel(x_ref, h_ref, y_ref, wr_ref, sg_ref, su_ref, sd_ref, m5_ref, o_ref):
    h = h_ref[...]
    a = _silu(_dot(h, sg_ref[...])) * _dot(h, su_ref[...])
    acc = _dot(a.astype(bf16), sd_ref[...])
    wr = wr_ref[...]
    for k in range(TOP_K):
        acc = acc + y_ref[k].astype(f32) * wr[:, k:k + 1]
    o_ref[...] = x_ref[...] + m5_ref[...] * acc


def _ffn_out(xs, h, yk, wrow, s_gate, s_up, s_down, mod5, geo, tm):
    t, d = xs.shape
    ff = s_gate.shape[1]
    rows = pl.BlockSpec((tm, d), lambda i: (i, 0))
    return pl.pallas_call(
        _ffn_out_kernel, grid=(t // tm,),
        in_specs=[rows, rows, pl.BlockSpec((TOP_K, tm, d), lambda i: (0, i, 0)),
                  pl.BlockSpec((tm, LANES), lambda i: (i, 0)),
                  pl.BlockSpec((d, ff), lambda i: (0, 0)), pl.BlockSpec((d, ff), lambda i: (0, 0)),
                  pl.BlockSpec((ff, d), lambda i: (0, 0)), _mod_block(d, tm, geo)],
        out_specs=rows,
        out_shape=jax.ShapeDtypeStruct((t, d), f32),
        compiler_params=_cparams(("parallel",)), name="ffn_out",
    )(xs, h, yk, wrow, s_gate, s_up, s_down, mod5)


def _dispatch(experts, rank, counts, tl):
    t = experts.shape[1]
    n_assign = t * TOP_K
    counts = counts[:, 0].astype(jnp.int32)
    padded = (counts + MOE_BLOCK - 1) // MOE_BLOCK * MOE_BLOCK
    pad_end = jnp.cumsum(padded)
    pad_start = (pad_end - padded).astype(jnp.int32)
    slot = _slots(experts, rank, pad_start, tl)
    n_blocks = (n_assign + MOE_BLOCK - 1) // MOE_BLOCK + N_EXPERTS
    n_blocks = (n_blocks + MOE_SUB - 1) // MOE_SUB * MOE_SUB
    token = jnp.broadcast_to(jnp.arange(t, dtype=jnp.int32), (TOP_K, t))
    buf_t = jnp.zeros((n_blocks * MOE_BLOCK,), jnp.int32).at[slot.reshape(-1)].add(
        token.reshape(-1), unique_indices=True)
    first_row = jnp.arange(n_blocks, dtype=jnp.int32) * MOE_BLOCK
    block_e = jnp.minimum(jnp.sum(pad_end[None, :] <= first_row[:, None], axis=1), N_EXPERTS - 1)
    return slot, buf_t, block_e.astype(jnp.int32)


def _final_norm_kernel(x_ref, w_ref, o_ref):
    o_ref[...] = _norm_rows(x_ref[...], w_ref[...])


def _final_norm(xs, w, nctx, tm):
    t, d = xs.shape
    off = nctx // tm
    return pl.pallas_call(
        _final_norm_kernel, grid=((t - nctx) // tm,),
        in_specs=[pl.BlockSpec((tm, d), lambda i: (i + off, 0)), pl.BlockSpec((1, d), lambda i: (0, 0))],
        out_specs=pl.BlockSpec((tm, d), lambda i: (i, 0)),
        out_shape=jax.ShapeDtypeStruct((t - nctx, d), f32),
        compiler_params=_cparams(("parallel",)), name="final_norm",
    )(xs, w.reshape(1, d))


def _grid_sincos(rows, cols, dim):
    quarter = dim // 4
    omega = 1.0 / (POS_BASE ** (jnp.arange(quarter, dtype=f32) / quarter))

    def axis_embed(n):
        ang = jnp.arange(n, dtype=f32)[:, None] * omega
        return jnp.concatenate([jnp.sin(ang), jnp.cos(ang)], axis=-1)

    er, ec = axis_embed(rows), axis_embed(cols)
    pe = jnp.concatenate([jnp.broadcast_to(er[:, None], (rows, cols, dim // 2)),
                          jnp.broadcast_to(ec[None], (rows, cols, dim // 2))], axis=-1)
    return pe.reshape(rows * cols, dim)


def _lower_bounds(logits):
    p = jax.nn.softmax(logits.astype(f32), axis=1)
    return jnp.cumsum(p, axis=1) - p[:, :1]


def _split_w_in(w_in):
    a0 = 5 * HG_WIDTH + 2 * GLA_HEADS * GLA_DK + 2 * GLA_HEADS * GLA_DV
    a1 = a0 + 2 * GLA_RANK
    b0 = a1 + 2 * SSD_INNER + 2 * SSD_GROUPS * SSD_STATE
    b1 = b0 + 2 * SSD_HEADS
    main = jnp.concatenate([w_in[..., b1:], w_in[..., :a0], w_in[..., a1:b0]], axis=-1).astype(bf16)
    small = jnp.concatenate([w_in[..., a0:a1], w_in[..., b0:b1]], axis=-1)
    small = jnp.pad(small, ((0, 0), (0, 0), (0, LANES - small.shape[-1]))).astype(bf16)
    return main, small


def kernel(x, c, ctx, c_ctx, norm_mix_w, norm_ffn_w, final_norm_w, ada_w, ada_b, w_in,
           hg_lb_logits, hg_norm_w, gla_gate_w, gla_gate_b, gla_norm_w, ssd_conv_w, ssd_conv_b,
           ssd_a_log, ssd_dt_bias, ssd_d, ssd_norm_w, w_br_hg, w_br_gla, w_br_ssd, w_out,
           router_w, router_bias, exp_w_gate, exp_w_up, exp_w_down, sh_w_gate, sh_w_up, sh_w_down):
    nb_all, seq, d = x.shape
    lc = ctx.shape[1]
    depth = w_in.shape[0]
    tm = min(512, lc)
    tr = min(256, lc)
    r_scan = min(256, lc)
    c_hg = min(64, lc)
    c_gla = min(128, lc)
    c_ssd = min(128, lc)

    n_streams = 2 if nb_all % 2 == 0 else 1
    nb = nb_all // n_streams
    nctx = nb * lc
    geo = (nctx, seq, nb)

    pe = _grid_sincos(seq // GRID_W, GRID_W, d)
    lb = _lower_bounds(hg_lb_logits)
    w_main, w_small = _split_w_in(w_in)
    w_hg, w_gla, w_ssd, w_o = (w.astype(bf16) for w in (w_br_hg, w_br_gla, w_br_ssd, w_out))
    e_gate, e_up, e_down = (w.astype(bf16) for w in (exp_w_gate, exp_w_up, exp_w_down))
    s_gate, s_up, s_down = (w.astype(bf16) for w in (sh_w_gate, sh_w_up, sh_w_down))

    mods, xs = [], []
    for s in range(n_streams):
        bsl = slice(s * nb, (s + 1) * nb)
        c_all = jnp.zeros((SUBLANES, d), f32).at[:nb].set(c[bsl]).at[nb].set(c_ctx)
        m = _mods(c_all, ada_w, ada_b)
        mods.append(m.reshape(depth, SUBLANES, N_MOD, 1, d).transpose(0, 2, 1, 3, 4))
        xs.append(_embed(ctx[bsl].reshape(nctx, d), x[bsl].reshape(nb * seq, d), pe, tm))

    for l in range(depth):
        routed = []
        for s in range(n_streams):
            m = mods[s][l]
            p, psmall, psmall_t = _in_proj(xs[s], norm_mix_w[l], m[0], m[1], w_main[l], w_small[l], geo, tm)
            hg_f, hg_b = _hgrn2_scan(p, lb[0, l], lb[1, l], geo, c_hg, r_scan)
            gl_f, gl_b = _gla_scan(p, psmall, gla_gate_w[:, l], gla_gate_b[:, l], geo, c_gla, r_scan)
            xbc = _ssd_conv(p, ssd_conv_w[l], ssd_conv_b[l], geo, tr)
            sd_f, sd_b = _ssd_scan(xbc, psmall, psmall_t, ssd_a_log[:, l], ssd_dt_bias[:, l], geo, c_ssd,
                                   r_scan)
            xs[s] = _merge(xs[s], p, hg_f, hg_b, gl_f, gl_b, sd_f, sd_b, xbc, hg_norm_w[l], gla_norm_w[l],
                           ssd_d[l], ssd_norm_w[l], w_hg[l], w_gla[l], w_ssd[l], w_o[l], m[2], geo, tr)
            h2, experts, _, rank, wrow, counts = _router(xs[s], norm_ffn_w[l], m[3], m[4], router_w[l],
                                                         router_bias[l], geo, tm)
            slot, buf_t, block_e = _dispatch(experts, rank, counts, tm)
            xg = h2.at[buf_t].get(mode="promise_in_bounds")
            routed.append((h2, xg, block_e, slot, wrow))
        for s in range(n_streams):
            h2, xg, block_e, slot, wrow = routed[s]
            yg = _experts(xg, block_e, e_gate[l], e_up[l], e_down[l])
            yk = yg.at[slot.reshape(-1)].get(mode="promise_in_bounds").reshape(TOP_K, -1, d)
            xs[s] = _ffn_out(xs[s], h2, yk, wrow, s_gate[l], s_up[l], s_down[l], mods[s][l][5], geo, tr)

    out = [_final_norm(xs[s], final_norm_w, nctx, tm).reshape(nb, seq, d) for s in range(n_streams)]
    return jnp.concatenate(out, axis=0)
```

```python
import functools
import math

import numpy as np
import jax
import jax.numpy as jnp
from jax import lax
from jax.experimental import pallas as pl
from jax.experimental.pallas import tpu as pltpu

f32 = jnp.float32
bf16 = jnp.bfloat16
HIGHEST = lax.Precision.HIGHEST

EPS = 1e-6
POS_BASE = 10000.0
GRID_W = 64
N_MOD = 6
HG_HEADS = 4
HG_DK = 128
HG_WIDTH = HG_HEADS * HG_DK
GLA_HEADS = 4
GLA_DK = 64
GLA_DV = 128
GLA_RANK = 16
GLA_TAU = 16.0
SSD_HEADS = 8
SSD_HEADDIM = 64
SSD_INNER = SSD_HEADS * SSD_HEADDIM
SSD_GROUPS = 2
SSD_STATE = 128
SSD_CONV = 5
N_BRANCH = 3
N_EXPERTS = 64
TOP_K = 8
N_GROUPS = 8
TOPK_GROUPS = 4
ROUTE_SCALE = 2.5
MOE_BLOCK = 256
NEG_MASK = -1e4

LANES = 128
SUBLANES = 8
VMEM_BYTES = 64 * 1024 * 1024
VMEM_LIMIT = VMEM_BYTES * 7 // 8
EXP_CLAMP = 80.0

P_GATE = 0
P_HG = 3072
P_GLA = 5632
P_SSD = 7168
P_TOTAL = 8704
S_LRF, S_LRB, S_DTF, S_DTB = 0, 16, 32, 40


def _cparams(sem):
    return pltpu.CompilerParams(dimension_semantics=sem, vmem_limit_bytes=VMEM_LIMIT)


def _split_hi_lo(x):
    hi = x.astype(bf16)
    lo = (x - hi.astype(f32)).astype(bf16)
    return hi, lo


def _dot(a, b):
    return jnp.dot(a, b, preferred_element_type=f32)


def _dot_nt(a, b):
    return lax.dot_general(a, b, (((1,), (1,)), ((), ())), preferred_element_type=f32)


def _dot_tn(a, b):
    return lax.dot_general(a, b, (((0,), (0,)), ((), ())), preferred_element_type=f32)


def _dot2(m, x):
    hi, lo = _split_hi_lo(x)
    return _dot(m, hi) + _dot(m, lo)


def _dot2_l(x, m):
    hi, lo = _split_hi_lo(x)
    return _dot(hi, m) + _dot(lo, m)


def _silu(x):
    return x * jax.nn.sigmoid(x)


def _log_sigmoid(z):
    return jnp.minimum(z, 0.0) - jnp.log(1.0 + jnp.exp(-jnp.abs(z)))


def _softplus(z):
    return jnp.maximum(z, 0.0) + jnp.log(1.0 + jnp.exp(-jnp.abs(z)))


def _scan_mask(c, rev):
    r = lax.broadcasted_iota(jnp.int32, (c, c), 0)
    s = lax.broadcasted_iota(jnp.int32, (c, c), 1)
    return (r <= s) if rev else (r >= s)


def _mods_kernel(c_ref, w_ref, b_ref, o_ref):
    o_ref[...] = jnp.dot(_silu(c_ref[...]), w_ref[...], precision=HIGHEST,
                         preferred_element_type=f32) + b_ref[...]


def _mods(c_all, ada_w, ada_b):
    depth, d, n = ada_w.shape
    tn = 1024
    return pl.pallas_call(
        _mods_kernel,
        grid=(depth, n // tn),
        in_specs=[pl.BlockSpec((SUBLANES, d), lambda l, j: (0, 0)),
                  pl.BlockSpec((None, d, tn), lambda l, j: (l, 0, j)),
                  pl.BlockSpec((None, 1, tn), lambda l, j: (l, 0, j))],
        out_specs=pl.BlockSpec((None, SUBLANES, tn), lambda l, j: (l, 0, j)),
        out_shape=jax.ShapeDtypeStruct((depth, SUBLANES, n), f32),
        compiler_params=_cparams(("parallel", "parallel")),
        name="mods",
    )(c_all, ada_w, ada_b.reshape(depth, 1, n))


def _embed_kernel(ctx_ref, x_ref, pe_ref, o_ref, *, nct):
    i = pl.program_id(0)

    @pl.when(i < nct)
    def _():
        o_ref[...] = ctx_ref[...]

    @pl.when(i >= nct)
    def _():
        o_ref[...] = x_ref[...] + pe_ref[...]


def _embed(ctx2, x2, pe, te):
    nctx, d = ctx2.shape
    nlat = x2.shape[0]
    nct, npe = nctx // te, pe.shape[0] // te
    return pl.pallas_call(
        functools.partial(_embed_kernel, nct=nct),
        grid=((nctx + nlat) // te,),
        in_specs=[pl.BlockSpec((te, d), lambda i: (jnp.minimum(i, nct - 1), 0)),
                  pl.BlockSpec((te, d), lambda i: (jnp.maximum(i - nct, 0), 0)),
                  pl.BlockSpec((te, d), lambda i: (jnp.maximum(i - nct, 0) % npe, 0))],
        out_specs=pl.BlockSpec((te, d), lambda i: (i, 0)),
        out_shape=jax.ShapeDtypeStruct((nctx + nlat, d), f32),
        compiler_params=_cparams(("parallel",)),
        name="embed",
    )(ctx2, x2, pe)


def _norm_rows(x, w):
    return x * lax.rsqrt(jnp.mean(x * x, axis=-1, keepdims=True) + EPS) * w


def _beats(a, b, a_first):
    return (a >= b) if a_first else (a > b)


def _router_kernel(x_ref, w_ref, sh_ref, sc_ref, rwt_ref, rb_ref,
                   h_ref, e_ref, wt_ref, rank_ref, wrow_ref, cnt_ref, *, n_tiles):
    @pl.when(pl.program_id(0) == 0)
    def _():
        cnt_ref[...] = jnp.zeros_like(cnt_ref)

    @pl.when(pl.program_id(0) >= n_tiles)
    def _():
        h_ref[...] = jnp.zeros_like(h_ref)

    @pl.when(pl.program_id(0) < n_tiles)
    def _():
        _route_tile(x_ref, w_ref, sh_ref, sc_ref, rwt_ref, rb_ref,
                    h_ref, e_ref, wt_ref, rank_ref, wrow_ref, cnt_ref)


def _route_tile(x_ref, w_ref, sh_ref, sc_ref, rwt_ref, rb_ref,
                h_ref, e_ref, wt_ref, rank_ref, wrow_ref, cnt_ref):
    y = _norm_rows(x_ref[...], w_ref[...])
    h = y * (1.0 + sc_ref[...]) + sh_ref[...]
    h_ref[...] = h.astype(bf16)
    tm = h.shape[0]
    logits = lax.dot_general(rwt_ref[...], h, (((1,), (1,)), ((), ())), precision=HIGHEST,
                             preferred_element_type=f32)
    scores = jax.nn.sigmoid(logits)
    biased = scores + rb_ref[...]

    gsz = N_EXPERTS // N_GROUPS
    miota = lax.broadcasted_iota(jnp.int32, (gsz, tm), 0)
    blocks, gscore = [], []
    for g in range(N_GROUPS):
        blk = biased[g * gsz:(g + 1) * gsz, :]
        m1 = jnp.max(blk, axis=0, keepdims=True)
        i1 = jnp.min(jnp.where(blk == m1, miota, gsz), axis=0, keepdims=True)
        m2 = jnp.max(jnp.where(miota == i1, -jnp.inf, blk), axis=0, keepdims=True)
        blocks.append(blk)
        gscore.append(m1 + m2)
    vals = []
    for g in range(N_GROUPS):
        ahead = jnp.zeros((1, tm), jnp.int32)
        for o in range(N_GROUPS):
            if o != g:
                ahead = ahead + _beats(gscore[o], gscore[g], o < g).astype(jnp.int32)
        vals.append(jnp.where(ahead < TOPK_GROUPS, blocks[g], NEG_MASK))
    vals = jnp.concatenate(vals, axis=0)

    eiota = lax.broadcasted_iota(jnp.int32, (N_EXPERTS, tm), 0)
    member = jnp.zeros((N_EXPERTS, tm), f32)
    chosen, weights = [], []
    for k in range(TOP_K):
        m = jnp.max(vals, axis=0, keepdims=True)
        ei = jnp.min(jnp.where(vals == m, eiota, N_EXPERTS), axis=0, keepdims=True)
        sel = eiota == ei
        chosen.append(ei)
        weights.append(jnp.sum(jnp.where(sel, scores, 0.0), axis=0, keepdims=True))
        member = member + sel.astype(f32)
        vals = jnp.where(sel, -jnp.inf, vals)
    wsum = weights[0]
    for k in range(1, TOP_K):
        wsum = wsum + weights[k]

    r = lax.broadcasted_iota(jnp.int32, (tm, tm), 0)
    s = lax.broadcasted_iota(jnp.int32, (tm, tm), 1)
    member_bf = member.astype(bf16)
    rank_all = _dot(member_bf, (r < s).astype(bf16)) + cnt_ref[:, 0:1]
    for k in range(TOP_K):
        e_ref[k:k + 1, :] = chosen[k]
        wt_ref[k:k + 1, :] = weights[k] / wsum * ROUTE_SCALE
        rank_ref[k:k + 1, :] = jnp.sum(jnp.where(eiota == chosen[k], rank_all, 0.0), axis=0,
                                       keepdims=True).astype(jnp.int32)
    cnt_ref[...] = cnt_ref[...] + _dot(member_bf, jnp.ones((tm, LANES), bf16))
    eye = (lax.broadcasted_iota(jnp.int32, (TOP_K, LANES), 0)
           == lax.broadcasted_iota(jnp.int32, (TOP_K, LANES), 1)).astype(bf16)
    w_hi, w_lo = _split_hi_lo(wt_ref[...])
    wrow_ref[...] = _dot_tn(w_hi, eye) + _dot_tn(w_lo, eye)


def _mod_block(d, tm, geo):
    nctx, seq, nb = geo
    return pl.BlockSpec((None, 1, d), lambda i: (jnp.where(i * tm < nctx, nb, (i * tm - nctx) // seq), 0, 0))


def _router(xs, w, shift, scale, router_w, router_bias, geo, tm):
    t, d = xs.shape
    ne = router_w.shape[1]
    nt = t // tm
    nt_table = max(nt, -(-VMEM_BYTES // (tm * d * 2)))
    last = nt - 1
    nctx, seq, nb = geo

    def mod(i):
        r0 = jnp.minimum(i, last) * tm
        return (jnp.where(r0 < nctx, nb, (r0 - nctx) // seq), 0, 0)

    kt = pl.BlockSpec((TOP_K, tm), lambda i: (0, jnp.minimum(i, last)))
    return pl.pallas_call(
        functools.partial(_router_kernel, n_tiles=nt), grid=(nt_table,),
        in_specs=[pl.BlockSpec((tm, d), lambda i: (jnp.minimum(i, last), 0)),
                  pl.BlockSpec((1, d), lambda i: (0, 0)),
                  pl.BlockSpec((None, 1, d), mod), pl.BlockSpec((None, 1, d), mod),
                  pl.BlockSpec((ne, d), lambda i: (0, 0)),
                  pl.BlockSpec((ne, 1), lambda i: (0, 0))],
        out_specs=[pl.BlockSpec((tm, d), lambda i: (i, 0)), kt, kt, kt,
                   pl.BlockSpec((tm, LANES), lambda i: (jnp.minimum(i, last), 0)),
                   pl.BlockSpec((ne, LANES), lambda i: (0, 0))],
        out_shape=[jax.ShapeDtypeStruct((nt_table * tm, d), bf16),
                   jax.ShapeDtypeStruct((TOP_K, t), jnp.int32),
                   jax.ShapeDtypeStruct((TOP_K, t), f32),
                   jax.ShapeDtypeStruct((TOP_K, t), jnp.int32),
                   jax.ShapeDtypeStruct((t, LANES), f32),
                   jax.ShapeDtypeStruct((ne, LANES), f32)],
        compiler_params=_cparams(("arbitrary",)), name="router",
    )(xs, w.reshape(1, d), shift, scale, router_w.T, router_bias.astype(f32).reshape(ne, 1))


PROJ_TN = 512


def _in_proj_kernel(x_ref, nw_ref, sh_ref, sc_ref, w_ref, ws_ref, wst_ref, p_ref, ps_ref, pst_ref):
    y = _norm_rows(x_ref[...], nw_ref[...])
    h = (y * (1.0 + sc_ref[...]) + sh_ref[...]).astype(bf16)
    for j in range(p_ref.shape[1] // PROJ_TN):
        sl = slice(j * PROJ_TN, (j + 1) * PROJ_TN)
        p_ref[:, sl] = _dot(h, w_ref[:, sl]).astype(p_ref.dtype)
    ps_ref[...] = _dot(h, ws_ref[...])
    pst_ref[...] = _dot_nt(wst_ref[...], h)


def _in_proj(xs, nw, shift, scale, w_main, w_small, geo, tm):
    t, d = xs.shape
    n = w_main.shape[1]
    ns = w_small.shape[1]

    def resident(shape):
        return pl.BlockSpec(shape, lambda i: (0, 0), pipeline_mode=pl.Buffered(1))

    return pl.pallas_call(
        _in_proj_kernel, grid=(t // tm,),
        in_specs=[pl.BlockSpec((tm, d), lambda i: (i, 0)),
                  pl.BlockSpec((1, d), lambda i: (0, 0)),
                  _mod_block(d, tm, geo), _mod_block(d, tm, geo),
                  resident((d, n)), resident((d, ns)), resident((ns, d))],
        out_specs=[pl.BlockSpec((tm, n), lambda i: (i, 0)),
                   pl.BlockSpec((tm, ns), lambda i: (i, 0)),
                   pl.BlockSpec((ns, tm), lambda i: (0, i))],
        out_shape=[jax.ShapeDtypeStruct((t, n), bf16),
                   jax.ShapeDtypeStruct((t, ns), f32),
                   jax.ShapeDtypeStruct((ns, t), f32)],
        compiler_params=_cparams(("parallel",)), name="in_proj",
    )(xs, nw.reshape(1, d), shift, scale, w_main, w_small, w_small.T)


def _chunk_blocks(geo, c, rev):
    nctx, seq, nb = geo
    lc = nctx // nb
    ncc, nlc = lc // c, seq // c

    def blk(b, i):
        j_ctx = (ncc - 1 - i) if rev else i
        j_lat = (nlc - 1 - (i - ncc)) if rev else (i - ncc)
        return jnp.where(i < ncc, b * ncc + j_ctx, nctx // c + b * nlc + j_lat)

    return blk, ncc + nlc


def _sub_chunks(n_rows, c, rev):
    n = n_rows // c
    order = range(n - 1, -1, -1) if rev else range(n)
    return [slice(j * c, (j + 1) * c) for j in order]


def _decay_terms(q, k, g, mask, rev):
    c = q.shape[0]
    b = _dot2(mask.astype(bf16), g)
    last = 0 if rev else c - 1
    b_tot = b[last:last + 1, :]
    b_mid = b[c // 2:c // 2 + 1, :]
    qd = q * jnp.exp(jnp.minimum(b - b_mid, EXP_CLAMP))
    kd = k * jnp.exp(jnp.minimum(b_mid - b, EXP_CLAMP))
    qe = q * jnp.exp(b)
    ke = k * jnp.exp(b_tot - b)
    return qd.astype(bf16), kd.astype(bf16), qe.astype(bf16), ke.astype(bf16), jnp.exp(b_tot)


def _hgrn2_gates(z, lb):
    e = jnp.exp(-jnp.abs(z))
    inv = 1.0 / (1.0 + e)
    sig_pos = jnp.where(z >= 0, 1.0, e) * inv
    sig_neg = jnp.where(z >= 0, e, 1.0) * inv
    ls = jnp.minimum(z, 0.0) - jnp.log(1.0 + e)
    log_f = jnp.where(lb > 0.0, jnp.log(jnp.where(lb > 0.0, lb + (1.0 - lb) * sig_pos, 1.0)), ls)
    return (1.0 - lb) * sig_neg, log_f


def _hgrn2_kernel(qf_ref, zf_ref, vf_ref, qb_ref, zb_ref, vb_ref, lbf_ref, lbb_ref,
                  of_ref, ob_ref, sf_ref, sb_ref, *, c):
    @pl.when(pl.program_id(1) == 0)
    def _():
        sf_ref[...] = jnp.zeros_like(sf_ref)
        sb_ref[...] = jnp.zeros_like(sb_ref)

    for rev, q_ref, z_ref, v_ref, lb_ref, o_ref, s_ref in (
            (False, qf_ref, zf_ref, vf_ref, lbf_ref, of_ref, sf_ref),
            (True, qb_ref, zb_ref, vb_ref, lbb_ref, ob_ref, sb_ref)):
        mask = _scan_mask(c, rev)
        for rows in _sub_chunks(q_ref.shape[0], c, rev):
            q = q_ref[rows, :].astype(f32)
            k, g = _hgrn2_gates(z_ref[rows, :].astype(f32), lb_ref[...])
            v = v_ref[rows, :]
            qd, kd, qe, ke, dtot = _decay_terms(q, k, g, mask, rev)
            for h in range(HG_HEADS):
                sl = slice(h * HG_DK, (h + 1) * HG_DK)
                st = s_ref[h]
                att = jnp.where(mask, _dot_nt(qd[:, sl], kd[:, sl]), 0.0)
                o = _dot(att.astype(bf16), v[:, sl]) + _dot_nt(qe[:, sl], st.astype(bf16))
                o_ref[rows, sl] = o.astype(o_ref.dtype)
                s_ref[h] = st * dtot[:, sl] + _dot_tn(v[:, sl], ke[:, sl])


def _hgrn2_scan(p, lb_f, lb_b, geo, c, r):
    t = p.shape[0]
    nb = geo[2]
    blk_f, steps = _chunk_blocks(geo, r, False)
    blk_b, _ = _chunk_blocks(geo, r, True)
    w = HG_WIDTH
    c0 = P_HG // w

    def spec(blk, col):
        return pl.BlockSpec((r, w), lambda b, i: (blk(b, i), col))

    vec = pl.BlockSpec((1, w), lambda b, i: (0, 0))
    return pl.pallas_call(
        functools.partial(_hgrn2_kernel, c=c), grid=(nb, steps),
        in_specs=[spec(blk_f, c0), spec(blk_f, c0 + 1), spec(blk_f, c0 + 3),
                  spec(blk_b, c0), spec(blk_b, c0 + 2), spec(blk_b, c0 + 3), vec, vec],
        out_specs=[spec(blk_f, 0), spec(blk_b, 0)],
        out_shape=[jax.ShapeDtypeStruct((t, w), bf16)] * 2,
        scratch_shapes=[pltpu.VMEM((HG_HEADS, HG_DK, HG_DK), f32)] * 2,
        compiler_params=_cparams(("parallel", "arbitrary")), name="hgrn2_scan",
    )(p, p, p, p, p, p, lb_f.reshape(1, w), lb_b.reshape(1, w))


def _gla_kernel(qf_ref, kf_ref, vf_ref, sf_in_ref, qb_ref, kb_ref, vb_ref, sb_in_ref,
                gwf_ref, gwb_ref, gbf_ref, gbb_ref, of_ref, ob_ref, sf_ref, sb_ref, *, c):
    @pl.when(pl.program_id(1) == 0)
    def _():
        sf_ref[...] = jnp.zeros_like(sf_ref)
        sb_ref[...] = jnp.zeros_like(sb_ref)

    npair = GLA_HEADS // 2
    lane = lax.broadcasted_iota(jnp.int32, (1, LANES), 1)
    vrow = lax.broadcasted_iota(jnp.int32, (2 * GLA_DV, LANES), 0)
    vcol = lax.broadcasted_iota(jnp.int32, (2 * GLA_DV, LANES), 1)
    block_diag = (vrow // GLA_DV) == (vcol // GLA_DK)
    for rev, q_ref, k_ref, v_ref, sm_ref, gw_ref, gb_ref, o_ref, s_ref in (
            (False, qf_ref, kf_ref, vf_ref, sf_in_ref, gwf_ref, gbf_ref, of_ref, sf_ref),
            (True, qb_ref, kb_ref, vb_ref, sb_in_ref, gwb_ref, gbb_ref, ob_ref, sb_ref)):
        mask = _scan_mask(c, rev)
        for rows in _sub_chunks(q_ref.shape[0], c, rev):
            q = q_ref[rows, :].astype(f32) * (GLA_DK ** -0.5)
            k = k_ref[rows, :].astype(f32)
            v = v_ref[rows, :]
            pre = jnp.dot(sm_ref[rows, :], gw_ref[...], precision=HIGHEST,
                          preferred_element_type=f32) + gb_ref[...]
            g = _log_sigmoid(pre) / GLA_TAU
            qd, kd, qe, ke, dtot = _decay_terms(q, k, g, mask, rev)
            for p in range(npair):
                sl = slice(p * LANES, (p + 1) * LANES)
                vsl = slice(p * 2 * GLA_DV, (p + 1) * 2 * GLA_DV)
                st = s_ref[p]
                o_inter = _dot_nt(qe[:, sl], st.astype(bf16))
                for hh in range(2):
                    head = (lane // GLA_DK) == hh
                    qh = jnp.where(head, qd[:, sl], jnp.zeros_like(qd[:, sl]))
                    att = jnp.where(mask, _dot_nt(qh, kd[:, sl]), 0.0)
                    osl = slice((2 * p + hh) * GLA_DV, (2 * p + hh + 1) * GLA_DV)
                    o = _dot(att.astype(bf16), v[:, osl]) + o_inter[:, hh * GLA_DV:(hh + 1) * GLA_DV]
                    o_ref[rows, osl] = o.astype(o_ref.dtype)
                upd = st * dtot[:, sl] + _dot_tn(v[:, vsl], ke[:, sl])
                s_ref[p] = jnp.where(block_diag, upd, 0.0)


def _gla_scan(p, psmall, gate_w, gate_b, geo, c, r):
    t = p.shape[0]
    nb = geo[2]
    blk_f, steps = _chunk_blocks(geo, r, False)
    blk_b, _ = _chunk_blocks(geo, r, True)
    wk, wv = GLA_HEADS * GLA_DK, GLA_HEADS * GLA_DV

    def spec(blk, width, col):
        return pl.BlockSpec((r, width), lambda b, i: (blk(b, i), col))

    def const(shape):
        return pl.BlockSpec(shape, lambda b, i: (0, 0))

    gwf = jnp.zeros((LANES, wk), f32).at[S_LRF:S_LRF + GLA_RANK].set(gate_w[0])
    gwb = jnp.zeros((LANES, wk), f32).at[S_LRB:S_LRB + GLA_RANK].set(gate_w[1])
    qcol, kcol, vcol = P_GLA // wk, P_GLA // wk + 1, (P_GLA + 2 * wk) // wv
    return pl.pallas_call(
        functools.partial(_gla_kernel, c=c), grid=(nb, steps),
        in_specs=[spec(blk_f, wk, qcol), spec(blk_f, wk, kcol), spec(blk_f, wv, vcol), spec(blk_f, LANES, 0),
                  spec(blk_b, wk, qcol), spec(blk_b, wk, kcol), spec(blk_b, wv, vcol), spec(blk_b, LANES, 0),
                  const((LANES, wk)), const((LANES, wk)), const((1, wk)), const((1, wk))],
        out_specs=[spec(blk_f, wv, 0), spec(blk_b, wv, 0)],
        out_shape=[jax.ShapeDtypeStruct((t, wv), bf16)] * 2,
        scratch_shapes=[pltpu.VMEM((GLA_HEADS // 2, 2 * GLA_DV, 2 * GLA_DK), f32)] * 2,
        compiler_params=_cparams(("parallel", "arbitrary")), name="gla_scan",
    )(p, p, p, psmall, p, p, p, psmall, gwf, gwb, gate_b[0].reshape(1, wk), gate_b[1].reshape(1, wk))


HALO = SUBLANES


def _conv_kernel(prev_ref, cur_ref, next_ref, w_ref, b_ref, o_ref, ext_ref, *, geo, tc):
    nctx, seq, nb = geo
    r0 = pl.program_id(0) * tc
    in_ctx = r0 < nctx
    seg_len = jnp.where(in_ctx, nctx // nb, seq)
    pos = jnp.where(in_ctx, r0 % (nctx // nb), (r0 - nctx) % seq)
    first = pos == 0
    last = pos + tc == seg_len
    ext_ref[0:HALO, :] = jnp.where(first, 0.0, prev_ref[...].astype(f32))
    ext_ref[HALO:HALO + tc, :] = cur_ref[...].astype(f32)
    ext_ref[HALO + tc:HALO + tc + HALO, :] = jnp.where(last, 0.0, next_ref[...].astype(f32))
    acc = jnp.zeros(o_ref.shape, f32) + b_ref[...]
    for j in range(SSD_CONV):
        off = HALO - SSD_CONV // 2 + j
        acc = acc + ext_ref[off:off + tc, :] * w_ref[j:j + 1, :]
    o_ref[...] = _silu(acc).astype(o_ref.dtype)


def _ssd_conv(p, conv_w, conv_b, geo, tc):
    t = p.shape[0]
    nch = conv_w.shape[1]
    tw = 512
    col0 = (P_SSD + SSD_INNER) // tw
    hb = tc // HALO
    nhb = t // HALO
    return pl.pallas_call(
        functools.partial(_conv_kernel, geo=geo, tc=tc),
        grid=(t // tc, nch // tw),
        in_specs=[pl.BlockSpec((HALO, tw), lambda i, j: (jnp.maximum(i * hb - 1, 0), col0 + j)),
                  pl.BlockSpec((tc, tw), lambda i, j: (i, col0 + j)),
                  pl.BlockSpec((HALO, tw), lambda i, j: (jnp.minimum((i + 1) * hb, nhb - 1), col0 + j)),
                  pl.BlockSpec((SSD_CONV, tw), lambda i, j: (0, j)),
                  pl.BlockSpec((1, tw), lambda i, j: (0, j))],
        out_specs=pl.BlockSpec((tc, tw), lambda i, j: (i, j)),
        out_shape=jax.ShapeDtypeStruct((t, nch), bf16),
        scratch_shapes=[pltpu.VMEM((tc + 2 * HALO, tw), f32)],
        compiler_params=_cparams(("parallel", "parallel")), name="ssd_conv",
    )(p, p, p, conv_w, conv_b.reshape(1, nch))


def _ssd_kernel(xf_ref, bf_ref, cf_ref, smf_ref, smtf_ref, xb_ref, bb_ref, cb_ref, smb_ref, smtb_ref,
                arow_ref, brow_ref, acol_ref, bcol_ref, ef_ref, eb_ref,
                yf_ref, yb_ref, sf_ref, sb_ref, *, c):
    @pl.when(pl.program_id(1) == 0)
    def _():
        sf_ref[...] = jnp.zeros_like(sf_ref)
        sb_ref[...] = jnp.zeros_like(sb_ref)

    gw = SSD_INNER // SSD_GROUPS
    lane = lax.broadcasted_iota(jnp.int32, (1, LANES), 1)
    for d, rev, x_ref, bm_ref, cm_ref, sm_ref, smt_ref, e_ref, y_ref, s_ref, lane0 in (
            (0, False, xf_ref, bf_ref, cf_ref, smf_ref, smtf_ref, ef_ref, yf_ref, sf_ref, S_DTF),
            (1, True, xb_ref, bb_ref, cb_ref, smb_ref, smtb_ref, eb_ref, yb_ref, sb_ref, S_DTB)):
        mask = _scan_mask(c, rev)
        mask_bf = mask.astype(bf16)
        mask_t_bf = _scan_mask(c, not rev).astype(bf16)
        expand = e_ref[...]
        for rows in _sub_chunks(x_ref.shape[0], c, rev):
            dt = _softplus(sm_ref[rows, :] + brow_ref[d:d + 1, :])
            cum = _dot2(mask_bf, dt * arow_ref[d:d + 1, :])
            last = 0 if rev else c - 1
            cum_end = cum[last:last + 1, :]
            dt_t = _softplus(smt_ref[:, rows] + bcol_ref[d])
            cum_t = _dot2_l(dt_t * acol_ref[d], mask_t_bf)

            x = x_ref[rows, :].astype(f32)
            xdt = x * _dot2_l(dt, expand)
            w = (xdt * _dot2_l(jnp.exp(cum_end - cum), expand)).astype(bf16)
            xdt = xdt.astype(bf16)
            dec_t = _dot2_l(jnp.exp(cum), expand)
            dec_end = _dot2_l(jnp.broadcast_to(jnp.exp(cum_end), (SUBLANES, LANES)), expand)[0:1, :]
            bm = bm_ref[rows, :]
            cm = cm_ref[rows, :]
            for g in range(SSD_GROUPS):
                gsl = slice(g * SSD_STATE, (g + 1) * SSD_STATE)
                hsl = slice(g * gw, (g + 1) * gw)
                st = s_ref[:, hsl]
                scores = _dot_nt(cm[:, gsl], bm[:, gsl])
                y_inter = _dot(cm[:, gsl], st.astype(bf16)) * dec_t[:, hsl]
                for tile in range(gw // LANES):
                    tsl = slice(g * gw + tile * LANES, g * gw + (tile + 1) * LANES)
                    xt = xdt[:, tsl]
                    acc = y_inter[:, tile * LANES:(tile + 1) * LANES]
                    for hh in range(LANES // SSD_HEADDIM):
                        h = (g * gw + tile * LANES) // SSD_HEADDIM + hh
                        diff = cum[:, lane0 + h:lane0 + h + 1] - cum_t[h:h + 1, :]
                        seg = jnp.where(mask, jnp.exp(jnp.minimum(diff, 0.0)), 0.0)
                        xh = jnp.where((lane // SSD_HEADDIM) == hh, xt, jnp.zeros_like(xt))
                        acc = acc + _dot((scores * seg).astype(bf16), xh)
                    y_ref[rows, tsl] = acc.astype(y_ref.dtype)
                s_ref[:, hsl] = st * dec_end[:, hsl] + _dot_tn(bm[:, gsl], w[:, hsl])


def _ssd_scan(xbc, psmall, psmall_t, a_log, dt_bias, geo, c, r):
    t = xbc.shape[0]
    nb = geo[2]
    blk_f, steps = _chunk_blocks(geo, r, False)
    blk_b, _ = _chunk_blocks(geo, r, True)
    wi, wn = SSD_INNER, SSD_GROUPS * SSD_STATE

    def spec(blk, width, col):
        return pl.BlockSpec((r, width), lambda b, i: (blk(b, i), col))

    def spec_t(blk, row):
        return pl.BlockSpec((SUBLANES, r), lambda b, i: (row, blk(b, i)))

    def const(shape):
        return pl.BlockSpec(shape, lambda b, i: (0,) * len(shape))

    a = -jnp.exp(a_log.astype(f32))
    arow = jnp.zeros((2, LANES), f32).at[0, S_DTF:S_DTF + SSD_HEADS].set(a[0]).at[1, S_DTB:S_DTB + SSD_HEADS].set(a[1])
    brow = jnp.zeros((2, LANES), f32).at[0, S_DTF:S_DTF + SSD_HEADS].set(dt_bias[0]).at[1, S_DTB:S_DTB + SSD_HEADS].set(dt_bias[1])
    head_of_lane = np.arange(wi) // SSD_HEADDIM
    ef = jnp.asarray((np.arange(LANES)[:, None] == S_DTF + head_of_lane[None, :]), bf16)
    eb = jnp.asarray((np.arange(LANES)[:, None] == S_DTB + head_of_lane[None, :]), bf16)
    xcol, bcol, ccol = 0, wi // wn, wi // wn + 1
    return pl.pallas_call(
        functools.partial(_ssd_kernel, c=c), grid=(nb, steps),
        in_specs=[spec(blk_f, wi, xcol), spec(blk_f, wn, bcol), spec(blk_f, wn, ccol), spec(blk_f, LANES, 0),
                  spec_t(blk_f, S_DTF // SUBLANES),
                  spec(blk_b, wi, xcol), spec(blk_b, wn, bcol), spec(blk_b, wn, ccol), spec(blk_b, LANES, 0),
                  spec_t(blk_b, S_DTB // SUBLANES),
                  const((2, LANES)), const((2, LANES)), const((2, SSD_HEADS, 1)), const((2, SSD_HEADS, 1)),
                  const((LANES, wi)), const((LANES, wi))],
        out_specs=[spec(blk_f, wi, 0), spec(blk_b, wi, 0)],
        out_shape=[jax.ShapeDtypeStruct((t, wi), bf16)] * 2,
        scratch_shapes=[pltpu.VMEM((SSD_STATE, wi), f32)] * 2,
        compiler_params=_cparams(("parallel", "arbitrary")), name="ssd_scan",
    )(xbc, xbc, xbc, psmall, psmall_t, xbc, xbc, xbc, psmall, psmall_t,
      arow, brow, a.reshape(2, SSD_HEADS, 1), dt_bias.astype(f32).reshape(2, SSD_HEADS, 1), ef, eb)


def _group_norm(o, w, width):
    parts = []
    for j in range(o.shape[1] // width):
        blk = o[:, j * width:(j + 1) * width]
        parts.append(blk * lax.rsqrt(jnp.mean(blk * blk, axis=-1, keepdims=True) + EPS))
    return jnp.concatenate(parts, axis=1) * w


def _merge_kernel(x_ref, hgf_ref, hgb_ref, hgg_ref, glf_ref, glb_ref, glg_ref,
                  sdf_ref, sdb_ref, sdx_ref, sdz_ref, gate_ref,
                  hgw_ref, glw_ref, sdd_ref, sdw_ref, wh_ref, wg_ref, ws_ref, wo_ref, m2_ref, o_ref):
    o_hg = _group_norm(hgf_ref[...].astype(f32) + hgb_ref[...].astype(f32), hgw_ref[...], HG_DK)
    o_hg = o_hg * _silu(hgg_ref[...].astype(f32))
    o_gl = _group_norm(glf_ref[...].astype(f32) + glb_ref[...].astype(f32), glw_ref[...], GLA_DV)
    o_gl = o_gl * _silu(glg_ref[...].astype(f32))
    y = sdf_ref[...].astype(f32) + sdb_ref[...].astype(f32) + sdd_ref[...] * sdx_ref[...].astype(f32)
    o_sd = _group_norm(y * _silu(sdz_ref[...].astype(f32)), sdw_ref[...], SSD_INNER // SSD_GROUPS)
    d = x_ref.shape[1]
    gate = jax.nn.sigmoid(gate_ref[...].astype(f32))
    y = (gate[:, 0:d] * _dot(o_hg.astype(bf16), wh_ref[...])
         + gate[:, d:2 * d] * _dot(o_gl.astype(bf16), wg_ref[...])
         + gate[:, 2 * d:3 * d] * _dot(o_sd.astype(bf16), ws_ref[...]))
    o_ref[...] = x_ref[...] + m2_ref[...] * _dot(y.astype(bf16), wo_ref[...])


def _merge(xs, p, hg_f, hg_b, gl_f, gl_b, sd_f, sd_b, xbc, hg_norm_w, gla_norm_w, ssd_d, ssd_norm_w,
           w_hg, w_gla, w_ssd, w_out, mod2, geo, tm):
    t, d = xs.shape
    w = 512

    def rows(width, col):
        return pl.BlockSpec((tm, width), lambda i: (i, col))

    def const(shape):
        return pl.BlockSpec(shape, lambda i: (0, 0))

    tile4 = lambda v: jnp.tile(v, w // v.shape[0]).reshape(1, w)
    return pl.pallas_call(
        _merge_kernel, grid=(t // tm,),
        in_specs=[rows(d, 0),
                  rows(w, 0), rows(w, 0), rows(w, (P_HG + 4 * w) // w),
                  rows(w, 0), rows(w, 0), rows(w, (P_GLA + 2 * w) // w),
                  rows(w, 0), rows(w, 0), rows(w, 0), rows(w, P_SSD // w),
                  rows(3 * d, P_GATE // (3 * d)),
                  const((1, w)), const((1, w)), const((1, w)), const((1, w)),
                  const((w, d)), const((w, d)), const((w, d)), const((d, d)),
                  _mod_block(d, tm, geo)],
        out_specs=rows(d, 0),
        out_shape=jax.ShapeDtypeStruct((t, d), f32),
        compiler_params=_cparams(("parallel",)), name="merge",
    )(xs, hg_f, hg_b, p, gl_f, gl_b, p, sd_f, sd_b, xbc, p, p,
      tile4(hg_norm_w), tile4(gla_norm_w), jnp.repeat(ssd_d, SSD_HEADDIM).reshape(1, w), ssd_norm_w.reshape(1, w),
      w_hg, w_gla, w_ssd, w_out, mod2)


MOE_SUB = 4


def _expert_kernel(be_ref, x_ref, *refs):
    o_ref = refs[-1]
    first = pl.program_id(0) * MOE_SUB
    same = be_ref[first + 1] == be_ref[first]
    for s in range(2, MOE_SUB):
        same = jnp.logical_and(same, be_ref[first + s] == be_ref[first])

    def ffn(x, wg_ref, wu_ref, wd_ref):
        a = _silu(_dot(x, wg_ref[...])) * _dot(x, wu_ref[...])
        return _dot(a.astype(bf16), wd_ref[...]).astype(o_ref.dtype)

    @pl.when(same)
    def _():
        o_ref[...] = ffn(x_ref[...], *refs[0:3])

    @pl.when(jnp.logical_not(same))
    def _():
        for s in range(MOE_SUB):
            rows = slice(s * MOE_BLOCK, (s + 1) * MOE_BLOCK)
            o_ref[rows, :] = ffn(x_ref[rows, :], *refs[3 * s:3 * s + 3])


def _experts(xg, block_e, w_gate, w_up, w_down):
    npad, d = xg.shape
    ff = w_gate.shape[2]
    rows = MOE_SUB * MOE_BLOCK
    w_specs = []
    for s in range(MOE_SUB):
        pick = lambda i, be, s=s: (be[i * MOE_SUB + s], 0, 0)
        w_specs += [pl.BlockSpec((None, d, ff), pick), pl.BlockSpec((None, d, ff), pick),
                    pl.BlockSpec((None, ff, d), pick)]
    grid_spec = pltpu.PrefetchScalarGridSpec(
        num_scalar_prefetch=1, grid=(npad // rows,),
        in_specs=[pl.BlockSpec((rows, d), lambda i, be: (i, 0))] + w_specs,
        out_specs=pl.BlockSpec((rows, d), lambda i, be: (i, 0)))
    return pl.pallas_call(
        _expert_kernel, grid_spec=grid_spec,
        out_shape=jax.ShapeDtypeStruct((npad, d), bf16),
        compiler_params=_cparams(("arbitrary",)), name="experts",
    )(block_e, xg, *([w_gate, w_up, w_down] * MOE_SUB))


def _slots_kernel(ps_ref, e_ref, r_ref, o_ref):
    e = e_ref[...]
    acc = r_ref[...]
    for j in range(N_EXPERTS):
        acc = acc + jnp.where(e == j, ps_ref[j], 0)
    o_ref[...] = acc


def _slots(experts, rank, pad_start, tl):
    k, t = experts.shape
    spec = pl.BlockSpec((k, tl), lambda i, ps: (0, i))
    return pl.pallas_call(
        _slots_kernel,
        grid_spec=pltpu.PrefetchScalarGridSpec(num_scalar_prefetch=1, grid=(t // tl,),
                                               in_specs=[spec, spec], out_specs=spec),
        out_shape=jax.ShapeDtypeStruct((k, t), jnp.int32),
        compiler_params=_cparams(("parallel",)), name="slots",
    )(pad_start, experts, rank)


def _ffn_out_kernel(x_ref, h_ref, y_ref, wr_ref, sg_ref, su_ref, sd_ref, m5_ref, o_ref):
    h = h_ref[...]
    a = _silu(_dot(h, sg_ref[...])) * _dot(h, su_ref[...])
    acc = _dot(a.astype(bf16), sd_ref[...])
    wr = wr_ref[...]
    for k in range(TOP_K):
        acc = acc + y_ref[k].astype(f32) * wr[:, k:k + 1]
    o_ref[...] = x_ref[...] + m5_ref[...] * acc


def _ffn_out(xs, h, yk, wrow, s_gate, s_up, s_down, mod5, geo, tm):
    t, d = xs.shape
    ff = s_gate.shape[1]
    rows = pl.BlockSpec((tm, d), lambda i: (i, 0))
    return pl.pallas_call(
        _ffn_out_kernel, grid=(t // tm,),
        in_specs=[rows, rows, pl.BlockSpec((TOP_K, tm, d), lambda i: (0, i, 0)),
                  pl.BlockSpec((tm, LANES), lambda i: (i, 0)),
                  pl.BlockSpec((d, ff), lambda i: (0, 0)), pl.BlockSpec((d, ff), lambda i: (0, 0)),
                  pl.BlockSpec((ff, d), lambda i: (0, 0)), _mod_block(d, tm, geo)],
        out_specs=rows,
        out_shape=jax.ShapeDtypeStruct((t, d), f32),
        compiler_params=_cparams(("parallel",)), name="ffn_out",
    )(xs, h, yk, wrow, s_gate, s_up, s_down, mod5)


def _dispatch(experts, rank, counts, tl):
    t = experts.shape[1]
    n_assign = t * TOP_K
    counts = counts[:, 0].astype(jnp.int32)
    padded = (counts + MOE_BLOCK - 1) // MOE_BLOCK * MOE_BLOCK
    pad_end = jnp.cumsum(padded)
    pad_start = (pad_end - padded).astype(jnp.int32)
    slot = _slots(experts, rank, pad_start, tl)
    n_blocks = (n_assign + MOE_BLOCK - 1) // MOE_BLOCK + N_EXPERTS
    n_blocks = (n_blocks + MOE_SUB - 1) // MOE_SUB * MOE_SUB
    token = jnp.broadcast_to(jnp.arange(t, dtype=jnp.int32), (TOP_K, t))
    buf_t = jnp.zeros((n_blocks * MOE_BLOCK,), jnp.int32).at[slot.reshape(-1)].add(
        token.reshape(-1), unique_indices=True)
    first_row = jnp.arange(n_blocks, dtype=jnp.int32) * MOE_BLOCK
    block_e = jnp.minimum(jnp.sum(pad_end[None, :] <= first_row[:, None], axis=1), N_EXPERTS - 1)
    return slot, buf_t, block_e.astype(jnp.int32)


def _final_norm_kernel(x_ref, w_ref, o_ref):
    o_ref[...] = _norm_rows(x_ref[...], w_ref[...])


def _final_norm(xs, w, nctx, tm):
    t, d = xs.shape
    off = nctx // tm
    return pl.pallas_call(
        _final_norm_kernel, grid=((t - nctx) // tm,),
        in_specs=[pl.BlockSpec((tm, d), lambda i: (i + off, 0)), pl.BlockSpec((1, d), lambda i: (0, 0))],
        out_specs=pl.BlockSpec((tm, d), lambda i: (i, 0)),
        out_shape=jax.ShapeDtypeStruct((t - nctx, d), f32),
        compiler_params=_cparams(("parallel",)), name="final_norm",
    )(xs, w.reshape(1, d))


def _grid_sincos(rows, cols, dim):
    quarter = dim // 4
    omega = 1.0 / (POS_BASE ** (jnp.arange(quarter, dtype=f32) / quarter))

    def axis_embed(n):
        ang = jnp.arange(n, dtype=f32)[:, None] * omega
        return jnp.concatenate([jnp.sin(ang), jnp.cos(ang)], axis=-1)

    er, ec = axis_embed(rows), axis_embed(cols)
    pe = jnp.concatenate([jnp.broadcast_to(er[:, None], (rows, cols, dim // 2)),
                          jnp.broadcast_to(ec[None], (rows, cols, dim // 2))], axis=-1)
    return pe.reshape(rows * cols, dim)


def _lower_bounds(logits):
    p = jax.nn.softmax(logits.astype(f32), axis=1)
    return jnp.cumsum(p, axis=1) - p[:, :1]


def _split_w_in(w_in):
    a0 = 5 * HG_WIDTH + 2 * GLA_HEADS * GLA_DK + 2 * GLA_HEADS * GLA_DV
    a1 = a0 + 2 * GLA_RANK
    b0 = a1 + 2 * SSD_INNER + 2 * SSD_GROUPS * SSD_STATE
    b1 = b0 + 2 * SSD_HEADS
    main = jnp.concatenate([w_in[..., b1:], w_in[..., :a0], w_in[..., a1:b0]], axis=-1).astype(bf16)
    small = jnp.concatenate([w_in[..., a0:a1], w_in[..., b0:b1]], axis=-1)
    small = jnp.pad(small, ((0, 0), (0, 0), (0, LANES - small.shape[-1]))).astype(bf16)
    return main, small


def kernel(x, c, ctx, c_ctx, norm_mix_w, norm_ffn_w, final_norm_w, ada_w, ada_b, w_in,
           hg_lb_logits, hg_norm_w, gla_gate_w, gla_gate_b, gla_norm_w, ssd_conv_w, ssd_conv_b,
           ssd_a_log, ssd_dt_bias, ssd_d, ssd_norm_w, w_br_hg, w_br_gla, w_br_ssd, w_out,
           router_w, router_bias, exp_w_gate, exp_w_up, exp_w_down, sh_w_gate, sh_w_up, sh_w_down):
    nb_all, seq, d = x.shape
    lc = ctx.shape[1]
    depth = w_in.shape[0]
    tm = min(512, lc)
    tr = min(256, lc)
    r_scan = min(256, lc)
    c_hg = min(64, lc)
    c_gla = min(128, lc)
    c_ssd = min(128, lc)

    n_streams = 2 if nb_all % 2 == 0 else 1
    nb = nb_all // n_streams
    nctx = nb * lc
    geo = (nctx, seq, nb)

    pe = _grid_sincos(seq // GRID_W, GRID_W, d)
    lb = _lower_bounds(hg_lb_logits)
    w_main, w_small = _split_w_in(w_in)
    w_hg, w_gla, w_ssd, w_o = (w.astype(bf16) for w in (w_br_hg, w_br_gla, w_br_ssd, w_out))
    e_gate, e_up, e_down = (w.astype(bf16) for w in (exp_w_gate, exp_w_up, exp_w_down))
    s_gate, s_up, s_down = (w.astype(bf16) for w in (sh_w_gate, sh_w_up, sh_w_down))

    mods, xs = [], []
    for s in range(n_streams):
        bsl = slice(s * nb, (s + 1) * nb)
        c_all = jnp.zeros((SUBLANES, d), f32).at[:nb].set(c[bsl]).at[nb].set(c_ctx)
        m = _mods(c_all, ada_w, ada_b)
        mods.append(m.reshape(depth, SUBLANES, N_MOD, 1, d).transpose(0, 2, 1, 3, 4))
        xs.append(_embed(ctx[bsl].reshape(nctx, d), x[bsl].reshape(nb * seq, d), pe, tm))

    for l in range(depth):
        routed = []
        for s in range(n_streams):
            m = mods[s][l]
            p, psmall, psmall_t = _in_proj(xs[s], norm_mix_w[l], m[0], m[1], w_main[l], w_small[l], geo, tm)
            hg_f, hg_b = _hgrn2_scan(p, lb[0, l], lb[1, l], geo, c_hg, r_scan)
            gl_f, gl_b = _gla_scan(p, psmall, gla_gate_w[:, l], gla_gate_b[:, l], geo, c_gla, r_scan)
            xbc = _ssd_conv(p, ssd_conv_w[l], ssd_conv_b[l], geo, tr)
            sd_f, sd_b = _ssd_scan(xbc, psmall, psmall_t, ssd_a_log[:, l], ssd_dt_bias[:, l], geo, c_ssd,
                                   r_scan)
            xs[s] = _merge(xs[s], p, hg_f, hg_b, gl_f, gl_b, sd_f, sd_b, xbc, hg_norm_w[l], gla_norm_w[l],
                           ssd_d[l], ssd_norm_w[l], w_hg[l], w_gla[l], w_ssd[l], w_o[l], m[2], geo, tr)
            h2, experts, _, rank, wrow, counts = _router(xs[s], norm_ffn_w[l], m[3], m[4], router_w[l],
                                                         router_bias[l], geo, tm)
            slot, buf_t, block_e = _dispatch(experts, rank, counts, tm)
            xg = h2.at[buf_t].get(mode="promise_in_bounds")
            routed.append((h2, xg, block_e, slot, wrow))
        for s in range(n_streams):
            h2, xg, block_e, slot, wrow = routed[s]
            yg = _experts(xg, block_e, e_gate[l], e_up[l], e_down[l])
            yk = yg.at[slot.reshape(-1)].get(mode="promise_in_bounds").reshape(TOP_K, -1, d)
            xs[s] = _ffn_out(xs[s], h2, yk, wrow, s_gate[l], s_up[l], s_down[l], mods[s][l][5], geo, tr)

    out = [_final_norm(xs[s], final_norm_w, nctx, tm).reshape(nb, seq, d) for s in range(n_streams)]
    return jnp.concatenate(out, axis=0)
```

```python
import functools
import math

import numpy as np
import jax
import jax.numpy as jnp
from jax import lax
from jax.experimental import pallas as pl
from jax.experimental.pallas import tpu as pltpu

f32 = jnp.float32
bf16 = jnp.bfloat16
HIGHEST = lax.Precision.HIGHEST

EPS = 1e-6
POS_BASE = 10000.0
GRID_W = 64
N_MOD = 6
HG_HEADS = 4
HG_DK = 128
HG_WIDTH = HG_HEADS * HG_DK
GLA_HEADS = 4
GLA_DK = 64
GLA_DV = 128
GLA_RANK = 16
GLA_TAU = 16.0
SSD_HEADS = 8
SSD_HEADDIM = 64
SSD_INNER = SSD_HEADS * SSD_HEADDIM
SSD_GROUPS = 2
SSD_STATE = 128
SSD_CONV = 5
N_BRANCH = 3
N_EXPERTS = 64
TOP_K = 8
N_GROUPS = 8
TOPK_GROUPS = 4
ROUTE_SCALE = 2.5
MOE_BLOCK = 256
NEG_MASK = -1e4

LANES = 128
SUBLANES = 8
VMEM_BYTES = 64 * 1024 * 1024
VMEM_LIMIT = VMEM_BYTES * 7 // 8
EXP_CLAMP = 80.0

P_GATE = 0
P_HG = 3072
P_GLA = 5632
P_SSD = 7168
P_TOTAL = 8704
S_LRF, S_LRB, S_DTF, S_DTB = 0, 16, 32, 40


def _cparams(sem):
    return pltpu.CompilerParams(dimension_semantics=sem, vmem_limit_bytes=VMEM_LIMIT)


def _split_hi_lo(x):
    hi = x.astype(bf16)
    lo = (x - hi.astype(f32)).astype(bf16)
    return hi, lo


def _dot(a, b):
    return jnp.dot(a, b, preferred_element_type=f32)


def _dot_nt(a, b):
    return lax.dot_general(a, b, (((1,), (1,)), ((), ())), preferred_element_type=f32)


def _dot_tn(a, b):
    return lax.dot_general(a, b, (((0,), (0,)), ((), ())), preferred_element_type=f32)


def _dot2(m, x):
    hi, lo = _split_hi_lo(x)
    return _dot(m, hi) + _dot(m, lo)


def _dot2_l(x, m):
    hi, lo = _split_hi_lo(x)
    return _dot(hi, m) + _dot(lo, m)


def _silu(x):
    return x * jax.nn.sigmoid(x)


def _log_sigmoid(z):
    return jnp.minimum(z, 0.0) - jnp.log(1.0 + jnp.exp(-jnp.abs(z)))


def _softplus(z):
    return jnp.maximum(z, 0.0) + jnp.log(1.0 + jnp.exp(-jnp.abs(z)))


def _scan_mask(c, rev):
    r = lax.broadcasted_iota(jnp.int32, (c, c), 0)
    s = lax.broadcasted_iota(jnp.int32, (c, c), 1)
    return (r <= s) if rev else (r >= s)


def _mods_kernel(c_ref, w_ref, b_ref, o_ref):
    o_ref[...] = jnp.dot(_silu(c_ref[...]), w_ref[...], precision=HIGHEST,
                         preferred_element_type=f32) + b_ref[...]


def _mods(c_all, ada_w, ada_b):
    depth, d, n = ada_w.shape
    tn = 1024
    return pl.pallas_call(
        _mods_kernel,
        grid=(depth, n // tn),
        in_specs=[pl.BlockSpec((SUBLANES, d), lambda l, j: (0, 0)),
                  pl.BlockSpec((None, d, tn), lambda l, j: (l, 0, j)),
                  pl.BlockSpec((None, 1, tn), lambda l, j: (l, 0, j))],
        out_specs=pl.BlockSpec((None, SUBLANES, tn), lambda l, j: (l, 0, j)),
        out_shape=jax.ShapeDtypeStruct((depth, SUBLANES, n), f32),
        compiler_params=_cparams(("parallel", "parallel")),
        name="mods",
    )(c_all, ada_w, ada_b.reshape(depth, 1, n))


def _embed_kernel(ctx_ref, x_ref, pe_ref, o_ref, *, nct):
    i = pl.program_id(0)

    @pl.when(i < nct)
    def _():
        o_ref[...] = ctx_ref[...]

    @pl.when(i >= nct)
    def _():
        o_ref[...] = x_ref[...] + pe_ref[...]


def _embed(ctx2, x2, pe, te):
    nctx, d = ctx2.shape
    nlat = x2.shape[0]
    nct, npe = nctx // te, pe.shape[0] // te
    return pl.pallas_call(
        functools.partial(_embed_kernel, nct=nct),
        grid=((nctx + nlat) // te,),
        in_specs=[pl.BlockSpec((te, d), lambda i: (jnp.minimum(i, nct - 1), 0)),
                  pl.BlockSpec((te, d), lambda i: (jnp.maximum(i - nct, 0), 0)),
                  pl.BlockSpec((te, d), lambda i: (jnp.maximum(i - nct, 0) % npe, 0))],
        out_specs=pl.BlockSpec((te, d), lambda i: (i, 0)),
        out_shape=jax.ShapeDtypeStruct((nctx + nlat, d), f32),
        compiler_params=_cparams(("parallel",)),
        name="embed",
    )(ctx2, x2, pe)


def _norm_rows(x, w):
    return x * lax.rsqrt(jnp.mean(x * x, axis=-1, keepdims=True) + EPS) * w


def _beats(a, b, a_first):
    return (a >= b) if a_first else (a > b)


def _router_kernel(x_ref, w_ref, sh_ref, sc_ref, rwt_ref, rb_ref,
                   h_ref, e_ref, wt_ref, rank_ref, wrow_ref, cnt_ref, *, n_tiles):
    @pl.when(pl.program_id(0) == 0)
    def _():
        cnt_ref[...] = jnp.zeros_like(cnt_ref)

    @pl.when(pl.program_id(0) >= n_tiles)
    def _():
        h_ref[...] = jnp.zeros_like(h_ref)

    @pl.when(pl.program_id(0) < n_tiles)
    def _():
        _route_tile(x_ref, w_ref, sh_ref, sc_ref, rwt_ref, rb_ref,
                    h_ref, e_ref, wt_ref, rank_ref, wrow_ref, cnt_ref)


def _route_tile(x_ref, w_ref, sh_ref, sc_ref, rwt_ref, rb_ref,
                h_ref, e_ref, wt_ref, rank_ref, wrow_ref, cnt_ref):
    y = _norm_rows(x_ref[...], w_ref[...])
    h = y * (1.0 + sc_ref[...]) + sh_ref[...]
    h_ref[...] = h.astype(bf16)
    tm = h.shape[0]
    logits = lax.dot_general(rwt_ref[...], h, (((1,), (1,)), ((), ())), precision=HIGHEST,
                             preferred_element_type=f32)
    scores = jax.nn.sigmoid(logits)
    biased = scores + rb_ref[...]

    gsz = N_EXPERTS // N_GROUPS
    miota = lax.broadcasted_iota(jnp.int32, (gsz, tm), 0)
    blocks, gscore = [], []
    for g in range(N_GROUPS):
        blk = biased[g * gsz:(g + 1) * gsz, :]
        m1 = jnp.max(blk, axis=0, keepdims=True)
        i1 = jnp.min(jnp.where(blk == m1, miota, gsz), axis=0, keepdims=True)
        m2 = jnp.max(jnp.where(miota == i1, -jnp.inf, blk), axis=0, keepdims=True)
        blocks.append(blk)
        gscore.append(m1 + m2)
    vals = []
    for g in range(N_GROUPS):
        ahead = jnp.zeros((1, tm), jnp.int32)
        for o in range(N_GROUPS):
            if o != g:
                ahead = ahead + _beats(gscore[o], gscore[g], o < g).astype(jnp.int32)
        vals.append(jnp.where(ahead < TOPK_GROUPS, blocks[g], NEG_MASK))
    vals = jnp.concatenate(vals, axis=0)

    eiota = lax.broadcasted_iota(jnp.int32, (N_EXPERTS, tm), 0)
    member = jnp.zeros((N_EXPERTS, tm), f32)
    chosen, weights = [], []
    for k in range(TOP_K):
        m = jnp.max(vals, axis=0, keepdims=True)
        ei = jnp.min(jnp.where(vals == m, eiota, N_EXPERTS), axis=0, keepdims=True)
        sel = eiota == ei
        chosen.append(ei)
        weights.append(jnp.sum(jnp.where(sel, scores, 0.0), axis=0, keepdims=True))
        member = member + sel.astype(f32)
        vals = jnp.where(sel, -jnp.inf, vals)
    wsum = weights[0]
    for k in range(1, TOP_K):
        wsum = wsum + weights[k]

    r = lax.broadcasted_iota(jnp.int32, (tm, tm), 0)
    s = lax.broadcasted_iota(jnp.int32, (tm, tm), 1)
    member_bf = member.astype(bf16)
    rank_all = _dot(member_bf, (r < s).astype(bf16)) + cnt_ref[:, 0:1]
    for k in range(TOP_K):
        e_ref[k:k + 1, :] = chosen[k]
        wt_ref[k:k + 1, :] = weights[k] / wsum * ROUTE_SCALE
        rank_ref[k:k + 1, :] = jnp.sum(jnp.where(eiota == chosen[k], rank_all, 0.0), axis=0,
                                       keepdims=True).astype(jnp.int32)
    cnt_ref[...] = cnt_ref[...] + _dot(member_bf, jnp.ones((tm, LANES), bf16))
    eye = (lax.broadcasted_iota(jnp.int32, (TOP_K, LANES), 0)
           == lax.broadcasted_iota(jnp.int32, (TOP_K, LANES), 1)).astype(bf16)
    w_hi, w_lo = _split_hi_lo(wt_ref[...])
    wrow_ref[...] = _dot_tn(w_hi, eye) + _dot_tn(w_lo, eye)


def _mod_block(d, tm, geo):
    nctx, seq, nb = geo
    return pl.BlockSpec((None, 1, d), lambda i: (jnp.where(i * tm < nctx, nb, (i * tm - nctx) // seq), 0, 0))


def _router(xs, w, shift, scale, router_w, router_bias, geo, tm):
    t, d = xs.shape
    ne = router_w.shape[1]
    nt = t // tm
    nt_table = max(nt, -(-VMEM_BYTES // (tm * d * 2)))
    last = nt - 1
    nctx, seq, nb = geo

    def mod(i):
        r0 = jnp.minimum(i, last) * tm
        return (jnp.where(r0 < nctx, nb, (r0 - nctx) // seq), 0, 0)

    kt = pl.BlockSpec((TOP_K, tm), lambda i: (0, jnp.minimum(i, last)))
    return pl.pallas_call(
        functools.partial(_router_kernel, n_tiles=nt), grid=(nt_table,),
        in_specs=[pl.BlockSpec((tm, d), lambda i: (jnp.minimum(i, last), 0)),
                  pl.BlockSpec((1, d), lambda i: (0, 0)),
                  pl.BlockSpec((None, 1, d), mod), pl.BlockSpec((None, 1, d), mod),
                  pl.BlockSpec((ne, d), lambda i: (0, 0)),
                  pl.BlockSpec((ne, 1), lambda i: (0, 0))],
        out_specs=[pl.BlockSpec((tm, d), lambda i: (i, 0)), kt, kt, kt,
                   pl.BlockSpec((tm, LANES), lambda i: (jnp.minimum(i, last), 0)),
                   pl.BlockSpec((ne, LANES), lambda i: (0, 0))],
        out_shape=[jax.ShapeDtypeStruct((nt_table * tm, d), bf16),
                   jax.ShapeDtypeStruct((TOP_K, t), jnp.int32),
                   jax.ShapeDtypeStruct((TOP_K, t), f32),
                   jax.ShapeDtypeStruct((TOP_K, t), jnp.int32),
                   jax.ShapeDtypeStruct((t, LANES), f32),
                   jax.ShapeDtypeStruct((ne, LANES), f32)],
        compiler_params=_cparams(("arbitrary",)), name="router",
    )(xs, w.reshape(1, d), shift, scale, router_w.T, router_bias.astype(f32).reshape(ne, 1))


PROJ_TN = 512


def _in_proj_kernel(x_ref, nw_ref, sh_ref, sc_ref, w_ref, ws_ref, wst_ref, p_ref, ps_ref, pst_ref):
    y = _norm_rows(x_ref[...], nw_ref[...])
    h = (y * (1.0 + sc_ref[...]) + sh_ref[...]).astype(bf16)
    for j in range(p_ref.shape[1] // PROJ_TN):
        sl = slice(j * PROJ_TN, (j + 1) * PROJ_TN)
        p_ref[:, sl] = _dot(h, w_ref[:, sl]).astype(p_ref.dtype)
    ps_ref[...] = _dot(h, ws_ref[...])
    pst_ref[...] = _dot_nt(wst_ref[...], h)


def _in_proj(xs, nw, shift, scale, w_main, w_small, geo, tm):
    t, d = xs.shape
    n = w_main.shape[1]
    ns = w_small.shape[1]

    def resident(shape):
        return pl.BlockSpec(shape, lambda i: (0, 0), pipeline_mode=pl.Buffered(1))

    return pl.pallas_call(
        _in_proj_kernel, grid=(t // tm,),
        in_specs=[pl.BlockSpec((tm, d), lambda i: (i, 0)),
                  pl.BlockSpec((1, d), lambda i: (0, 0)),
                  _mod_block(d, tm, geo), _mod_block(d, tm, geo),
                  resident((d, n)), resident((d, ns)), resident((ns, d))],
        out_specs=[pl.BlockSpec((tm, n), lambda i: (i, 0)),
                   pl.BlockSpec((tm, ns), lambda i: (i, 0)),
                   pl.BlockSpec((ns, tm), lambda i: (0, i))],
        out_shape=[jax.ShapeDtypeStruct((t, n), bf16),
                   jax.ShapeDtypeStruct((t, ns), f32),
                   jax.ShapeDtypeStruct((ns, t), f32)],
        compiler_params=_cparams(("parallel",)), name="in_proj",
    )(xs, nw.reshape(1, d), shift, scale, w_main, w_small, w_small.T)


def _chunk_blocks(geo, c, rev):
    nctx, seq, nb = geo
    lc = nctx // nb
    ncc, nlc = lc // c, seq // c

    def blk(b, i):
        j_ctx = (ncc - 1 - i) if rev else i
        j_lat = (nlc - 1 - (i - ncc)) if rev else (i - ncc)
        return jnp.where(i < ncc, b * ncc + j_ctx, nctx // c + b * nlc + j_lat)

    return blk, ncc + nlc


def _sub_chunks(n_rows, c, rev):
    n = n_rows // c
    order = range(n - 1, -1, -1) if rev else range(n)
    return [slice(j * c, (j + 1) * c) for j in order]


def _decay_terms(q, k, g, mask, rev):
    c = q.shape[0]
    b = _dot2(mask.astype(bf16), g)
    last = 0 if rev else c - 1
    b_tot = b[last:last + 1, :]
    b_mid = b[c // 2:c // 2 + 1, :]
    qd = q * jnp.exp(jnp.minimum(b - b_mid, EXP_CLAMP))
    kd = k * jnp.exp(jnp.minimum(b_mid - b, EXP_CLAMP))
    qe = q * jnp.exp(b)
    ke = k * jnp.exp(b_tot - b)
    return qd.astype(bf16), kd.astype(bf16), qe.astype(bf16), ke.astype(bf16), jnp.exp(b_tot)


def _hgrn2_gates(z, lb):
    e = jnp.exp(-jnp.abs(z))
    inv = 1.0 / (1.0 + e)
    sig_pos = jnp.where(z >= 0, 1.0, e) * inv
    sig_neg = jnp.where(z >= 0, e, 1.0) * inv
    ls = jnp.minimum(z, 0.0) - jnp.log(1.0 + e)
    log_f = jnp.where(lb > 0.0, jnp.log(jnp.where(lb > 0.0, lb + (1.0 - lb) * sig_pos, 1.0)), ls)
    return (1.0 - lb) * sig_neg, log_f


def _hgrn2_kernel(qf_ref, zf_ref, vf_ref, qb_ref, zb_ref, vb_ref, lbf_ref, lbb_ref,
                  of_ref, ob_ref, sf_ref, sb_ref, *, c):
    @pl.when(pl.program_id(1) == 0)
    def _():
        sf_ref[...] = jnp.zeros_like(sf_ref)
        sb_ref[...] = jnp.zeros_like(sb_ref)

    for rev, q_ref, z_ref, v_ref, lb_ref, o_ref, s_ref in (
            (False, qf_ref, zf_ref, vf_ref, lbf_ref, of_ref, sf_ref),
            (True, qb_ref, zb_ref, vb_ref, lbb_ref, ob_ref, sb_ref)):
        mask = _scan_mask(c, rev)
        for rows in _sub_chunks(q_ref.shape[0], c, rev):
            q = q_ref[rows, :].astype(f32)
            k, g = _hgrn2_gates(z_ref[rows, :].astype(f32), lb_ref[...])
            v = v_ref[rows, :]
            qd, kd, qe, ke, dtot = _decay_terms(q, k, g, mask, rev)
            for h in range(HG_HEADS):
                sl = slice(h * HG_DK, (h + 1) * HG_DK)
                st = s_ref[h]
                att = jnp.where(mask, _dot_nt(qd[:, sl], kd[:, sl]), 0.0)
                o = _dot(att.astype(bf16), v[:, sl]) + _dot_nt(qe[:, sl], st.astype(bf16))
                o_ref[rows, sl] = o.astype(o_ref.dtype)
                s_ref[h] = st * dtot[:, sl] + _dot_tn(v[:, sl], ke[:, sl])


def _hgrn2_scan(p, lb_f, lb_b, geo, c, r):
    t = p.shape[0]
    nb = geo[2]
    blk_f, steps = _chunk_blocks(geo, r, False)
    blk_b, _ = _chunk_blocks(geo, r, True)
    w = HG_WIDTH
    c0 = P_HG // w

    def spec(blk, col):
        return pl.BlockSpec((r, w), lambda b, i: (blk(b, i), col))

    vec = pl.BlockSpec((1, w), lambda b, i: (0, 0))
    return pl.pallas_call(
        functools.partial(_hgrn2_kernel, c=c), grid=(nb, steps),
        in_specs=[spec(blk_f, c0), spec(blk_f, c0 + 1), spec(blk_f, c0 + 3),
                  spec(blk_b, c0), spec(blk_b, c0 + 2), spec(blk_b, c0 + 3), vec, vec],
        out_specs=[spec(blk_f, 0), spec(blk_b, 0)],
        out_shape=[jax.ShapeDtypeStruct((t, w), bf16)] * 2,
        scratch_shapes=[pltpu.VMEM((HG_HEADS, HG_DK, HG_DK), f32)] * 2,
        compiler_params=_cparams(("parallel", "arbitrary")), name="hgrn2_scan",
    )(p, p, p, p, p, p, lb_f.reshape(1, w), lb_b.reshape(1, w))


def _gla_kernel(qf_ref, kf_ref, vf_ref, sf_in_ref, qb_ref, kb_ref, vb_ref, sb_in_ref,
                gwf_ref, gwb_ref, gbf_ref, gbb_ref, of_ref, ob_ref, sf_ref, sb_ref, *, c):
    @pl.when(pl.program_id(1) == 0)
    def _():
        sf_ref[...] = jnp.zeros_like(sf_ref)
        sb_ref[...] = jnp.zeros_like(sb_ref)

    npair = GLA_HEADS // 2
    lane = lax.broadcasted_iota(jnp.int32, (1, LANES), 1)
    vrow = lax.broadcasted_iota(jnp.int32, (2 * GLA_DV, LANES), 0)
    vcol = lax.broadcasted_iota(jnp.int32, (2 * GLA_DV, LANES), 1)
    block_diag = (vrow // GLA_DV) == (vcol // GLA_DK)
    for rev, q_ref, k_ref, v_ref, sm_ref, gw_ref, gb_ref, o_ref, s_ref in (
            (False, qf_ref, kf_ref, vf_ref, sf_in_ref, gwf_ref, gbf_ref, of_ref, sf_ref),
            (True, qb_ref, kb_ref, vb_ref, sb_in_ref, gwb_ref, gbb_ref, ob_ref, sb_ref)):
        mask = _scan_mask(c, rev)
        for rows in _sub_chunks(q_ref.shape[0], c, rev):
            q = q_ref[rows, :].astype(f32) * (GLA_DK ** -0.5)
            k = k_ref[rows, :].astype(f32)
            v = v_ref[rows, :]
            pre = jnp.dot(sm_ref[rows, :], gw_ref[...], precision=HIGHEST,
                          preferred_element_type=f32) + gb_ref[...]
            g = _log_sigmoid(pre) / GLA_TAU
            qd, kd, qe, ke, dtot = _decay_terms(q, k, g, mask, rev)
            for p in range(npair):
                sl = slice(p * LANES, (p + 1) * LANES)
                vsl = slice(p * 2 * GLA_DV, (p + 1) * 2 * GLA_DV)
                st = s_ref[p]
                o_inter = _dot_nt(qe[:, sl], st.astype(bf16))
                for hh in range(2):
                    head = (lane // GLA_DK) == hh
                    qh = jnp.where(head, qd[:, sl], jnp.zeros_like(qd[:, sl]))
                    att = jnp.where(mask, _dot_nt(qh, kd[:, sl]), 0.0)
                    osl = slice((2 * p + hh) * GLA_DV, (2 * p + hh + 1) * GLA_DV)
                    o = _dot(att.astype(bf16), v[:, osl]) + o_inter[:, hh * GLA_DV:(hh + 1) * GLA_DV]
                    o_ref[rows, osl] = o.astype(o_ref.dtype)
                upd = st * dtot[:, sl] + _dot_tn(v[:, vsl], ke[:, sl])
                s_ref[p] = jnp.where(block_diag, upd, 0.0)


def _gla_scan(p, psmall, gate_w, gate_b, geo, c, r):
    t = p.shape[0]
    nb = geo[2]
    blk_f, steps = _chunk_blocks(geo, r, False)
    blk_b, _ = _chunk_blocks(geo, r, True)
    wk, wv = GLA_HEADS * GLA_DK, GLA_HEADS * GLA_DV

    def spec(blk, width, col):
        return pl.BlockSpec((r, width), lambda b, i: (blk(b, i), col))

    def const(shape):
        return pl.BlockSpec(shape, lambda b, i: (0, 0))

    gwf = jnp.zeros((LANES, wk), f32).at[S_LRF:S_LRF + GLA_RANK].set(gate_w[0])
    gwb = jnp.zeros((LANES, wk), f32).at[S_LRB:S_LRB + GLA_RANK].set(gate_w[1])
    qcol, kcol, vcol = P_GLA // wk, P_GLA // wk + 1, (P_GLA + 2 * wk) // wv
    return pl.pallas_call(
        functools.partial(_gla_kernel, c=c), grid=(nb, steps),
        in_specs=[spec(blk_f, wk, qcol), spec(blk_f, wk, kcol), spec(blk_f, wv, vcol), spec(blk_f, LANES, 0),
                  spec(blk_b, wk, qcol), spec(blk_b, wk, kcol), spec(blk_b, wv, vcol), spec(blk_b, LANES, 0),
                  const((LANES, wk)), const((LANES, wk)), const((1, wk)), const((1, wk))],
        out_specs=[spec(blk_f, wv, 0), spec(blk_b, wv, 0)],
        out_shape=[jax.ShapeDtypeStruct((t, wv), bf16)] * 2,
        scratch_shapes=[pltpu.VMEM((GLA_HEADS // 2, 2 * GLA_DV, 2 * GLA_DK), f32)] * 2,
        compiler_params=_cparams(("parallel", "arbitrary")), name="gla_scan",
    )(p, p, p, psmall, p, p, p, psmall, gwf, gwb, gate_b[0].reshape(1, wk), gate_b[1].reshape(1, wk))


HALO = SUBLANES


def _conv_kernel(prev_ref, cur_ref, next_ref, w_ref, b_ref, o_ref, ext_ref, *, geo, tc):
    nctx, seq, nb = geo
    r0 = pl.program_id(0) * tc
    in_ctx = r0 < nctx
    seg_len = jnp.where(in_ctx, nctx // nb, seq)
    pos = jnp.where(in_ctx, r0 % (nctx // nb), (r0 - nctx) % seq)
    first = pos == 0
    last = pos + tc == seg_len
    ext_ref[0:HALO, :] = jnp.where(first, 0.0, prev_ref[...].astype(f32))
    ext_ref[HALO:HALO + tc, :] = cur_ref[...].astype(f32)
    ext_ref[HALO + tc:HALO + tc + HALO, :] = jnp.where(last, 0.0, next_ref[...].astype(f32))
    acc = jnp.zeros(o_ref.shape, f32) + b_ref[...]
    for j in range(SSD_CONV):
        off = HALO - SSD_CONV // 2 + j
        acc = acc + ext_ref[off:off + tc, :] * w_ref[j:j + 1, :]
    o_ref[...] = _silu(acc).astype(o_ref.dtype)


def _ssd_conv(p, conv_w, conv_b, geo, tc):
    t = p.shape[0]
    nch = conv_w.shape[1]
    tw = 512
    col0 = (P_SSD + SSD_INNER) // tw
    hb = tc // HALO
    nhb = t // HALO
    return pl.pallas_call(
        functools.partial(_conv_kernel, geo=geo, tc=tc),
        grid=(t // tc, nch // tw),
        in_specs=[pl.BlockSpec((HALO, tw), lambda i, j: (jnp.maximum(i * hb - 1, 0), col0 + j)),
                  pl.BlockSpec((tc, tw), lambda i, j: (i, col0 + j)),
                  pl.BlockSpec((HALO, tw), lambda i, j: (jnp.minimum((i + 1) * hb, nhb - 1), col0 + j)),
                  pl.BlockSpec((SSD_CONV, tw), lambda i, j: (0, j)),
                  pl.BlockSpec((1, tw), lambda i, j: (0, j))],
        out_specs=pl.BlockSpec((tc, tw), lambda i, j: (i, j)),
        out_shape=jax.ShapeDtypeStruct((t, nch), bf16),
        scratch_shapes=[pltpu.VMEM((tc + 2 * HALO, tw), f32)],
        compiler_params=_cparams(("parallel", "parallel")), name="ssd_conv",
    )(p, p, p, conv_w, conv_b.reshape(1, nch))


def _ssd_kernel(xf_ref, bf_ref, cf_ref, smf_ref, smtf_ref, xb_ref, bb_ref, cb_ref, smb_ref, smtb_ref,
                arow_ref, brow_ref, acol_ref, bcol_ref, ef_ref, eb_ref,
                yf_ref, yb_ref, sf_ref, sb_ref, *, c):
    @pl.when(pl.program_id(1) == 0)
    def _():
        sf_ref[...] = jnp.zeros_like(sf_ref)
        sb_ref[...] = jnp.zeros_like(sb_ref)

    gw = SSD_INNER // SSD_GROUPS
    lane = lax.broadcasted_iota(jnp.int32, (1, LANES), 1)
    for d, rev, x_ref, bm_ref, cm_ref, sm_ref, smt_ref, e_ref, y_ref, s_ref, lane0 in (
            (0, False, xf_ref, bf_ref, cf_ref, smf_ref, smtf_ref, ef_ref, yf_ref, sf_ref, S_DTF),
            (1, True, xb_ref, bb_ref, cb_ref, smb_ref, smtb_ref, eb_ref, yb_ref, sb_ref, S_DTB)):
        mask = _scan_mask(c, rev)
        mask_bf = mask.astype(bf16)
        mask_t_bf = _scan_mask(c, not rev).astype(bf16)
        expand = e_ref[...]
        for rows in _sub_chunks(x_ref.shape[0], c, rev):
            dt = _softplus(sm_ref[rows, :] + brow_ref[d:d + 1, :])
            cum = _dot2(mask_bf, dt * arow_ref[d:d + 1, :])
            last = 0 if rev else c - 1
            cum_end = cum[last:last + 1, :]
            dt_t = _softplus(smt_ref[:, rows] + bcol_ref[d])
            cum_t = _dot2_l(dt_t * acol_ref[d], mask_t_bf)

            x = x_ref[rows, :].astype(f32)
            xdt = x * _dot2_l(dt, expand)
            w = (xdt * _dot2_l(jnp.exp(cum_end - cum), expand)).astype(bf16)
            xdt = xdt.astype(bf16)
            dec_t = _dot2_l(jnp.exp(cum), expand)
            dec_end = _dot2_l(jnp.broadcast_to(jnp.exp(cum_end), (SUBLANES, LANES)), expand)[0:1, :]
            bm = bm_ref[rows, :]
            cm = cm_ref[rows, :]
            for g in range(SSD_GROUPS):
                gsl = slice(g * SSD_STATE, (g + 1) * SSD_STATE)
                hsl = slice(g * gw, (g + 1) * gw)
                st = s_ref[:, hsl]
                scores = _dot_nt(cm[:, gsl], bm[:, gsl])
                y_inter = _dot(cm[:, gsl], st.astype(bf16)) * dec_t[:, hsl]
                for tile in range(gw // LANES):
                    tsl = slice(g * gw + tile * LANES, g * gw + (tile + 1) * LANES)
                    xt = xdt[:, tsl]
                    acc = y_inter[:, tile * LANES:(tile + 1) * LANES]
                    for hh in range(LANES // SSD_HEADDIM):
                        h = (g * gw + tile * LANES) // SSD_HEADDIM + hh
                        diff = cum[:, lane0 + h:lane0 + h + 1] - cum_t[h:h + 1, :]
                        seg = jnp.where(mask, jnp.exp(jnp.minimum(diff, 0.0)), 0.0)
                        xh = jnp.where((lane // SSD_HEADDIM) == hh, xt, jnp.zeros_like(xt))
                        acc = acc + _dot((scores * seg).astype(bf16), xh)
                    y_ref[rows, tsl] = acc.astype(y_ref.dtype)
                s_ref[:, hsl] = st * dec_end[:, hsl] + _dot_tn(bm[:, gsl], w[:, hsl])


def _ssd_scan(xbc, psmall, psmall_t, a_log, dt_bias, geo, c, r):
    t = xbc.shape[0]
    nb = geo[2]
    blk_f, steps = _chunk_blocks(geo, r, False)
    blk_b, _ = _chunk_blocks(geo, r, True)
    wi, wn = SSD_INNER, SSD_GROUPS * SSD_STATE

    def spec(blk, width, col):
        return pl.BlockSpec((r, width), lambda b, i: (blk(b, i), col))

    def spec_t(blk, row):
        return pl.BlockSpec((SUBLANES, r), lambda b, i: (row, blk(b, i)))

    def const(shape):
        return pl.BlockSpec(shape, lambda b, i: (0,) * len(shape))

    a = -jnp.exp(a_log.astype(f32))
    arow = jnp.zeros((2, LANES), f32).at[0, S_DTF:S_DTF + SSD_HEADS].set(a[0]).at[1, S_DTB:S_DTB + SSD_HEADS].set(a[1])
    brow = jnp.zeros((2, LANES), f32).at[0, S_DTF:S_DTF + SSD_HEADS].set(dt_bias[0]).at[1, S_DTB:S_DTB + SSD_HEADS].set(dt_bias[1])
    head_of_lane = np.arange(wi) // SSD_HEADDIM
    ef = jnp.asarray((np.arange(LANES)[:, None] == S_DTF + head_of_lane[None, :]), bf16)
    eb = jnp.asarray((np.arange(LANES)[:, None] == S_DTB + head_of_lane[None, :]), bf16)
    xcol, bcol, ccol = 0, wi // wn, wi // wn + 1
    return pl.pallas_call(
        functools.partial(_ssd_kernel, c=c), grid=(nb, steps),
        in_specs=[spec(blk_f, wi, xcol), spec(blk_f, wn, bcol), spec(blk_f, wn, ccol), spec(blk_f, LANES, 0),
                  spec_t(blk_f, S_DTF // SUBLANES),
                  spec(blk_b, wi, xcol), spec(blk_b, wn, bcol), spec(blk_b, wn, ccol), spec(blk_b, LANES, 0),
                  spec_t(blk_b, S_DTB // SUBLANES),
                  const((2, LANES)), const((2, LANES)), const((2, SSD_HEADS, 1)), const((2, SSD_HEADS, 1)),
                  const((LANES, wi)), const((LANES, wi))],
        out_specs=[spec(blk_f, wi, 0), spec(blk_b, wi, 0)],
        out_shape=[jax.ShapeDtypeStruct((t, wi), bf16)] * 2,
        scratch_shapes=[pltpu.VMEM((SSD_STATE, wi), f32)] * 2,
        compiler_params=_cparams(("parallel", "arbitrary")), name="ssd_scan",
    )(xbc, xbc, xbc, psmall, psmall_t, xbc, xbc, xbc, psmall, psmall_t,
      arow, brow, a.reshape(2, SSD_HEADS, 1), dt_bias.astype(f32).reshape(2, SSD_HEADS, 1), ef, eb)


def _group_norm(o, w, width):
    parts = []
    for j in range(o.shape[1] // width):
        blk = o[:, j * width:(j + 1) * width]
        parts.append(blk * lax.rsqrt(jnp.mean(blk * blk, axis=-1, keepdims=True) + EPS))
    return jnp.concatenate(parts, axis=1) * w


def _merge_kernel(x_ref, hgf_ref, hgb_ref, hgg_ref, glf_ref, glb_ref, glg_ref,
                  sdf_ref, sdb_ref, sdx_ref, sdz_ref, gate_ref,
                  hgw_ref, glw_ref, sdd_ref, sdw_ref, wh_ref, wg_ref, ws_ref, wo_ref, m2_ref, o_ref):
    o_hg = _group_norm(hgf_ref[...].astype(f32) + hgb_ref[...].astype(f32), hgw_ref[...], HG_DK)
    o_hg = o_hg * _silu(hgg_ref[...].astype(f32))
    o_gl = _group_norm(glf_ref[...].astype(f32) + glb_ref[...].astype(f32), glw_ref[...], GLA_DV)
    o_gl = o_gl * _silu(glg_ref[...].astype(f32))
    y = sdf_ref[...].astype(f32) + sdb_ref[...].astype(f32) + sdd_ref[...] * sdx_ref[...].astype(f32)
    o_sd = _group_norm(y * _silu(sdz_ref[...].astype(f32)), sdw_ref[...], SSD_INNER // SSD_GROUPS)
    d = x_ref.shape[1]
    gate = jax.nn.sigmoid(gate_ref[...].astype(f32))
    y = (gate[:, 0:d] * _dot(o_hg.astype(bf16), wh_ref[...])
         + gate[:, d:2 * d] * _dot(o_gl.astype(bf16), wg_ref[...])
         + gate[:, 2 * d:3 * d] * _dot(o_sd.astype(bf16), ws_ref[...]))
    o_ref[...] = x_ref[...] + m2_ref[...] * _dot(y.astype(bf16), wo_ref[...])


def _merge(xs, p, hg_f, hg_b, gl_f, gl_b, sd_f, sd_b, xbc, hg_norm_w, gla_norm_w, ssd_d, ssd_norm_w,
           w_hg, w_gla, w_ssd, w_out, mod2, geo, tm):
    t, d = xs.shape
    w = 512

    def rows(width, col):
        return pl.BlockSpec((tm, width), lambda i: (i, col))

    def const(shape):
        return pl.BlockSpec(shape, lambda i: (0, 0))

    tile4 = lambda v: jnp.tile(v, w // v.shape[0]).reshape(1, w)
    return pl.pallas_call(
        _merge_kernel, grid=(t // tm,),
        in_specs=[rows(d, 0),
                  rows(w, 0), rows(w, 0), rows(w, (P_HG + 4 * w) // w),
                  rows(w, 0), rows(w, 0), rows(w, (P_GLA + 2 * w) // w),
                  rows(w, 0), rows(w, 0), rows(w, 0), rows(w, P_SSD // w),
                  rows(3 * d, P_GATE // (3 * d)),
                  const((1, w)), const((1, w)), const((1, w)), const((1, w)),
                  const((w, d)), const((w, d)), const((w, d)), const((d, d)),
                  _mod_block(d, tm, geo)],
        out_specs=rows(d, 0),
        out_shape=jax.ShapeDtypeStruct((t, d), f32),
        compiler_params=_cparams(("parallel",)), name="merge",
    )(xs, hg_f, hg_b, p, gl_f, gl_b, p, sd_f, sd_b, xbc, p, p,
      tile4(hg_norm_w), tile4(gla_norm_w), jnp.repeat(ssd_d, SSD_HEADDIM).reshape(1, w), ssd_norm_w.reshape(1, w),
      w_hg, w_gla, w_ssd, w_out, mod2)


MOE_SUB = 4


def _expert_kernel(be_ref, x_ref, *refs):
    o_ref = refs[-1]
    first = pl.program_id(0) * MOE_SUB
    same = be_ref[first + 1] == be_ref[first]
    for s in range(2, MOE_SUB):
        same = jnp.logical_and(same, be_ref[first + s] == be_ref[first])

    def ffn(x, wg_ref, wu_ref, wd_ref):
        a = _silu(_dot(x, wg_ref[...].astype(bf16))) * _dot(x, wu_ref[...].astype(bf16))
        return _dot(a.astype(bf16), wd_ref[...].astype(bf16)).astype(o_ref.dtype)

    @pl.when(same)
    def _():
        o_ref[...] = ffn(x_ref[...], *refs[0:3])

    @pl.when(jnp.logical_not(same))
    def _():
        for s in range(MOE_SUB):
            rows = slice(s * MOE_BLOCK, (s + 1) * MOE_BLOCK)
            o_ref[rows, :] = ffn(x_ref[rows, :], *refs[3 * s:3 * s + 3])


def _experts(xg, block_e, w_gate, w_up, w_down):
    npad, d = xg.shape
    ff = w_gate.shape[2]
    rows = MOE_SUB * MOE_BLOCK
    w_specs = []
    for s in range(MOE_SUB):
        pick = lambda i, be, s=s: (be[i * MOE_SUB + s], 0, 0)
        w_specs += [pl.BlockSpec((None, d, ff), pick), pl.BlockSpec((None, d, ff), pick),
                    pl.BlockSpec((None, ff, d), pick)]
    grid_spec = pltpu.PrefetchScalarGridSpec(
        num_scalar_prefetch=1, grid=(npad // rows,),
        in_specs=[pl.BlockSpec((rows, d), lambda i, be: (i, 0))] + w_specs,
        out_specs=pl.BlockSpec((rows, d), lambda i, be: (i, 0)))
    return pl.pallas_call(
        _expert_kernel, grid_spec=grid_spec,
        out_shape=jax.ShapeDtypeStruct((npad, d), bf16),
        compiler_params=_cparams(("arbitrary",)), name="experts",
    )(block_e, xg, *([w_gate, w_up, w_down] * MOE_SUB))


def _slots_kernel(ps_ref, e_ref, r_ref, o_ref):
    e = e_ref[...]
    acc = r_ref[...]
    for j in range(N_EXPERTS):
        acc = acc + jnp.where(e == j, ps_ref[j], 0)
    o_ref[...] = acc


def _slots(experts, rank, pad_start, tl):
    k, t = experts.shape
    spec = pl.BlockSpec((k, tl), lambda i, ps: (0, i))
    return pl.pallas_call(
        _slots_kernel,
        grid_spec=pltpu.PrefetchScalarGridSpec(num_scalar_prefetch=1, grid=(t // tl,),
                                               in_specs=[spec, spec], out_specs=spec),
        out_shape=jax.ShapeDtypeStruct((k, t), jnp.int32),
        compiler_params=_cparams(("parallel",)), name="slots",
    )(pad_start, experts, rank)


def _ffn_out_kernel(x_ref, h_ref, y_ref, wr_ref, sg_ref, su_ref, sd_ref, m5_ref, o_ref):
    h = h_ref[...]
    a = _silu(_dot(h, sg_ref[...])) * _dot(h, su_ref[...])
    acc = _dot(a.astype(bf16), sd_ref[...])
    wr = wr_ref[...]
    for k in range(TOP_K):
        acc = acc + y_ref[k].astype(f32) * wr[:, k:k + 1]
    o_ref[...] = x_ref[...] + m5_ref[...] * acc


def _ffn_out(xs, h, yk, wrow, s_gate, s_up, s_down, mod5, geo, tm):
    t, d = xs.shape
    ff = s_gate.shape[1]
    rows = pl.BlockSpec((tm, d), lambda i: (i, 0))
    return pl.pallas_call(
        _ffn_out_kernel, grid=(t // tm,),
        in_specs=[rows, rows, pl.BlockSpec((TOP_K, tm, d), lambda i: (0, i, 0)),
                  pl.BlockSpec((tm, LANES), lambda i: (i, 0)),
                  pl.BlockSpec((d, ff), lambda i: (0, 0)), pl.BlockSpec((d, ff), lambda i: (0, 0)),
                  pl.BlockSpec((ff, d), lambda i: (0, 0)), _mod_block(d, tm, geo)],
        out_specs=rows,
        out_shape=jax.ShapeDtypeStruct((t, d), f32),
        compiler_params=_cparams(("parallel",)), name="ffn_out",
    )(xs, h, yk, wrow, s_gate, s_up, s_down, mod5)


def _dispatch(experts, rank, counts, tl):
    t = experts.shape[1]
    n_assign = t * TOP_K
    counts = counts[:, 0].astype(jnp.int32)
    padded = (counts + MOE_BLOCK - 1) // MOE_BLOCK * MOE_BLOCK
    pad_end = jnp.cumsum(padded)
    pad_start = (pad_end - padded).astype(jnp.int32)
    slot = _slots(experts, rank, pad_start, tl)
    n_blocks = (n_assign + MOE_BLOCK - 1) // MOE_BLOCK + N_EXPERTS
    n_blocks = (n_blocks + MOE_SUB - 1) // MOE_SUB * MOE_SUB
    token = jnp.broadcast_to(jnp.arange(t, dtype=jnp.int32), (TOP_K, t))
    buf_t = jnp.zeros((n_blocks * MOE_BLOCK,), jnp.int32).at[slot.reshape(-1)].add(
        token.reshape(-1), unique_indices=True)
    first_row = jnp.arange(n_blocks, dtype=jnp.int32) * MOE_BLOCK
    block_e = jnp.minimum(jnp.sum(pad_end[None, :] <= first_row[:, None], axis=1), N_EXPERTS - 1)
    return slot, buf_t, block_e.astype(jnp.int32)


def _final_norm_kernel(x_ref, w_ref, o_ref):
    o_ref[...] = _norm_rows(x_ref[...], w_ref[...])


def _final_norm(xs, w, nctx, tm):
    t, d = xs.shape
    off = nctx // tm
    return pl.pallas_call(
        _final_norm_kernel, grid=((t - nctx) // tm,),
        in_specs=[pl.BlockSpec((tm, d), lambda i: (i + off, 0)), pl.BlockSpec((1, d), lambda i: (0, 0))],
        out_specs=pl.BlockSpec((tm, d), lambda i: (i, 0)),
        out_shape=jax.ShapeDtypeStruct((t - nctx, d), f32),
        compiler_params=_cparams(("parallel",)), name="final_norm",
    )(xs, w.reshape(1, d))


def _grid_sincos(rows, cols, dim):
    quarter = dim // 4
    omega = 1.0 / (POS_BASE ** (jnp.arange(quarter, dtype=f32) / quarter))

    def axis_embed(n):
        ang = jnp.arange(n, dtype=f32)[:, None] * omega
        return jnp.concatenate([jnp.sin(ang), jnp.cos(ang)], axis=-1)

    er, ec = axis_embed(rows), axis_embed(cols)
    pe = jnp.concatenate([jnp.broadcast_to(er[:, None], (rows, cols, dim // 2)),
                          jnp.broadcast_to(ec[None], (rows, cols, dim // 2))], axis=-1)
    return pe.reshape(rows * cols, dim)


def _lower_bounds(logits):
    p = jax.nn.softmax(logits.astype(f32), axis=1)
    return jnp.cumsum(p, axis=1) - p[:, :1]


def _split_w_in(w_in):
    a0 = 5 * HG_WIDTH + 2 * GLA_HEADS * GLA_DK + 2 * GLA_HEADS * GLA_DV
    a1 = a0 + 2 * GLA_RANK
    b0 = a1 + 2 * SSD_INNER + 2 * SSD_GROUPS * SSD_STATE
    b1 = b0 + 2 * SSD_HEADS
    main = jnp.concatenate([w_in[..., b1:], w_in[..., :a0], w_in[..., a1:b0]], axis=-1).astype(bf16)
    small = jnp.concatenate([w_in[..., a0:a1], w_in[..., b0:b1]], axis=-1)
    small = jnp.pad(small, ((0, 0), (0, 0), (0, LANES - small.shape[-1]))).astype(bf16)
    return main, small


def kernel(x, c, ctx, c_ctx, norm_mix_w, norm_ffn_w, final_norm_w, ada_w, ada_b, w_in,
           hg_lb_logits, hg_norm_w, gla_gate_w, gla_gate_b, gla_norm_w, ssd_conv_w, ssd_conv_b,
           ssd_a_log, ssd_dt_bias, ssd_d, ssd_norm_w, w_br_hg, w_br_gla, w_br_ssd, w_out,
           router_w, router_bias, exp_w_gate, exp_w_up, exp_w_down, sh_w_gate, sh_w_up, sh_w_down):
    nb_all, seq, d = x.shape
    lc = ctx.shape[1]
    depth = w_in.shape[0]
    tm = min(512, lc)
    tr = min(256, lc)
    r_scan = min(256, lc)
    c_hg = min(64, lc)
    c_gla = min(128, lc)
    c_ssd = min(128, lc)

    n_streams = 2 if nb_all % 2 == 0 else 1
    nb = nb_all // n_streams
    nctx = nb * lc
    geo = (nctx, seq, nb)

    pe = _grid_sincos(seq // GRID_W, GRID_W, d)
    lb = _lower_bounds(hg_lb_logits)
    w_main, w_small = _split_w_in(w_in)
    w_hg, w_gla, w_ssd, w_o = (w.astype(bf16) for w in (w_br_hg, w_br_gla, w_br_ssd, w_out))
    s_gate, s_up, s_down = (w.astype(bf16) for w in (sh_w_gate, sh_w_up, sh_w_down))

    mods, xs = [], []
    for s in range(n_streams):
        bsl = slice(s * nb, (s + 1) * nb)
        c_all = jnp.zeros((SUBLANES, d), f32).at[:nb].set(c[bsl]).at[nb].set(c_ctx)
        m = _mods(c_all, ada_w, ada_b)
        mods.append(m.reshape(depth, SUBLANES, N_MOD, 1, d).transpose(0, 2, 1, 3, 4))
        xs.append(_embed(ctx[bsl].reshape(nctx, d), x[bsl].reshape(nb * seq, d), pe, tm))

    def mix_and_route(s, l):
        m = mods[s][l]
        p, psmall, psmall_t = _in_proj(xs[s], norm_mix_w[l], m[0], m[1], w_main[l], w_small[l], geo, tm)
        hg_f, hg_b = _hgrn2_scan(p, lb[0, l], lb[1, l], geo, c_hg, r_scan)
        gl_f, gl_b = _gla_scan(p, psmall, gla_gate_w[:, l], gla_gate_b[:, l], geo, c_gla, r_scan)
        xbc = _ssd_conv(p, ssd_conv_w[l], ssd_conv_b[l], geo, tr)
        sd_f, sd_b = _ssd_scan(xbc, psmall, psmall_t, ssd_a_log[:, l], ssd_dt_bias[:, l], geo, c_ssd, r_scan)
        xs[s] = _merge(xs[s], p, hg_f, hg_b, gl_f, gl_b, sd_f, sd_b, xbc, hg_norm_w[l], gla_norm_w[l],
                       ssd_d[l], ssd_norm_w[l], w_hg[l], w_gla[l], w_ssd[l], w_o[l], m[2], geo, tr)
        h2, experts, _, rank, wrow, counts = _router(xs[s], norm_ffn_w[l], m[3], m[4], router_w[l],
                                                     router_bias[l], geo, tm)
        slot, buf_t, block_e = _dispatch(experts, rank, counts, tm)
        xg = h2.at[buf_t].get(mode="promise_in_bounds")
        return h2, xg, block_e, slot, wrow

    def expert_ffn(state, l):
        h2, xg, block_e, slot, wrow = state
        yg = _experts(xg, block_e, exp_w_gate[l], exp_w_up[l], exp_w_down[l])
        yk = yg.at[slot.reshape(-1)].get(mode="promise_in_bounds").reshape(TOP_K, -1, d)
        return h2, yk, wrow

    def ffn_out(s, state, l):
        h2, yk, wrow = state
        xs[s] = _ffn_out(xs[s], h2, yk, wrow, s_gate[l], s_up[l], s_down[l], mods[s][l][5], geo, tr)

    deferred = None
    for l in range(depth):
        routed = []
        for s in range(n_streams):
            routed.append(mix_and_route(s, l))
            if s == 0 and deferred is not None:
                ffn_out(n_streams - 1, *deferred)
                deferred = None
        combined = [expert_ffn(routed[s], l) for s in range(n_streams)]
        for s in range(n_streams - 1):
            ffn_out(s, combined[s], l)
        if n_streams > 1:
            deferred = (combined[-1], l)
        else:
            ffn_out(0, combined[0], l)
    if deferred is not None:
        ffn_out(n_streams - 1, *deferred)

    out = [_final_norm(xs[s], final_norm_w, nctx, tm).reshape(nb, seq, d) for s in range(n_streams)]
    return jnp.concatenate(out, axis=0)
```

```python
import functools
import math

import numpy as np
import jax
import jax.numpy as jnp
from jax import lax
from jax.experimental import pallas as pl
from jax.experimental.pallas import tpu as pltpu

f32 = jnp.float32
bf16 = jnp.bfloat16
HIGHEST = lax.Precision.HIGHEST

EPS = 1e-6
POS_BASE = 10000.0
GRID_W = 64
N_MOD = 6
HG_HEADS = 4
HG_DK = 128
HG_WIDTH = HG_HEADS * HG_DK
GLA_HEADS = 4
GLA_DK = 64
GLA_DV = 128
GLA_RANK = 16
GLA_TAU = 16.0
SSD_HEADS = 8
SSD_HEADDIM = 64
SSD_INNER = SSD_HEADS * SSD_HEADDIM
SSD_GROUPS = 2
SSD_STATE = 128
SSD_CONV = 5
N_BRANCH = 3
N_EXPERTS = 64
TOP_K = 8
N_GROUPS = 8
TOPK_GROUPS = 4
ROUTE_SCALE = 2.5
MOE_BLOCK = 256
NEG_MASK = -1e4

LANES = 128
SUBLANES = 8
VMEM_BYTES = 64 * 1024 * 1024
VMEM_LIMIT = VMEM_BYTES * 7 // 8
EXP_CLAMP = 80.0

P_GATE = 0
P_HG = 3072
P_GLA = 5632
P_SSD = 7168
P_TOTAL = 8704
S_LRF, S_LRB, S_DTF, S_DTB = 0, 16, 32, 40


def _cparams(sem):
    return pltpu.CompilerParams(dimension_semantics=sem, vmem_limit_bytes=VMEM_LIMIT)


def _split_hi_lo(x):
    hi = x.astype(bf16)
    lo = (x - hi.astype(f32)).astype(bf16)
    return hi, lo


def _dot(a, b):
    return jnp.dot(a, b, preferred_element_type=f32)


def _dot_nt(a, b):
    return lax.dot_general(a, b, (((1,), (1,)), ((), ())), preferred_element_type=f32)


def _dot_tn(a, b):
    return lax.dot_general(a, b, (((0,), (0,)), ((), ())), preferred_element_type=f32)


def _dot2(m, x):
    hi, lo = _split_hi_lo(x)
    return _dot(m, hi) + _dot(m, lo)


def _dot2_l(x, m):
    hi, lo = _split_hi_lo(x)
    return _dot(hi, m) + _dot(lo, m)


def _silu(x):
    return x * jax.nn.sigmoid(x)


def _log_sigmoid(z):
    return jnp.minimum(z, 0.0) - jnp.log(1.0 + jnp.exp(-jnp.abs(z)))


def _softplus(z):
    return jnp.maximum(z, 0.0) + jnp.log(1.0 + jnp.exp(-jnp.abs(z)))


def _scan_mask(c, rev):
    r = lax.broadcasted_iota(jnp.int32, (c, c), 0)
    s = lax.broadcasted_iota(jnp.int32, (c, c), 1)
    return (r <= s) if rev else (r >= s)


def _mods_kernel(c_ref, w_ref, b_ref, o_ref):
    o_ref[...] = jnp.dot(_silu(c_ref[...]), w_ref[...], precision=HIGHEST,
                         preferred_element_type=f32) + b_ref[...]


def _mods(c_all, ada_w, ada_b):
    depth, d, n = ada_w.shape
    tn = 1024
    return pl.pallas_call(
        _mods_kernel,
        grid=(depth, n // tn),
        in_specs=[pl.BlockSpec((SUBLANES, d), lambda l, j: (0, 0)),
                  pl.BlockSpec((None, d, tn), lambda l, j: (l, 0, j)),
                  pl.BlockSpec((None, 1, tn), lambda l, j: (l, 0, j))],
        out_specs=pl.BlockSpec((None, SUBLANES, tn), lambda l, j: (l, 0, j)),
        out_shape=jax.ShapeDtypeStruct((depth, SUBLANES, n), f32),
        compiler_params=_cparams(("parallel", "parallel")),
        name="mods",
    )(c_all, ada_w, ada_b.reshape(depth, 1, n))


def _embed_kernel(ctx_ref, x_ref, pe_ref, o_ref, *, nct):
    i = pl.program_id(0)

    @pl.when(i < nct)
    def _():
        o_ref[...] = ctx_ref[...]

    @pl.when(i >= nct)
    def _():
        o_ref[...] = x_ref[...] + pe_ref[...]


def _embed(ctx2, x2, pe, te):
    nctx, d = ctx2.shape
    nlat = x2.shape[0]
    nct, npe = nctx // te, pe.shape[0] // te
    return pl.pallas_call(
        functools.partial(_embed_kernel, nct=nct),
        grid=((nctx + nlat) // te,),
        in_specs=[pl.BlockSpec((te, d), lambda i: (jnp.minimum(i, nct - 1), 0)),
                  pl.BlockSpec((te, d), lambda i: (jnp.maximum(i - nct, 0), 0)),
                  pl.BlockSpec((te, d), lambda i: (jnp.maximum(i - nct, 0) % npe, 0))],
        out_specs=pl.BlockSpec((te, d), lambda i: (i, 0)),
        out_shape=jax.ShapeDtypeStruct((nctx + nlat, d), f32),
        compiler_params=_cparams(("parallel",)),
        name="embed",
    )(ctx2, x2, pe)


def _norm_rows(x, w):
    return x * lax.rsqrt(jnp.mean(x * x, axis=-1, keepdims=True) + EPS) * w


def _beats(a, b, a_first):
    return (a >= b) if a_first else (a > b)


def _router_kernel(x_ref, w_ref, sh_ref, sc_ref, rwt_ref, rb_ref,
                   h_ref, e_ref, wt_ref, rank_ref, wrow_ref, cnt_ref, *, n_tiles):
    @pl.when(pl.program_id(0) == 0)
    def _():
        cnt_ref[...] = jnp.zeros_like(cnt_ref)

    @pl.when(pl.program_id(0) >= n_tiles)
    def _():
        h_ref[...] = jnp.zeros_like(h_ref)

    @pl.when(pl.program_id(0) < n_tiles)
    def _():
        _route_tile(x_ref, w_ref, sh_ref, sc_ref, rwt_ref, rb_ref,
                    h_ref, e_ref, wt_ref, rank_ref, wrow_ref, cnt_ref)


def _route_tile(x_ref, w_ref, sh_ref, sc_ref, rwt_ref, rb_ref,
                h_ref, e_ref, wt_ref, rank_ref, wrow_ref, cnt_ref):
    y = _norm_rows(x_ref[...], w_ref[...])
    h = y * (1.0 + sc_ref[...]) + sh_ref[...]
    h_ref[...] = h.astype(bf16)
    tm = h.shape[0]
    logits = lax.dot_general(rwt_ref[...], h, (((1,), (1,)), ((), ())), precision=HIGHEST,
                             preferred_element_type=f32)
    scores = jax.nn.sigmoid(logits)
    biased = scores + rb_ref[...]

    gsz = N_EXPERTS // N_GROUPS
    miota = lax.broadcasted_iota(jnp.int32, (gsz, tm), 0)
    blocks, gscore = [], []
    for g in range(N_GROUPS):
        blk = biased[g * gsz:(g + 1) * gsz, :]
        m1 = jnp.max(blk, axis=0, keepdims=True)
        i1 = jnp.min(jnp.where(blk == m1, miota, gsz), axis=0, keepdims=True)
        m2 = jnp.max(jnp.where(miota == i1, -jnp.inf, blk), axis=0, keepdims=True)
        blocks.append(blk)
        gscore.append(m1 + m2)
    vals = []
    for g in range(N_GROUPS):
        ahead = jnp.zeros((1, tm), jnp.int32)
        for o in range(N_GROUPS):
            if o != g:
                ahead = ahead + _beats(gscore[o], gscore[g], o < g).astype(jnp.int32)
        vals.append(jnp.where(ahead < TOPK_GROUPS, blocks[g], NEG_MASK))
    vals = jnp.concatenate(vals, axis=0)

    eiota = lax.broadcasted_iota(jnp.int32, (N_EXPERTS, tm), 0)
    member = jnp.zeros((N_EXPERTS, tm), f32)
    chosen, weights = [], []
    for k in range(TOP_K):
        m = jnp.max(vals, axis=0, keepdims=True)
        ei = jnp.min(jnp.where(vals == m, eiota, N_EXPERTS), axis=0, keepdims=True)
        sel = eiota == ei
        chosen.append(ei)
        weights.append(jnp.sum(jnp.where(sel, scores, 0.0), axis=0, keepdims=True))
        member = member + sel.astype(f32)
        vals = jnp.where(sel, -jnp.inf, vals)
    wsum = weights[0]
    for k in range(1, TOP_K):
        wsum = wsum + weights[k]

    r = lax.broadcasted_iota(jnp.int32, (tm, tm), 0)
    s = lax.broadcasted_iota(jnp.int32, (tm, tm), 1)
    member_bf = member.astype(bf16)
    rank_all = _dot(member_bf, (r < s).astype(bf16)) + cnt_ref[:, 0:1]
    for k in range(TOP_K):
        e_ref[k:k + 1, :] = chosen[k]
        wt_ref[k:k + 1, :] = weights[k] / wsum * ROUTE_SCALE
        rank_ref[k:k + 1, :] = jnp.sum(jnp.where(eiota == chosen[k], rank_all, 0.0), axis=0,
                                       keepdims=True).astype(jnp.int32)
    cnt_ref[...] = cnt_ref[...] + _dot(member_bf, jnp.ones((tm, LANES), bf16))
    eye = (lax.broadcasted_iota(jnp.int32, (TOP_K, LANES), 0)
           == lax.broadcasted_iota(jnp.int32, (TOP_K, LANES), 1)).astype(bf16)
    w_hi, w_lo = _split_hi_lo(wt_ref[...])
    wrow_ref[...] = _dot_tn(w_hi, eye) + _dot_tn(w_lo, eye)


def _mod_block(d, tm, geo):
    nctx, seq, nb = geo
    return pl.BlockSpec((None, 1, d), lambda i: (jnp.where(i * tm < nctx, nb, (i * tm - nctx) // seq), 0, 0))


def _router(xs, w, shift, scale, router_w, router_bias, geo, tm):
    t, d = xs.shape
    ne = router_w.shape[1]
    nt = t // tm
    nt_table = max(nt, -(-VMEM_BYTES // (tm * d * 2)))
    last = nt - 1
    nctx, seq, nb = geo

    def mod(i):
        r0 = jnp.minimum(i, last) * tm
        return (jnp.where(r0 < nctx, nb, (r0 - nctx) // seq), 0, 0)

    kt = pl.BlockSpec((TOP_K, tm), lambda i: (0, jnp.minimum(i, last)))
    return pl.pallas_call(
        functools.partial(_router_kernel, n_tiles=nt), grid=(nt_table,),
        in_specs=[pl.BlockSpec((tm, d), lambda i: (jnp.minimum(i, last), 0)),
                  pl.BlockSpec((1, d), lambda i: (0, 0)),
                  pl.BlockSpec((None, 1, d), mod), pl.BlockSpec((None, 1, d), mod),
                  pl.BlockSpec((ne, d), lambda i: (0, 0)),
                  pl.BlockSpec((ne, 1), lambda i: (0, 0))],
        out_specs=[pl.BlockSpec((tm, d), lambda i: (i, 0)), kt, kt, kt,
                   pl.BlockSpec((tm, LANES), lambda i: (jnp.minimum(i, last), 0)),
                   pl.BlockSpec((ne, LANES), lambda i: (0, 0))],
        out_shape=[jax.ShapeDtypeStruct((nt_table * tm, d), bf16),
                   jax.ShapeDtypeStruct((TOP_K, t), jnp.int32),
                   jax.ShapeDtypeStruct((TOP_K, t), f32),
                   jax.ShapeDtypeStruct((TOP_K, t), jnp.int32),
                   jax.ShapeDtypeStruct((t, LANES), f32),
                   jax.ShapeDtypeStruct((ne, LANES), f32)],
        compiler_params=_cparams(("arbitrary",)), name="router",
    )(xs, w.reshape(1, d), shift, scale, router_w.T, router_bias.astype(f32).reshape(ne, 1))


PROJ_TN = 512


def _in_proj_kernel(x_ref, nw_ref, sh_ref, sc_ref, w_ref, ws_ref, wst_ref, p_ref, ps_ref, pst_ref):
    y = _norm_rows(x_ref[...], nw_ref[...])
    h = (y * (1.0 + sc_ref[...]) + sh_ref[...]).astype(bf16)
    for j in range(p_ref.shape[1] // PROJ_TN):
        sl = slice(j * PROJ_TN, (j + 1) * PROJ_TN)
        p_ref[:, sl] = _dot(h, w_ref[:, sl]).astype(p_ref.dtype)
    ps_ref[...] = _dot(h, ws_ref[...])
    pst_ref[...] = _dot_nt(wst_ref[...], h)


def _in_proj(xs, nw, shift, scale, w_main, w_small, geo, tm):
    t, d = xs.shape
    n = w_main.shape[1]
    ns = w_small.shape[1]

    def resident(shape):
        return pl.BlockSpec(shape, lambda i: (0, 0), pipeline_mode=pl.Buffered(1))

    return pl.pallas_call(
        _in_proj_kernel, grid=(t // tm,),
        in_specs=[pl.BlockSpec((tm, d), lambda i: (i, 0)),
                  pl.BlockSpec((1, d), lambda i: (0, 0)),
                  _mod_block(d, tm, geo), _mod_block(d, tm, geo),
                  resident((d, n)), resident((d, ns)), resident((ns, d))],
        out_specs=[pl.BlockSpec((tm, n), lambda i: (i, 0)),
                   pl.BlockSpec((tm, ns), lambda i: (i, 0)),
                   pl.BlockSpec((ns, tm), lambda i: (0, i))],
        out_shape=[jax.ShapeDtypeStruct((t, n), bf16),
                   jax.ShapeDtypeStruct((t, ns), f32),
                   jax.ShapeDtypeStruct((ns, t), f32)],
        compiler_params=_cparams(("parallel",)), name="in_proj",
        cost_estimate=pl.CostEstimate(flops=2 * t * d * (n + 2 * ns), transcendentals=t,
                                      bytes_accessed=t * (4 * d + 2 * n + 8 * ns) + 2 * d * (n + 2 * ns)),
    )(xs, nw.reshape(1, d), shift, scale, w_main, w_small, w_small.T)


def _chunk_blocks(geo, c, rev):
    nctx, seq, nb = geo
    lc = nctx // nb
    ncc, nlc = lc // c, seq // c

    def blk(b, i):
        j_ctx = (ncc - 1 - i) if rev else i
        j_lat = (nlc - 1 - (i - ncc)) if rev else (i - ncc)
        return jnp.where(i < ncc, b * ncc + j_ctx, nctx // c + b * nlc + j_lat)

    return blk, ncc + nlc


def _sub_chunks(n_rows, c, rev):
    n = n_rows // c
    order = range(n - 1, -1, -1) if rev else range(n)
    return [slice(j * c, (j + 1) * c) for j in order]


def _decay_terms(q, k, g, mask, rev):
    c = q.shape[0]
    b = _dot2(mask.astype(bf16), g)
    last = 0 if rev else c - 1
    b_tot = b[last:last + 1, :]
    b_mid = b[c // 2:c // 2 + 1, :]
    qd = q * jnp.exp(jnp.minimum(b - b_mid, EXP_CLAMP))
    kd = k * jnp.exp(jnp.minimum(b_mid - b, EXP_CLAMP))
    qe = q * jnp.exp(b)
    ke = k * jnp.exp(b_tot - b)
    return qd.astype(bf16), kd.astype(bf16), qe.astype(bf16), ke.astype(bf16), jnp.exp(b_tot)


def _hgrn2_gates(z, lb):
    e = jnp.exp(-jnp.abs(z))
    inv = 1.0 / (1.0 + e)
    sig_pos = jnp.where(z >= 0, 1.0, e) * inv
    sig_neg = jnp.where(z >= 0, e, 1.0) * inv
    ls = jnp.minimum(z, 0.0) - jnp.log(1.0 + e)
    log_f = jnp.where(lb > 0.0, jnp.log(jnp.where(lb > 0.0, lb + (1.0 - lb) * sig_pos, 1.0)), ls)
    return (1.0 - lb) * sig_neg, log_f


def _hgrn2_kernel(qf_ref, zf_ref, vf_ref, qb_ref, zb_ref, vb_ref, lbf_ref, lbb_ref,
                  of_ref, ob_ref, sf_ref, sb_ref, *, c):
    @pl.when(pl.program_id(1) == 0)
    def _():
        sf_ref[...] = jnp.zeros_like(sf_ref)
        sb_ref[...] = jnp.zeros_like(sb_ref)

    for rev, q_ref, z_ref, v_ref, lb_ref, o_ref, s_ref in (
            (False, qf_ref, zf_ref, vf_ref, lbf_ref, of_ref, sf_ref),
            (True, qb_ref, zb_ref, vb_ref, lbb_ref, ob_ref, sb_ref)):
        mask = _scan_mask(c, rev)
        for rows in _sub_chunks(q_ref.shape[0], c, rev):
            q = q_ref[rows, :].astype(f32)
            k, g = _hgrn2_gates(z_ref[rows, :].astype(f32), lb_ref[...])
            v = v_ref[rows, :]
            qd, kd, qe, ke, dtot = _decay_terms(q, k, g, mask, rev)
            for h in range(HG_HEADS):
                sl = slice(h * HG_DK, (h + 1) * HG_DK)
                st = s_ref[h]
                att = jnp.where(mask, _dot_nt(qd[:, sl], kd[:, sl]), 0.0)
                o = _dot(att.astype(bf16), v[:, sl]) + _dot_nt(qe[:, sl], st.astype(bf16))
                o_ref[rows, sl] = o.astype(o_ref.dtype)
                s_ref[h] = st * dtot[:, sl] + _dot_tn(v[:, sl], ke[:, sl])


def _hgrn2_scan(p, lb_f, lb_b, geo, c, r):
    t = p.shape[0]
    nb = geo[2]
    blk_f, steps = _chunk_blocks(geo, r, False)
    blk_b, _ = _chunk_blocks(geo, r, True)
    w = HG_WIDTH
    c0 = P_HG // w

    def spec(blk, col):
        return pl.BlockSpec((r, w), lambda b, i: (blk(b, i), col))

    vec = pl.BlockSpec((1, w), lambda b, i: (0, 0))
    return pl.pallas_call(
        functools.partial(_hgrn2_kernel, c=c), grid=(nb, steps),
        in_specs=[spec(blk_f, c0), spec(blk_f, c0 + 1), spec(blk_f, c0 + 3),
                  spec(blk_b, c0), spec(blk_b, c0 + 2), spec(blk_b, c0 + 3), vec, vec],
        out_specs=[spec(blk_f, 0), spec(blk_b, 0)],
        out_shape=[jax.ShapeDtypeStruct((t, w), bf16)] * 2,
        scratch_shapes=[pltpu.VMEM((HG_HEADS, HG_DK, HG_DK), f32)] * 2,
        compiler_params=_cparams(("parallel", "arbitrary")), name="hgrn2_scan",
    )(p, p, p, p, p, p, lb_f.reshape(1, w), lb_b.reshape(1, w))


def _gla_kernel(qf_ref, kf_ref, vf_ref, sf_in_ref, qb_ref, kb_ref, vb_ref, sb_in_ref,
                gwf_ref, gwb_ref, gbf_ref, gbb_ref, of_ref, ob_ref, sf_ref, sb_ref, *, c):
    @pl.when(pl.program_id(1) == 0)
    def _():
        sf_ref[...] = jnp.zeros_like(sf_ref)
        sb_ref[...] = jnp.zeros_like(sb_ref)

    npair = GLA_HEADS // 2
    lane = lax.broadcasted_iota(jnp.int32, (1, LANES), 1)
    vrow = lax.broadcasted_iota(jnp.int32, (2 * GLA_DV, LANES), 0)
    vcol = lax.broadcasted_iota(jnp.int32, (2 * GLA_DV, LANES), 1)
    block_diag = (vrow // GLA_DV) == (vcol // GLA_DK)
    for rev, q_ref, k_ref, v_ref, sm_ref, gw_ref, gb_ref, o_ref, s_ref in (
            (False, qf_ref, kf_ref, vf_ref, sf_in_ref, gwf_ref, gbf_ref, of_ref, sf_ref),
            (True, qb_ref, kb_ref, vb_ref, sb_in_ref, gwb_ref, gbb_ref, ob_ref, sb_ref)):
        mask = _scan_mask(c, rev)
        for rows in _sub_chunks(q_ref.shape[0], c, rev):
            q = q_ref[rows, :].astype(f32) * (GLA_DK ** -0.5)
            k = k_ref[rows, :].astype(f32)
            v = v_ref[rows, :]
            pre = jnp.dot(sm_ref[rows, :], gw_ref[...], precision=HIGHEST,
                          preferred_element_type=f32) + gb_ref[...]
            g = _log_sigmoid(pre) / GLA_TAU
            qd, kd, qe, ke, dtot = _decay_terms(q, k, g, mask, rev)
            for p in range(npair):
                sl = slice(p * LANES, (p + 1) * LANES)
                vsl = slice(p * 2 * GLA_DV, (p + 1) * 2 * GLA_DV)
                st = s_ref[p]
                o_inter = _dot_nt(qe[:, sl], st.astype(bf16))
                for hh in range(2):
                    head = (lane // GLA_DK) == hh
                    qh = jnp.where(head, qd[:, sl], jnp.zeros_like(qd[:, sl]))
                    att = jnp.where(mask, _dot_nt(qh, kd[:, sl]), 0.0)
                    osl = slice((2 * p + hh) * GLA_DV, (2 * p + hh + 1) * GLA_DV)
                    o = _dot(att.astype(bf16), v[:, osl]) + o_inter[:, hh * GLA_DV:(hh + 1) * GLA_DV]
                    o_ref[rows, osl] = o.astype(o_ref.dtype)
                upd = st * dtot[:, sl] + _dot_tn(v[:, vsl], ke[:, sl])
                s_ref[p] = jnp.where(block_diag, upd, 0.0)


def _gla_scan(p, psmall, gate_w, gate_b, geo, c, r):
    t = p.shape[0]
    nb = geo[2]
    blk_f, steps = _chunk_blocks(geo, r, False)
    blk_b, _ = _chunk_blocks(geo, r, True)
    wk, wv = GLA_HEADS * GLA_DK, GLA_HEADS * GLA_DV

    def spec(blk, width, col):
        return pl.BlockSpec((r, width), lambda b, i: (blk(b, i), col))

    def const(shape):
        return pl.BlockSpec(shape, lambda b, i: (0, 0))

    gwf = jnp.zeros((LANES, wk), f32).at[S_LRF:S_LRF + GLA_RANK].set(gate_w[0])
    gwb = jnp.zeros((LANES, wk), f32).at[S_LRB:S_LRB + GLA_RANK].set(gate_w[1])
    qcol, kcol, vcol = P_GLA // wk, P_GLA // wk + 1, (P_GLA + 2 * wk) // wv
    return pl.pallas_call(
        functools.partial(_gla_kernel, c=c), grid=(nb, steps),
        in_specs=[spec(blk_f, wk, qcol), spec(blk_f, wk, kcol), spec(blk_f, wv, vcol), spec(blk_f, LANES, 0),
                  spec(blk_b, wk, qcol), spec(blk_b, wk, kcol), spec(blk_b, wv, vcol), spec(blk_b, LANES, 0),
                  const((LANES, wk)), const((LANES, wk)), const((1, wk)), const((1, wk))],
        out_specs=[spec(blk_f, wv, 0), spec(blk_b, wv, 0)],
        out_shape=[jax.ShapeDtypeStruct((t, wv), bf16)] * 2,
        scratch_shapes=[pltpu.VMEM((GLA_HEADS // 2, 2 * GLA_DV, 2 * GLA_DK), f32)] * 2,
        compiler_params=_cparams(("parallel", "arbitrary")), name="gla_scan",
    )(p, p, p, psmall, p, p, p, psmall, gwf, gwb, gate_b[0].reshape(1, wk), gate_b[1].reshape(1, wk))


HALO = SUBLANES


def _conv_kernel(prev_ref, cur_ref, next_ref, w_ref, b_ref, o_ref, ext_ref, *, geo, tc):
    nctx, seq, nb = geo
    r0 = pl.program_id(0) * tc
    in_ctx = r0 < nctx
    seg_len = jnp.where(in_ctx, nctx // nb, seq)
    pos = jnp.where(in_ctx, r0 % (nctx // nb), (r0 - nctx) % seq)
    first = pos == 0
    last = pos + tc == seg_len
    ext_ref[0:HALO, :] = jnp.where(first, 0.0, prev_ref[...].astype(f32))
    ext_ref[HALO:HALO + tc, :] = cur_ref[...].astype(f32)
    ext_ref[HALO + tc:HALO + tc + HALO, :] = jnp.where(last, 0.0, next_ref[...].astype(f32))
    acc = jnp.zeros(o_ref.shape, f32) + b_ref[...]
    for j in range(SSD_CONV):
        off = HALO - SSD_CONV // 2 + j
        acc = acc + ext_ref[off:off + tc, :] * w_ref[j:j + 1, :]
    o_ref[...] = _silu(acc).astype(o_ref.dtype)


def _ssd_conv(p, conv_w, conv_b, geo, tc):
    t = p.shape[0]
    nch = conv_w.shape[1]
    tw = 512
    col0 = (P_SSD + SSD_INNER) // tw
    hb = tc // HALO
    nhb = t // HALO
    return pl.pallas_call(
        functools.partial(_conv_kernel, geo=geo, tc=tc),
        grid=(t // tc, nch // tw),
        in_specs=[pl.BlockSpec((HALO, tw), lambda i, j: (jnp.maximum(i * hb - 1, 0), col0 + j)),
                  pl.BlockSpec((tc, tw), lambda i, j: (i, col0 + j)),
                  pl.BlockSpec((HALO, tw), lambda i, j: (jnp.minimum((i + 1) * hb, nhb - 1), col0 + j)),
                  pl.BlockSpec((SSD_CONV, tw), lambda i, j: (0, j)),
                  pl.BlockSpec((1, tw), lambda i, j: (0, j))],
        out_specs=pl.BlockSpec((tc, tw), lambda i, j: (i, j)),
        out_shape=jax.ShapeDtypeStruct((t, nch), bf16),
        scratch_shapes=[pltpu.VMEM((tc + 2 * HALO, tw), f32)],
        compiler_params=_cparams(("parallel", "parallel")), name="ssd_conv",
    )(p, p, p, conv_w, conv_b.reshape(1, nch))


def _ssd_kernel(xf_ref, bf_ref, cf_ref, smf_ref, smtf_ref, xb_ref, bb_ref, cb_ref, smb_ref, smtb_ref,
                arow_ref, brow_ref, acol_ref, bcol_ref, ef_ref, eb_ref,
                yf_ref, yb_ref, sf_ref, sb_ref, *, c):
    @pl.when(pl.program_id(1) == 0)
    def _():
        sf_ref[...] = jnp.zeros_like(sf_ref)
        sb_ref[...] = jnp.zeros_like(sb_ref)

    gw = SSD_INNER // SSD_GROUPS
    lane = lax.broadcasted_iota(jnp.int32, (1, LANES), 1)
    for d, rev, x_ref, bm_ref, cm_ref, sm_ref, smt_ref, e_ref, y_ref, s_ref, lane0 in (
            (0, False, xf_ref, bf_ref, cf_ref, smf_ref, smtf_ref, ef_ref, yf_ref, sf_ref, S_DTF),
            (1, True, xb_ref, bb_ref, cb_ref, smb_ref, smtb_ref, eb_ref, yb_ref, sb_ref, S_DTB)):
        mask = _scan_mask(c, rev)
        mask_bf = mask.astype(bf16)
        mask_t_bf = _scan_mask(c, not rev).astype(bf16)
        expand = e_ref[...]
        for rows in _sub_chunks(x_ref.shape[0], c, rev):
            dt = _softplus(sm_ref[rows, :] + brow_ref[d:d + 1, :])
            cum = _dot2(mask_bf, dt * arow_ref[d:d + 1, :])
            last = 0 if rev else c - 1
            cum_end = cum[last:last + 1, :]
            dt_t = _softplus(smt_ref[:, rows] + bcol_ref[d])
            cum_t = _dot2_l(dt_t * acol_ref[d], mask_t_bf)

            x = x_ref[rows, :].astype(f32)
            xdt = x * _dot2_l(dt, expand)
            w = (xdt * _dot2_l(jnp.exp(cum_end - cum), expand)).astype(bf16)
            xdt = xdt.astype(bf16)
            dec_t = _dot2_l(jnp.exp(cum), expand)
            dec_end = _dot2_l(jnp.broadcast_to(jnp.exp(cum_end), (SUBLANES, LANES)), expand)[0:1, :]
            bm = bm_ref[rows, :]
            cm = cm_ref[rows, :]
            for g in range(SSD_GROUPS):
                gsl = slice(g * SSD_STATE, (g + 1) * SSD_STATE)
                hsl = slice(g * gw, (g + 1) * gw)
                st = s_ref[:, hsl]
                scores = _dot_nt(cm[:, gsl], bm[:, gsl])
                y_inter = _dot(cm[:, gsl], st.astype(bf16)) * dec_t[:, hsl]
                for tile in range(gw // LANES):
                    tsl = slice(g * gw + tile * LANES, g * gw + (tile + 1) * LANES)
                    xt = xdt[:, tsl]
                    acc = y_inter[:, tile * LANES:(tile + 1) * LANES]
                    for hh in range(LANES // SSD_HEADDIM):
                        h = (g * gw + tile * LANES) // SSD_HEADDIM + hh
                        diff = cum[:, lane0 + h:lane0 + h + 1] - cum_t[h:h + 1, :]
                        seg = jnp.where(mask, jnp.exp(jnp.minimum(diff, 0.0)), 0.0)
                        xh = jnp.where((lane // SSD_HEADDIM) == hh, xt, jnp.zeros_like(xt))
                        acc = acc + _dot((scores * seg).astype(bf16), xh)
                    y_ref[rows, tsl] = acc.astype(y_ref.dtype)
                s_ref[:, hsl] = st * dec_end[:, hsl] + _dot_tn(bm[:, gsl], w[:, hsl])


def _ssd_scan(xbc, psmall, psmall_t, a_log, dt_bias, geo, c, r):
    t = xbc.shape[0]
    nb = geo[2]
    blk_f, steps = _chunk_blocks(geo, r, False)
    blk_b, _ = _chunk_blocks(geo, r, True)
    wi, wn = SSD_INNER, SSD_GROUPS * SSD_STATE

    def spec(blk, width, col):
        return pl.BlockSpec((r, width), lambda b, i: (blk(b, i), col))

    def spec_t(blk, row):
        return pl.BlockSpec((SUBLANES, r), lambda b, i: (row, blk(b, i)))

    def const(shape):
        return pl.BlockSpec(shape, lambda b, i: (0,) * len(shape))

    a = -jnp.exp(a_log.astype(f32))
    arow = jnp.zeros((2, LANES), f32).at[0, S_DTF:S_DTF + SSD_HEADS].set(a[0]).at[1, S_DTB:S_DTB + SSD_HEADS].set(a[1])
    brow = jnp.zeros((2, LANES), f32).at[0, S_DTF:S_DTF + SSD_HEADS].set(dt_bias[0]).at[1, S_DTB:S_DTB + SSD_HEADS].set(dt_bias[1])
    head_of_lane = np.arange(wi) // SSD_HEADDIM
    ef = jnp.asarray((np.arange(LANES)[:, None] == S_DTF + head_of_lane[None, :]), bf16)
    eb = jnp.asarray((np.arange(LANES)[:, None] == S_DTB + head_of_lane[None, :]), bf16)
    xcol, bcol, ccol = 0, wi // wn, wi // wn + 1
    return pl.pallas_call(
        functools.partial(_ssd_kernel, c=c), grid=(nb, steps),
        in_specs=[spec(blk_f, wi, xcol), spec(blk_f, wn, bcol), spec(blk_f, wn, ccol), spec(blk_f, LANES, 0),
                  spec_t(blk_f, S_DTF // SUBLANES),
                  spec(blk_b, wi, xcol), spec(blk_b, wn, bcol), spec(blk_b, wn, ccol), spec(blk_b, LANES, 0),
                  spec_t(blk_b, S_DTB // SUBLANES),
                  const((2, LANES)), const((2, LANES)), const((2, SSD_HEADS, 1)), const((2, SSD_HEADS, 1)),
                  const((LANES, wi)), const((LANES, wi))],
        out_specs=[spec(blk_f, wi, 0), spec(blk_b, wi, 0)],
        out_shape=[jax.ShapeDtypeStruct((t, wi), bf16)] * 2,
        scratch_shapes=[pltpu.VMEM((SSD_STATE, wi), f32)] * 2,
        compiler_params=_cparams(("parallel", "arbitrary")), name="ssd_scan",
    )(xbc, xbc, xbc, psmall, psmall_t, xbc, xbc, xbc, psmall, psmall_t,
      arow, brow, a.reshape(2, SSD_HEADS, 1), dt_bias.astype(f32).reshape(2, SSD_HEADS, 1), ef, eb)


def _group_norm(o, w, width):
    parts = []
    for j in range(o.shape[1] // width):
        blk = o[:, j * width:(j + 1) * width]
        parts.append(blk * lax.rsqrt(jnp.mean(blk * blk, axis=-1, keepdims=True) + EPS))
    return jnp.concatenate(parts, axis=1) * w


def _merge_kernel(x_ref, hgf_ref, hgb_ref, hgg_ref, glf_ref, glb_ref, glg_ref,
                  sdf_ref, sdb_ref, sdx_ref, sdz_ref, gate_ref,
                  hgw_ref, glw_ref, sdd_ref, sdw_ref, wh_ref, wg_ref, ws_ref, wo_ref, m2_ref, o_ref):
    o_hg = _group_norm(hgf_ref[...].astype(f32) + hgb_ref[...].astype(f32), hgw_ref[...], HG_DK)
    o_hg = o_hg * _silu(hgg_ref[...].astype(f32))
    o_gl = _group_norm(glf_ref[...].astype(f32) + glb_ref[...].astype(f32), glw_ref[...], GLA_DV)
    o_gl = o_gl * _silu(glg_ref[...].astype(f32))
    y = sdf_ref[...].astype(f32) + sdb_ref[...].astype(f32) + sdd_ref[...] * sdx_ref[...].astype(f32)
    o_sd = _group_norm(y * _silu(sdz_ref[...].astype(f32)), sdw_ref[...], SSD_INNER // SSD_GROUPS)
    d = x_ref.shape[1]
    gate = jax.nn.sigmoid(gate_ref[...].astype(f32))
    y = (gate[:, 0:d] * _dot(o_hg.astype(bf16), wh_ref[...])
         + gate[:, d:2 * d] * _dot(o_gl.astype(bf16), wg_ref[...])
         + gate[:, 2 * d:3 * d] * _dot(o_sd.astype(bf16), ws_ref[...]))
    o_ref[...] = x_ref[...] + m2_ref[...] * _dot(y.astype(bf16), wo_ref[...])


def _merge(xs, p, hg_f, hg_b, gl_f, gl_b, sd_f, sd_b, xbc, hg_norm_w, gla_norm_w, ssd_d, ssd_norm_w,
           w_hg, w_gla, w_ssd, w_out, mod2, geo, tm):
    t, d = xs.shape
    w = 512

    def rows(width, col):
        return pl.BlockSpec((tm, width), lambda i: (i, col))

    def const(shape):
        return pl.BlockSpec(shape, lambda i: (0, 0))

    tile4 = lambda v: jnp.tile(v, w // v.shape[0]).reshape(1, w)
    return pl.pallas_call(
        _merge_kernel, grid=(t // tm,),
        in_specs=[rows(d, 0),
                  rows(w, 0), rows(w, 0), rows(w, (P_HG + 4 * w) // w),
                  rows(w, 0), rows(w, 0), rows(w, (P_GLA + 2 * w) // w),
                  rows(w, 0), rows(w, 0), rows(w, 0), rows(w, P_SSD // w),
                  rows(3 * d, P_GATE // (3 * d)),
                  const((1, w)), const((1, w)), const((1, w)), const((1, w)),
                  const((w, d)), const((w, d)), const((w, d)), const((d, d)),
                  _mod_block(d, tm, geo)],
        out_specs=rows(d, 0),
        out_shape=jax.ShapeDtypeStruct((t, d), f32),
        compiler_params=_cparams(("parallel",)), name="merge",
        cost_estimate=pl.CostEstimate(flops=2 * t * d * (3 * w + d), transcendentals=t * (3 * w + 3 * d),
                                      bytes_accessed=t * (8 * d + 2 * 11 * w + 6 * d) + 2 * d * (3 * w + d)),
    )(xs, hg_f, hg_b, p, gl_f, gl_b, p, sd_f, sd_b, xbc, p, p,
      tile4(hg_norm_w), tile4(gla_norm_w), jnp.repeat(ssd_d, SSD_HEADDIM).reshape(1, w), ssd_norm_w.reshape(1, w),
      w_hg, w_gla, w_ssd, w_out, mod2)


MOE_SUB = 4


def _expert_kernel(be_ref, x_ref, *refs):
    o_ref, tok_ref = refs[-2:]
    first = pl.program_id(0) * MOE_SUB
    same = be_ref[first + 1] == be_ref[first]
    for s in range(2, MOE_SUB):
        same = jnp.logical_and(same, be_ref[first + s] == be_ref[first])

    def ffn(x, wg_ref, wu_ref, wd_ref):
        a = _silu(_dot(x, wg_ref[...])) * _dot(x, wu_ref[...])
        return _dot(a.astype(bf16), wd_ref[...]).astype(o_ref.dtype)

    tok_ref[...] = jnp.zeros_like(tok_ref)

    @pl.when(same)
    def _():
        o_ref[...] = ffn(x_ref[...], *refs[0:3])

    @pl.when(jnp.logical_not(same))
    def _():
        for s in range(MOE_SUB):
            rows = slice(s * MOE_BLOCK, (s + 1) * MOE_BLOCK)
            o_ref[rows, :] = ffn(x_ref[rows, :], *refs[3 * s:3 * s + 3])


def _experts(xg, block_e, w_gate, w_up, w_down):
    npad, d = xg.shape
    ff = w_gate.shape[2]
    rows = MOE_SUB * MOE_BLOCK
    w_specs = []
    for s in range(MOE_SUB):
        pick = lambda i, be, s=s: (be[i * MOE_SUB + s], 0, 0)
        w_specs += [pl.BlockSpec((None, d, ff), pick), pl.BlockSpec((None, d, ff), pick),
                    pl.BlockSpec((None, ff, d), pick)]
    grid_spec = pltpu.PrefetchScalarGridSpec(
        num_scalar_prefetch=1, grid=(npad // rows,),
        in_specs=[pl.BlockSpec((rows, d), lambda i, be: (i, 0))] + w_specs,
        out_specs=[pl.BlockSpec((rows, d), lambda i, be: (i, 0)),
                   pl.BlockSpec((SUBLANES, LANES), lambda i, be: (0, 0))])
    return pl.pallas_call(
        _expert_kernel, grid_spec=grid_spec,
        out_shape=[jax.ShapeDtypeStruct((npad, d), bf16),
                   jax.ShapeDtypeStruct((SUBLANES, LANES), jnp.int32)],
        compiler_params=_cparams(("arbitrary",)), name="experts",
        cost_estimate=pl.CostEstimate(flops=6 * npad * d * ff, transcendentals=npad * ff,
                                      bytes_accessed=4 * npad * d + 6 * N_EXPERTS * d * ff),
    )(block_e, xg, *([w_gate, w_up, w_down] * MOE_SUB))


def _slots_kernel(ps_ref, e_ref, r_ref, o_ref):
    e = e_ref[...]
    acc = r_ref[...]
    for j in range(N_EXPERTS):
        acc = acc + jnp.where(e == j, ps_ref[j], 0)
    o_ref[...] = acc


def _slots(experts, rank, pad_start, tl):
    k, t = experts.shape
    spec = pl.BlockSpec((k, tl), lambda i, ps: (0, i))
    return pl.pallas_call(
        _slots_kernel,
        grid_spec=pltpu.PrefetchScalarGridSpec(num_scalar_prefetch=1, grid=(t // tl,),
                                               in_specs=[spec, spec], out_specs=spec),
        out_shape=jax.ShapeDtypeStruct((k, t), jnp.int32),
        compiler_params=_cparams(("parallel",)), name="slots",
    )(pad_start, experts, rank)


def _ffn_out_kernel(x_ref, h_ref, y_ref, wr_ref, sg_ref, su_ref, sd_ref, m5_ref, o_ref):
    h = h_ref[...]
    a = _silu(_dot(h, sg_ref[...])) * _dot(h, su_ref[...])
    acc = _dot(a.astype(bf16), sd_ref[...])
    wr = wr_ref[...]
    for k in range(TOP_K):
        acc = acc + y_ref[k].astype(f32) * wr[:, k:k + 1]
    o_ref[...] = x_ref[...] + m5_ref[...] * acc


def _ffn_out(xs, h, yk, wrow, s_gate, s_up, s_down, mod5, geo, tm):
    t, d = xs.shape
    ff = s_gate.shape[1]
    rows = pl.BlockSpec((tm, d), lambda i: (i, 0))
    return pl.pallas_call(
        _ffn_out_kernel, grid=(t // tm,),
        in_specs=[rows, rows, pl.BlockSpec((TOP_K, tm, d), lambda i: (0, i, 0)),
                  pl.BlockSpec((tm, LANES), lambda i: (i, 0)),
                  pl.BlockSpec((d, ff), lambda i: (0, 0)), pl.BlockSpec((d, ff), lambda i: (0, 0)),
                  pl.BlockSpec((ff, d), lambda i: (0, 0)), _mod_block(d, tm, geo)],
        out_specs=rows,
        out_shape=jax.ShapeDtypeStruct((t, d), f32),
        compiler_params=_cparams(("parallel",)), name="ffn_out",
    )(xs, h, yk, wrow, s_gate, s_up, s_down, mod5)


def _dispatch(experts, rank, counts, tl):
    t = experts.shape[1]
    n_assign = t * TOP_K
    counts = counts[:, 0].astype(jnp.int32)
    padded = (counts + MOE_BLOCK - 1) // MOE_BLOCK * MOE_BLOCK
    pad_end = jnp.cumsum(padded)
    pad_start = (pad_end - padded).astype(jnp.int32)
    slot = _slots(experts, rank, pad_start, tl)
    n_blocks = (n_assign + MOE_BLOCK - 1) // MOE_BLOCK + N_EXPERTS
    n_blocks = (n_blocks + MOE_SUB - 1) // MOE_SUB * MOE_SUB
    token = jnp.broadcast_to(jnp.arange(t, dtype=jnp.int32), (TOP_K, t)).reshape(-1)
    slot_flat = slot.reshape(-1)
    filler = jnp.arange(n_blocks * MOE_BLOCK, dtype=jnp.int32) % t
    buf_t = filler.at[slot_flat].add(token - slot_flat % t, unique_indices=True)
    first_row = jnp.arange(n_blocks, dtype=jnp.int32) * MOE_BLOCK
    block_e = jnp.minimum(jnp.sum(pad_end[None, :] <= first_row[:, None], axis=1), N_EXPERTS - 1)
    return slot, buf_t, block_e.astype(jnp.int32)


def _final_norm_kernel(x_ref, w_ref, o_ref):
    o_ref[...] = _norm_rows(x_ref[...], w_ref[...])


def _final_norm(xs, w, nctx, tm):
    t, d = xs.shape
    off = nctx // tm
    return pl.pallas_call(
        _final_norm_kernel, grid=((t - nctx) // tm,),
        in_specs=[pl.BlockSpec((tm, d), lambda i: (i + off, 0)), pl.BlockSpec((1, d), lambda i: (0, 0))],
        out_specs=pl.BlockSpec((tm, d), lambda i: (i, 0)),
        out_shape=jax.ShapeDtypeStruct((t - nctx, d), f32),
        compiler_params=_cparams(("parallel",)), name="final_norm",
    )(xs, w.reshape(1, d))


def _grid_sincos(rows, cols, dim):
    quarter = dim // 4
    omega = 1.0 / (POS_BASE ** (jnp.arange(quarter, dtype=f32) / quarter))

    def axis_embed(n):
        ang = jnp.arange(n, dtype=f32)[:, None] * omega
        return jnp.concatenate([jnp.sin(ang), jnp.cos(ang)], axis=-1)

    er, ec = axis_embed(rows), axis_embed(cols)
    pe = jnp.concatenate([jnp.broadcast_to(er[:, None], (rows, cols, dim // 2)),
                          jnp.broadcast_to(ec[None], (rows, cols, dim // 2))], axis=-1)
    return pe.reshape(rows * cols, dim)


def _lower_bounds(logits):
    p = jax.nn.softmax(logits.astype(f32), axis=1)
    return jnp.cumsum(p, axis=1) - p[:, :1]


def _split_w_in(w_in):
    a0 = 5 * HG_WIDTH + 2 * GLA_HEADS * GLA_DK + 2 * GLA_HEADS * GLA_DV
    a1 = a0 + 2 * GLA_RANK
    b0 = a1 + 2 * SSD_INNER + 2 * SSD_GROUPS * SSD_STATE
    b1 = b0 + 2 * SSD_HEADS
    main = jnp.concatenate([w_in[..., b1:], w_in[..., :a0], w_in[..., a1:b0]], axis=-1).astype(bf16)
    small = jnp.concatenate([w_in[..., a0:a1], w_in[..., b0:b1]], axis=-1)
    small = jnp.pad(small, ((0, 0), (0, 0), (0, LANES - small.shape[-1]))).astype(bf16)
    return main, small


def kernel(x, c, ctx, c_ctx, norm_mix_w, norm_ffn_w, final_norm_w, ada_w, ada_b, w_in,
           hg_lb_logits, hg_norm_w, gla_gate_w, gla_gate_b, gla_norm_w, ssd_conv_w, ssd_conv_b,
           ssd_a_log, ssd_dt_bias, ssd_d, ssd_norm_w, w_br_hg, w_br_gla, w_br_ssd, w_out,
           router_w, router_bias, exp_w_gate, exp_w_up, exp_w_down, sh_w_gate, sh_w_up, sh_w_down):
    nb_all, seq, d = x.shape
    lc = ctx.shape[1]
    depth = w_in.shape[0]
    tm = min(512, lc)
    tr = min(256, lc)
    r_scan = min(256, lc)
    c_hg = min(64, lc)
    c_gla = min(128, lc)
    c_ssd = min(128, lc)

    n_streams = 2 if nb_all % 2 == 0 else 1
    nb = nb_all // n_streams
    nctx = nb * lc
    geo = (nctx, seq, nb)

    pe = _grid_sincos(seq // GRID_W, GRID_W, d)
    lb = _lower_bounds(hg_lb_logits)
    w_main, w_small = _split_w_in(w_in)
    w_hg, w_gla, w_ssd, w_o = (w.astype(bf16) for w in (w_br_hg, w_br_gla, w_br_ssd, w_out))
    e_gate, e_up, e_down = (w.astype(bf16) for w in (exp_w_gate, exp_w_up, exp_w_down))
    s_gate, s_up, s_down = (w.astype(bf16) for w in (sh_w_gate, sh_w_up, sh_w_down))

    mods, xs = [], []
    for s in range(n_streams):
        bsl = slice(s * nb, (s + 1) * nb)
        c_all = jnp.zeros((SUBLANES, d), f32).at[:nb].set(c[bsl]).at[nb].set(c_ctx)
        m = _mods(c_all, ada_w, ada_b)
        mods.append(m.reshape(depth, SUBLANES, N_MOD, 1, d).transpose(0, 2, 1, 3, 4))
        xs.append(_embed(ctx[bsl].reshape(nctx, d), x[bsl].reshape(nb * seq, d), pe, tm))

    def mix_and_route(s, l):
        m = mods[s][l]
        p, psmall, psmall_t = _in_proj(xs[s], norm_mix_w[l], m[0], m[1], w_main[l], w_small[l], geo, tm)
        hg_f, hg_b = _hgrn2_scan(p, lb[0, l], lb[1, l], geo, c_hg, r_scan)
        gl_f, gl_b = _gla_scan(p, psmall, gla_gate_w[:, l], gla_gate_b[:, l], geo, c_gla, r_scan)
        xbc = _ssd_conv(p, ssd_conv_w[l], ssd_conv_b[l], geo, tr)
        sd_f, sd_b = _ssd_scan(xbc, psmall, psmall_t, ssd_a_log[:, l], ssd_dt_bias[:, l], geo, c_ssd, r_scan)
        xs[s] = _merge(xs[s], p, hg_f, hg_b, gl_f, gl_b, sd_f, sd_b, xbc, hg_norm_w[l], gla_norm_w[l],
                       ssd_d[l], ssd_norm_w[l], w_hg[l], w_gla[l], w_ssd[l], w_o[l], m[2], geo, tr)
        h2, experts, _, rank, wrow, counts = _router(xs[s], norm_ffn_w[l], m[3], m[4], router_w[l],
                                                     router_bias[l], geo, tm)
        slot, buf_t, block_e = _dispatch(experts, rank, counts, tm)
        xg = h2.at[buf_t].get(mode="promise_in_bounds")
        zero = jnp.minimum(counts[0, 0], 0.0)
        return dict(h2=h2, xg=xg, block_e=block_e, slot=slot, wrow=wrow, zero=zero)

    def expert_ffn(st, l, after=None):
        block_e = st["block_e"] if after is None else st["block_e"] + after.astype(jnp.int32)
        yg, tok = _experts(st["xg"], block_e, e_gate[l], e_up[l], e_down[l])
        yk = yg.at[st["slot"].reshape(-1)].get(mode="promise_in_bounds").reshape(TOP_K, -1, d)
        return dict(h2=st["h2"], yk=yk, wrow=st["wrow"], zero=tok[0, 0].astype(f32))

    def ffn_out(s, st, l, after=None):
        wrow = st["wrow"] if after is None else st["wrow"] + after
        xs[s] = _ffn_out(xs[s], st["h2"], st["yk"], wrow, s_gate[l], s_up[l], s_down[l], mods[s][l][5], geo,
                         tr)

    if n_streams == 1:
        for l in range(depth):
            ffn_out(0, expert_ffn(mix_and_route(0, l), l), l)
    else:
        pending = None
        for l in range(depth):
            ra = mix_and_route(0, l)
            if pending is not None:
                ffn_out(1, pending, l - 1, after=ra["zero"])
            rb = mix_and_route(1, l)
            ca = expert_ffn(ra, l, after=rb["zero"])
            cb = expert_ffn(rb, l)
            ffn_out(0, ca, l, after=cb["zero"])
            pending = cb
        ffn_out(1, pending, depth - 1)

    out = [_final_norm(xs[s], final_norm_w, nctx, tm).reshape(nb, seq, d) for s in range(n_streams)]
    return jnp.concatenate(out, axis=0)
```

```python
import functools
import math

import numpy as np
import jax
import jax.numpy as jnp
from jax import lax
from jax.experimental import pallas as pl
from jax.experimental.pallas import tpu as pltpu

f32 = jnp.float32
bf16 = jnp.bfloat16
HIGHEST = lax.Precision.HIGHEST

EPS = 1e-6
POS_BASE = 10000.0
GRID_W = 64
N_MOD = 6
HG_HEADS = 4
HG_DK = 128
HG_WIDTH = HG_HEADS * HG_DK
GLA_HEADS = 4
GLA_DK = 64
GLA_DV = 128
GLA_RANK = 16
GLA_TAU = 16.0
SSD_HEADS = 8
SSD_HEADDIM = 64
SSD_INNER = SSD_HEADS * SSD_HEADDIM
SSD_GROUPS = 2
SSD_STATE = 128
SSD_CONV = 5
N_BRANCH = 3
N_EXPERTS = 64
TOP_K = 8
N_GROUPS = 8
TOPK_GROUPS = 4
ROUTE_SCALE = 2.5
MOE_BLOCK = 256
NEG_MASK = -1e4

LANES = 128
SUBLANES = 8
VMEM_BYTES = 64 * 1024 * 1024
VMEM_LIMIT = VMEM_BYTES * 7 // 8
EXP_CLAMP = 80.0

P_GATE = 0
P_HG = 3072
P_GLA = 5632
P_SSD = 7168
P_TOTAL = 8704
S_LRF, S_LRB, S_DTF, S_DTB = 0, 16, 32, 40


def _cparams(sem):
    return pltpu.CompilerParams(dimension_semantics=sem, vmem_limit_bytes=VMEM_LIMIT)


def _split_hi_lo(x):
    hi = x.astype(bf16)
    lo = (x - hi.astype(f32)).astype(bf16)
    return hi, lo


def _dot(a, b):
    return jnp.dot(a, b, preferred_element_type=f32)


def _dot_nt(a, b):
    return lax.dot_general(a, b, (((1,), (1,)), ((), ())), preferred_element_type=f32)


def _dot_tn(a, b):
    return lax.dot_general(a, b, (((0,), (0,)), ((), ())), preferred_element_type=f32)


def _dot2(m, x):
    hi, lo = _split_hi_lo(x)
    return _dot(m, hi) + _dot(m, lo)


def _dot2_l(x, m):
    hi, lo = _split_hi_lo(x)
    return _dot(hi, m) + _dot(lo, m)


def _silu(x):
    return x * jax.nn.sigmoid(x)


def _log_sigmoid(z):
    return jnp.minimum(z, 0.0) - jnp.log(1.0 + jnp.exp(-jnp.abs(z)))


def _softplus(z):
    return jnp.maximum(z, 0.0) + jnp.log(1.0 + jnp.exp(-jnp.abs(z)))


def _scan_mask(c, rev):
    r = lax.broadcasted_iota(jnp.int32, (c, c), 0)
    s = lax.broadcasted_iota(jnp.int32, (c, c), 1)
    return (r <= s) if rev else (r >= s)


def _mods_kernel(c_ref, w_ref, b_ref, o_ref):
    o_ref[...] = jnp.dot(_silu(c_ref[...]), w_ref[...], precision=HIGHEST,
                         preferred_element_type=f32) + b_ref[...]


def _mods(c_all, ada_w, ada_b):
    depth, d, n = ada_w.shape
    tn = 1024
    return pl.pallas_call(
        _mods_kernel,
        grid=(depth, n // tn),
        in_specs=[pl.BlockSpec((SUBLANES, d), lambda l, j: (0, 0)),
                  pl.BlockSpec((None, d, tn), lambda l, j: (l, 0, j)),
                  pl.BlockSpec((None, 1, tn), lambda l, j: (l, 0, j))],
        out_specs=pl.BlockSpec((None, SUBLANES, tn), lambda l, j: (l, 0, j)),
        out_shape=jax.ShapeDtypeStruct((depth, SUBLANES, n), f32),
        compiler_params=_cparams(("parallel", "parallel")),
        name="mods",
    )(c_all, ada_w, ada_b.reshape(depth, 1, n))


def _embed_kernel(ctx_ref, x_ref, pe_ref, o_ref, *, nct):
    i = pl.program_id(0)

    @pl.when(i < nct)
    def _():
        o_ref[...] = ctx_ref[...]

    @pl.when(i >= nct)
    def _():
        o_ref[...] = x_ref[...] + pe_ref[...]


def _embed(ctx2, x2, pe, te):
    nctx, d = ctx2.shape
    nlat = x2.shape[0]
    nct, npe = nctx // te, pe.shape[0] // te
    return pl.pallas_call(
        functools.partial(_embed_kernel, nct=nct),
        grid=((nctx + nlat) // te,),
        in_specs=[pl.BlockSpec((te, d), lambda i: (jnp.minimum(i, nct - 1), 0)),
                  pl.BlockSpec((te, d), lambda i: (jnp.maximum(i - nct, 0), 0)),
                  pl.BlockSpec((te, d), lambda i: (jnp.maximum(i - nct, 0) % npe, 0))],
        out_specs=pl.BlockSpec((te, d), lambda i: (i, 0)),
        out_shape=jax.ShapeDtypeStruct((nctx + nlat, d), f32),
        compiler_params=_cparams(("parallel",)),
        name="embed",
    )(ctx2, x2, pe)


def _norm_rows(x, w):
    return x * lax.rsqrt(jnp.mean(x * x, axis=-1, keepdims=True) + EPS) * w


def _beats(a, b, a_first):
    return (a >= b) if a_first else (a > b)


def _router_kernel(x_ref, w_ref, sh_ref, sc_ref, rwt_ref, rb_ref,
                   h_ref, e_ref, wt_ref, rank_ref, wrow_ref, cnt_ref, *, n_tiles):
    @pl.when(pl.program_id(0) == 0)
    def _():
        cnt_ref[...] = jnp.zeros_like(cnt_ref)

    @pl.when(pl.program_id(0) >= n_tiles)
    def _():
        h_ref[...] = jnp.zeros_like(h_ref)

    @pl.when(pl.program_id(0) < n_tiles)
    def _():
        _route_tile(x_ref, w_ref, sh_ref, sc_ref, rwt_ref, rb_ref,
                    h_ref, e_ref, wt_ref, rank_ref, wrow_ref, cnt_ref)


def _route_tile(x_ref, w_ref, sh_ref, sc_ref, rwt_ref, rb_ref,
                h_ref, e_ref, wt_ref, rank_ref, wrow_ref, cnt_ref):
    y = _norm_rows(x_ref[...], w_ref[...])
    h = y * (1.0 + sc_ref[...]) + sh_ref[...]
    h_ref[...] = h.astype(bf16)
    tm = h.shape[0]
    logits = lax.dot_general(rwt_ref[...], h, (((1,), (1,)), ((), ())), precision=HIGHEST,
                             preferred_element_type=f32)
    scores = jax.nn.sigmoid(logits)
    biased = scores + rb_ref[...]

    gsz = N_EXPERTS // N_GROUPS
    miota = lax.broadcasted_iota(jnp.int32, (gsz, tm), 0)
    blocks, gscore = [], []
    for g in range(N_GROUPS):
        blk = biased[g * gsz:(g + 1) * gsz, :]
        m1 = jnp.max(blk, axis=0, keepdims=True)
        i1 = jnp.min(jnp.where(blk == m1, miota, gsz), axis=0, keepdims=True)
        m2 = jnp.max(jnp.where(miota == i1, -jnp.inf, blk), axis=0, keepdims=True)
        blocks.append(blk)
        gscore.append(m1 + m2)
    vals = []
    for g in range(N_GROUPS):
        ahead = jnp.zeros((1, tm), jnp.int32)
        for o in range(N_GROUPS):
            if o != g:
                ahead = ahead + _beats(gscore[o], gscore[g], o < g).astype(jnp.int32)
        vals.append(jnp.where(ahead < TOPK_GROUPS, blocks[g], NEG_MASK))
    vals = jnp.concatenate(vals, axis=0)

    eiota = lax.broadcasted_iota(jnp.int32, (N_EXPERTS, tm), 0)
    member = jnp.zeros((N_EXPERTS, tm), f32)
    chosen, weights = [], []
    for k in range(TOP_K):
        m = jnp.max(vals, axis=0, keepdims=True)
        ei = jnp.min(jnp.where(vals == m, eiota, N_EXPERTS), axis=0, keepdims=True)
        sel = eiota == ei
        chosen.append(ei)
        weights.append(jnp.sum(jnp.where(sel, scores, 0.0), axis=0, keepdims=True))
        member = member + sel.astype(f32)
        vals = jnp.where(sel, -jnp.inf, vals)
    wsum = weights[0]
    for k in range(1, TOP_K):
        wsum = wsum + weights[k]

    r = lax.broadcasted_iota(jnp.int32, (tm, tm), 0)
    s = lax.broadcasted_iota(jnp.int32, (tm, tm), 1)
    member_bf = member.astype(bf16)
    rank_all = _dot(member_bf, (r < s).astype(bf16)) + cnt_ref[:, 0:1]
    for k in range(TOP_K):
        e_ref[k:k + 1, :] = chosen[k]
        wt_ref[k:k + 1, :] = weights[k] / wsum * ROUTE_SCALE
        rank_ref[k:k + 1, :] = jnp.sum(jnp.where(eiota == chosen[k], rank_all, 0.0), axis=0,
                                       keepdims=True).astype(jnp.int32)
    cnt_ref[...] = cnt_ref[...] + _dot(member_bf, jnp.ones((tm, LANES), bf16))
    eye = (lax.broadcasted_iota(jnp.int32, (TOP_K, LANES), 0)
           == lax.broadcasted_iota(jnp.int32, (TOP_K, LANES), 1)).astype(bf16)
    w_hi, w_lo = _split_hi_lo(wt_ref[...])
    wrow_ref[...] = _dot_tn(w_hi, eye) + _dot_tn(w_lo, eye)


def _mod_block(d, tm, geo):
    nctx, seq, nb = geo
    return pl.BlockSpec((None, 1, d), lambda i: (jnp.where(i * tm < nctx, nb, (i * tm - nctx) // seq), 0, 0))


def _router(xs, w, shift, scale, router_w, router_bias, geo, tm):
    t, d = xs.shape
    ne = router_w.shape[1]
    nt = t // tm
    nt_table = max(nt, -(-VMEM_BYTES // (tm * d * 2)))
    last = nt - 1
    nctx, seq, nb = geo

    def mod(i):
        r0 = jnp.minimum(i, last) * tm
        return (jnp.where(r0 < nctx, nb, (r0 - nctx) // seq), 0, 0)

    kt = pl.BlockSpec((TOP_K, tm), lambda i: (0, jnp.minimum(i, last)))
    return pl.pallas_call(
        functools.partial(_router_kernel, n_tiles=nt), grid=(nt_table,),
        in_specs=[pl.BlockSpec((tm, d), lambda i: (jnp.minimum(i, last), 0)),
                  pl.BlockSpec((1, d), lambda i: (0, 0)),
                  pl.BlockSpec((None, 1, d), mod), pl.BlockSpec((None, 1, d), mod),
                  pl.BlockSpec((ne, d), lambda i: (0, 0)),
                  pl.BlockSpec((ne, 1), lambda i: (0, 0))],
        out_specs=[pl.BlockSpec((tm, d), lambda i: (i, 0)), kt, kt, kt,
                   pl.BlockSpec((tm, LANES), lambda i: (jnp.minimum(i, last), 0)),
                   pl.BlockSpec((ne, LANES), lambda i: (0, 0))],
        out_shape=[jax.ShapeDtypeStruct((nt_table * tm, d), bf16),
                   jax.ShapeDtypeStruct((TOP_K, t), jnp.int32),
                   jax.ShapeDtypeStruct((TOP_K, t), f32),
                   jax.ShapeDtypeStruct((TOP_K, t), jnp.int32),
                   jax.ShapeDtypeStruct((t, LANES), f32),
                   jax.ShapeDtypeStruct((ne, LANES), f32)],
        compiler_params=_cparams(("arbitrary",)), name="router",
    )(xs, w.reshape(1, d), shift, scale, router_w.T, router_bias.astype(f32).reshape(ne, 1))


PROJ_TN = 512


def _in_proj_kernel(x_ref, nw_ref, sh_ref, sc_ref, w_ref, ws_ref, wst_ref, p_ref, ps_ref, pst_ref):
    y = _norm_rows(x_ref[...], nw_ref[...])
    h = (y * (1.0 + sc_ref[...]) + sh_ref[...]).astype(bf16)
    for j in range(p_ref.shape[1] // PROJ_TN):
        sl = slice(j * PROJ_TN, (j + 1) * PROJ_TN)
        p_ref[:, sl] = _dot(h, w_ref[:, sl]).astype(p_ref.dtype)
    ps_ref[...] = _dot(h, ws_ref[...])
    pst_ref[...] = _dot_nt(wst_ref[...], h)


def _in_proj(xs, nw, shift, scale, w_main, w_small, geo, tm):
    t, d = xs.shape
    n = w_main.shape[1]
    ns = w_small.shape[1]

    def resident(shape):
        return pl.BlockSpec(shape, lambda i: (0, 0), pipeline_mode=pl.Buffered(1))

    return pl.pallas_call(
        _in_proj_kernel, grid=(t // tm,),
        in_specs=[pl.BlockSpec((tm, d), lambda i: (i, 0)),
                  pl.BlockSpec((1, d), lambda i: (0, 0)),
                  _mod_block(d, tm, geo), _mod_block(d, tm, geo),
                  resident((d, n)), resident((d, ns)), resident((ns, d))],
        out_specs=[pl.BlockSpec((tm, n), lambda i: (i, 0)),
                   pl.BlockSpec((tm, ns), lambda i: (i, 0)),
                   pl.BlockSpec((ns, tm), lambda i: (0, i))],
        out_shape=[jax.ShapeDtypeStruct((t, n), bf16),
                   jax.ShapeDtypeStruct((t, ns), f32),
                   jax.ShapeDtypeStruct((ns, t), f32)],
        compiler_params=_cparams(("parallel",)), name="in_proj",
        cost_estimate=pl.CostEstimate(flops=2 * t * d * (n + 2 * ns), transcendentals=t,
                                      bytes_accessed=t * (4 * d + 2 * n + 8 * ns) + 2 * d * (n + 2 * ns)),
    )(xs, nw.reshape(1, d), shift, scale, w_main, w_small, w_small.T)


def _chunk_blocks(geo, c, rev):
    nctx, seq, nb = geo
    lc = nctx // nb
    ncc, nlc = lc // c, seq // c

    def blk(b, i):
        j_ctx = (ncc - 1 - i) if rev else i
        j_lat = (nlc - 1 - (i - ncc)) if rev else (i - ncc)
        return jnp.where(i < ncc, b * ncc + j_ctx, nctx // c + b * nlc + j_lat)

    return blk, ncc + nlc


def _sub_chunks(n_rows, c, rev):
    n = n_rows // c
    order = range(n - 1, -1, -1) if rev else range(n)
    return [slice(j * c, (j + 1) * c) for j in order]


def _decay_terms(q, k, g, mask, rev):
    c = q.shape[0]
    b = _dot2(mask.astype(bf16), g)
    last = 0 if rev else c - 1
    b_tot = b[last:last + 1, :]
    b_mid = b[c // 2:c // 2 + 1, :]
    rel = b - b_mid
    qd = q * jnp.exp(jnp.minimum(rel, EXP_CLAMP))
    kd = k * jnp.exp(jnp.minimum(-rel, EXP_CLAMP))
    qe = qd * jnp.exp(b_mid)
    ke = kd * jnp.exp(b_tot - b_mid)
    return qd.astype(bf16), kd.astype(bf16), qe.astype(bf16), ke.astype(bf16), jnp.exp(b_tot)


def _hgrn2_gates(z, lb):
    e = jnp.exp(-jnp.abs(z))
    inv = 1.0 / (1.0 + e)
    sig_pos = jnp.where(z >= 0, 1.0, e) * inv
    sig_neg = jnp.where(z >= 0, e, 1.0) * inv
    ls = jnp.minimum(z, 0.0) - jnp.log(1.0 + e)
    log_f = jnp.where(lb > 0.0, jnp.log(jnp.where(lb > 0.0, lb + (1.0 - lb) * sig_pos, 1.0)), ls)
    return (1.0 - lb) * sig_neg, log_f


def _hgrn2_kernel(qf_ref, zf_ref, vf_ref, qb_ref, zb_ref, vb_ref, lbf_ref, lbb_ref,
                  of_ref, ob_ref, sf_ref, sb_ref, *, c):
    @pl.when(pl.program_id(1) == 0)
    def _():
        sf_ref[...] = jnp.zeros_like(sf_ref)
        sb_ref[...] = jnp.zeros_like(sb_ref)

    for rev, q_ref, z_ref, v_ref, lb_ref, o_ref, s_ref in (
            (False, qf_ref, zf_ref, vf_ref, lbf_ref, of_ref, sf_ref),
            (True, qb_ref, zb_ref, vb_ref, lbb_ref, ob_ref, sb_ref)):
        mask = _scan_mask(c, rev)
        for rows in _sub_chunks(q_ref.shape[0], c, rev):
            q = q_ref[rows, :].astype(f32)
            k, g = _hgrn2_gates(z_ref[rows, :].astype(f32), lb_ref[...])
            v = v_ref[rows, :]
            qd, kd, qe, ke, dtot = _decay_terms(q, k, g, mask, rev)
            for h in range(HG_HEADS):
                sl = slice(h * HG_DK, (h + 1) * HG_DK)
                st = s_ref[h]
                att = jnp.where(mask, _dot_nt(qd[:, sl], kd[:, sl]), 0.0)
                o = _dot(att.astype(bf16), v[:, sl]) + _dot_nt(qe[:, sl], st.astype(bf16))
                o_ref[rows, sl] = o.astype(o_ref.dtype)
                s_ref[h] = st * dtot[:, sl] + _dot_tn(v[:, sl], ke[:, sl])


def _hgrn2_scan(p, lb_f, lb_b, geo, c, r):
    t = p.shape[0]
    nb = geo[2]
    blk_f, steps = _chunk_blocks(geo, r, False)
    blk_b, _ = _chunk_blocks(geo, r, True)
    w = HG_WIDTH
    c0 = P_HG // w

    def spec(blk, col):
        return pl.BlockSpec((r, w), lambda b, i: (blk(b, i), col))

    vec = pl.BlockSpec((1, w), lambda b, i: (0, 0))
    return pl.pallas_call(
        functools.partial(_hgrn2_kernel, c=c), grid=(nb, steps),
        in_specs=[spec(blk_f, c0), spec(blk_f, c0 + 1), spec(blk_f, c0 + 3),
                  spec(blk_b, c0), spec(blk_b, c0 + 2), spec(blk_b, c0 + 3), vec, vec],
        out_specs=[spec(blk_f, 0), spec(blk_b, 0)],
        out_shape=[jax.ShapeDtypeStruct((t, w), bf16)] * 2,
        scratch_shapes=[pltpu.VMEM((HG_HEADS, HG_DK, HG_DK), f32)] * 2,
        compiler_params=_cparams(("parallel", "arbitrary")), name="hgrn2_scan",
    )(p, p, p, p, p, p, lb_f.reshape(1, w), lb_b.reshape(1, w))


def _gla_kernel(qf_ref, kf_ref, vf_ref, sf_in_ref, qb_ref, kb_ref, vb_ref, sb_in_ref,
                gwf_ref, gwb_ref, gbf_ref, gbb_ref, of_ref, ob_ref, sf_ref, sb_ref, *, c):
    @pl.when(pl.program_id(1) == 0)
    def _():
        sf_ref[...] = jnp.zeros_like(sf_ref)
        sb_ref[...] = jnp.zeros_like(sb_ref)

    npair = GLA_HEADS // 2
    lane = lax.broadcasted_iota(jnp.int32, (1, LANES), 1)
    vrow = lax.broadcasted_iota(jnp.int32, (2 * GLA_DV, LANES), 0)
    vcol = lax.broadcasted_iota(jnp.int32, (2 * GLA_DV, LANES), 1)
    block_diag = (vrow // GLA_DV) == (vcol // GLA_DK)
    for rev, q_ref, k_ref, v_ref, sm_ref, gw_ref, gb_ref, o_ref, s_ref in (
            (False, qf_ref, kf_ref, vf_ref, sf_in_ref, gwf_ref, gbf_ref, of_ref, sf_ref),
            (True, qb_ref, kb_ref, vb_ref, sb_in_ref, gwb_ref, gbb_ref, ob_ref, sb_ref)):
        mask = _scan_mask(c, rev)
        for rows in _sub_chunks(q_ref.shape[0], c, rev):
            q = q_ref[rows, :].astype(f32) * (GLA_DK ** -0.5)
            k = k_ref[rows, :].astype(f32)
            v = v_ref[rows, :]
            pre = _dot2_l(sm_ref[rows, :], gw_ref[...]) + gb_ref[...]
            g = _log_sigmoid(pre) / GLA_TAU
            qd, kd, qe, ke, dtot = _decay_terms(q, k, g, mask, rev)
            for p in range(npair):
                sl = slice(p * LANES, (p + 1) * LANES)
                vsl = slice(p * 2 * GLA_DV, (p + 1) * 2 * GLA_DV)
                st = s_ref[p]
                o_inter = _dot_nt(qe[:, sl], st.astype(bf16))
                for hh in range(2):
                    head = (lane // GLA_DK) == hh
                    qh = jnp.where(head, qd[:, sl], jnp.zeros_like(qd[:, sl]))
                    att = jnp.where(mask, _dot_nt(qh, kd[:, sl]), 0.0)
                    osl = slice((2 * p + hh) * GLA_DV, (2 * p + hh + 1) * GLA_DV)
                    o = _dot(att.astype(bf16), v[:, osl]) + o_inter[:, hh * GLA_DV:(hh + 1) * GLA_DV]
                    o_ref[rows, osl] = o.astype(o_ref.dtype)
                upd = st * dtot[:, sl] + _dot_tn(v[:, vsl], ke[:, sl])
                s_ref[p] = jnp.where(block_diag, upd, 0.0)


def _gla_scan(p, psmall, gate_w, gate_b, geo, c, r):
    t = p.shape[0]
    nb = geo[2]
    blk_f, steps = _chunk_blocks(geo, r, False)
    blk_b, _ = _chunk_blocks(geo, r, True)
    wk, wv = GLA_HEADS * GLA_DK, GLA_HEADS * GLA_DV

    def spec(blk, width, col):
        return pl.BlockSpec((r, width), lambda b, i: (blk(b, i), col))

    def const(shape):
        return pl.BlockSpec(shape, lambda b, i: (0, 0))

    gwf = jnp.zeros((LANES, wk), f32).at[S_LRF:S_LRF + GLA_RANK].set(gate_w[0]).astype(bf16)
    gwb = jnp.zeros((LANES, wk), f32).at[S_LRB:S_LRB + GLA_RANK].set(gate_w[1]).astype(bf16)
    qcol, kcol, vcol = P_GLA // wk, P_GLA // wk + 1, (P_GLA + 2 * wk) // wv
    return pl.pallas_call(
        functools.partial(_gla_kernel, c=c), grid=(nb, steps),
        in_specs=[spec(blk_f, wk, qcol), spec(blk_f, wk, kcol), spec(blk_f, wv, vcol), spec(blk_f, LANES, 0),
                  spec(blk_b, wk, qcol), spec(blk_b, wk, kcol), spec(blk_b, wv, vcol), spec(blk_b, LANES, 0),
                  const((LANES, wk)), const((LANES, wk)), const((1, wk)), const((1, wk))],
        out_specs=[spec(blk_f, wv, 0), spec(blk_b, wv, 0)],
        out_shape=[jax.ShapeDtypeStruct((t, wv), bf16)] * 2,
        scratch_shapes=[pltpu.VMEM((GLA_HEADS // 2, 2 * GLA_DV, 2 * GLA_DK), f32)] * 2,
        compiler_params=_cparams(("parallel", "arbitrary")), name="gla_scan",
    )(p, p, p, psmall, p, p, p, psmall, gwf, gwb, gate_b[0].reshape(1, wk), gate_b[1].reshape(1, wk))


HALO = SUBLANES


def _conv_kernel(prev_ref, cur_ref, next_ref, w_ref, b_ref, o_ref, ext_ref, *, geo, tc):
    nctx, seq, nb = geo
    r0 = pl.program_id(0) * tc
    in_ctx = r0 < nctx
    seg_len = jnp.where(in_ctx, nctx // nb, seq)
    pos = jnp.where(in_ctx, r0 % (nctx // nb), (r0 - nctx) % seq)
    first = pos == 0
    last = pos + tc == seg_len
    ext_ref[0:HALO, :] = jnp.where(first, 0.0, prev_ref[...].astype(f32))
    ext_ref[HALO:HALO + tc, :] = cur_ref[...].astype(f32)
    ext_ref[HALO + tc:HALO + tc + HALO, :] = jnp.where(last, 0.0, next_ref[...].astype(f32))
    acc = jnp.zeros(o_ref.shape, f32) + b_ref[...]
    for j in range(SSD_CONV):
        off = HALO - SSD_CONV // 2 + j
        acc = acc + ext_ref[off:off + tc, :] * w_ref[j:j + 1, :]
    o_ref[...] = _silu(acc).astype(o_ref.dtype)


def _ssd_conv(p, conv_w, conv_b, geo, tc):
    t = p.shape[0]
    nch = conv_w.shape[1]
    tw = 512
    col0 = (P_SSD + SSD_INNER) // tw
    hb = tc // HALO
    nhb = t // HALO
    return pl.pallas_call(
        functools.partial(_conv_kernel, geo=geo, tc=tc),
        grid=(t // tc, nch // tw),
        in_specs=[pl.BlockSpec((HALO, tw), lambda i, j: (jnp.maximum(i * hb - 1, 0), col0 + j)),
                  pl.BlockSpec((tc, tw), lambda i, j: (i, col0 + j)),
                  pl.BlockSpec((HALO, tw), lambda i, j: (jnp.minimum((i + 1) * hb, nhb - 1), col0 + j)),
                  pl.BlockSpec((SSD_CONV, tw), lambda i, j: (0, j)),
                  pl.BlockSpec((1, tw), lambda i, j: (0, j))],
        out_specs=pl.BlockSpec((tc, tw), lambda i, j: (i, j)),
        out_shape=jax.ShapeDtypeStruct((t, nch), bf16),
        scratch_shapes=[pltpu.VMEM((tc + 2 * HALO, tw), f32)],
        compiler_params=_cparams(("parallel", "parallel")), name="ssd_conv",
    )(p, p, p, conv_w, conv_b.reshape(1, nch))


def _ssd_kernel(xf_ref, bf_ref, cf_ref, smf_ref, smtf_ref, xb_ref, bb_ref, cb_ref, smb_ref, smtb_ref,
                arow_ref, brow_ref, acol_ref, bcol_ref, ef_ref, eb_ref,
                yf_ref, yb_ref, sf_ref, sb_ref, *, c):
    @pl.when(pl.program_id(1) == 0)
    def _():
        sf_ref[...] = jnp.zeros_like(sf_ref)
        sb_ref[...] = jnp.zeros_like(sb_ref)

    gw = SSD_INNER // SSD_GROUPS
    lane = lax.broadcasted_iota(jnp.int32, (1, LANES), 1)
    for d, rev, x_ref, bm_ref, cm_ref, sm_ref, smt_ref, e_ref, y_ref, s_ref, lane0 in (
            (0, False, xf_ref, bf_ref, cf_ref, smf_ref, smtf_ref, ef_ref, yf_ref, sf_ref, S_DTF),
            (1, True, xb_ref, bb_ref, cb_ref, smb_ref, smtb_ref, eb_ref, yb_ref, sb_ref, S_DTB)):
        mask = _scan_mask(c, rev)
        mask_bf = mask.astype(bf16)
        mask_t_bf = _scan_mask(c, not rev).astype(bf16)
        expand = e_ref[...]
        for rows in _sub_chunks(x_ref.shape[0], c, rev):
            dt = _softplus(sm_ref[rows, :] + brow_ref[d:d + 1, :])
            cum = _dot2(mask_bf, dt * arow_ref[d:d + 1, :])
            last = 0 if rev else c - 1
            cum_end = cum[last:last + 1, :]
            dt_t = _softplus(smt_ref[:, rows] + bcol_ref[d])
            cum_t = _dot2_l(dt_t * acol_ref[d], mask_t_bf)

            x = x_ref[rows, :].astype(f32)
            xdt = x * _dot2_l(dt, expand)
            w = (xdt * _dot2_l(jnp.exp(cum_end - cum), expand)).astype(bf16)
            xdt = xdt.astype(bf16)
            dec_t = _dot2_l(jnp.exp(cum), expand)
            dec_end = _dot2_l(jnp.broadcast_to(jnp.exp(cum_end), (SUBLANES, LANES)), expand)[0:1, :]
            bm = bm_ref[rows, :]
            cm = cm_ref[rows, :]
            for g in range(SSD_GROUPS):
                gsl = slice(g * SSD_STATE, (g + 1) * SSD_STATE)
                hsl = slice(g * gw, (g + 1) * gw)
                st = s_ref[:, hsl]
                scores = _dot_nt(cm[:, gsl], bm[:, gsl])
                y_inter = _dot(cm[:, gsl], st.astype(bf16)) * dec_t[:, hsl]
                for tile in range(gw // LANES):
                    tsl = slice(g * gw + tile * LANES, g * gw + (tile + 1) * LANES)
                    xt = xdt[:, tsl]
                    acc = y_inter[:, tile * LANES:(tile + 1) * LANES]
                    for hh in range(LANES // SSD_HEADDIM):
                        h = (g * gw + tile * LANES) // SSD_HEADDIM + hh
                        diff = cum[:, lane0 + h:lane0 + h + 1] - cum_t[h:h + 1, :]
                        seg = jnp.where(mask, jnp.exp(jnp.minimum(diff, 0.0)), 0.0)
                        xh = jnp.where((lane // SSD_HEADDIM) == hh, xt, jnp.zeros_like(xt))
                        acc = acc + _dot((scores * seg).astype(bf16), xh)
                    y_ref[rows, tsl] = acc.astype(y_ref.dtype)
                s_ref[:, hsl] = st * dec_end[:, hsl] + _dot_tn(bm[:, gsl], w[:, hsl])


def _ssd_scan(xbc, psmall, psmall_t, a_log, dt_bias, geo, c, r):
    t = xbc.shape[0]
    nb = geo[2]
    blk_f, steps = _chunk_blocks(geo, r, False)
    blk_b, _ = _chunk_blocks(geo, r, True)
    wi, wn = SSD_INNER, SSD_GROUPS * SSD_STATE

    def spec(blk, width, col):
        return pl.BlockSpec((r, width), lambda b, i: (blk(b, i), col))

    def spec_t(blk, row):
        return pl.BlockSpec((SUBLANES, r), lambda b, i: (row, blk(b, i)))

    def const(shape):
        return pl.BlockSpec(shape, lambda b, i: (0,) * len(shape))

    a = -jnp.exp(a_log.astype(f32))
    arow = jnp.zeros((2, LANES), f32).at[0, S_DTF:S_DTF + SSD_HEADS].set(a[0]).at[1, S_DTB:S_DTB + SSD_HEADS].set(a[1])
    brow = jnp.zeros((2, LANES), f32).at[0, S_DTF:S_DTF + SSD_HEADS].set(dt_bias[0]).at[1, S_DTB:S_DTB + SSD_HEADS].set(dt_bias[1])
    head_of_lane = np.arange(wi) // SSD_HEADDIM
    ef = jnp.asarray((np.arange(LANES)[:, None] == S_DTF + head_of_lane[None, :]), bf16)
    eb = jnp.asarray((np.arange(LANES)[:, None] == S_DTB + head_of_lane[None, :]), bf16)
    xcol, bcol, ccol = 0, wi // wn, wi // wn + 1
    return pl.pallas_call(
        functools.partial(_ssd_kernel, c=c), grid=(nb, steps),
        in_specs=[spec(blk_f, wi, xcol), spec(blk_f, wn, bcol), spec(blk_f, wn, ccol), spec(blk_f, LANES, 0),
                  spec_t(blk_f, S_DTF // SUBLANES),
                  spec(blk_b, wi, xcol), spec(blk_b, wn, bcol), spec(blk_b, wn, ccol), spec(blk_b, LANES, 0),
                  spec_t(blk_b, S_DTB // SUBLANES),
                  const((2, LANES)), const((2, LANES)), const((2, SSD_HEADS, 1)), const((2, SSD_HEADS, 1)),
                  const((LANES, wi)), const((LANES, wi))],
        out_specs=[spec(blk_f, wi, 0), spec(blk_b, wi, 0)],
        out_shape=[jax.ShapeDtypeStruct((t, wi), bf16)] * 2,
        scratch_shapes=[pltpu.VMEM((SSD_STATE, wi), f32)] * 2,
        compiler_params=_cparams(("parallel", "arbitrary")), name="ssd_scan",
    )(xbc, xbc, xbc, psmall, psmall_t, xbc, xbc, xbc, psmall, psmall_t,
      arow, brow, a.reshape(2, SSD_HEADS, 1), dt_bias.astype(f32).reshape(2, SSD_HEADS, 1), ef, eb)


def _group_norm(o, w, width):
    parts = []
    for j in range(o.shape[1] // width):
        blk = o[:, j * width:(j + 1) * width]
        parts.append(blk * lax.rsqrt(jnp.mean(blk * blk, axis=-1, keepdims=True) + EPS))
    return jnp.concatenate(parts, axis=1) * w


def _merge_kernel(x_ref, hgf_ref, hgb_ref, hgg_ref, glf_ref, glb_ref, glg_ref,
                  sdf_ref, sdb_ref, sdx_ref, sdz_ref, gate_ref,
                  hgw_ref, glw_ref, sdd_ref, sdw_ref, wh_ref, wg_ref, ws_ref, wo_ref, m2_ref, o_ref):
    o_hg = _group_norm(hgf_ref[...].astype(f32) + hgb_ref[...].astype(f32), hgw_ref[...], HG_DK)
    o_hg = o_hg * _silu(hgg_ref[...].astype(f32))
    o_gl = _group_norm(glf_ref[...].astype(f32) + glb_ref[...].astype(f32), glw_ref[...], GLA_DV)
    o_gl = o_gl * _silu(glg_ref[...].astype(f32))
    y = sdf_ref[...].astype(f32) + sdb_ref[...].astype(f32) + sdd_ref[...] * sdx_ref[...].astype(f32)
    o_sd = _group_norm(y * _silu(sdz_ref[...].astype(f32)), sdw_ref[...], SSD_INNER // SSD_GROUPS)
    d = x_ref.shape[1]
    gate = jax.nn.sigmoid(gate_ref[...].astype(f32))
    y = (gate[:, 0:d] * _dot(o_hg.astype(bf16), wh_ref[...])
         + gate[:, d:2 * d] * _dot(o_gl.astype(bf16), wg_ref[...])
         + gate[:, 2 * d:3 * d] * _dot(o_sd.astype(bf16), ws_ref[...]))
    o_ref[...] = x_ref[...] + m2_ref[...] * _dot(y.astype(bf16), wo_ref[...])


def _merge(xs, p, hg_f, hg_b, gl_f, gl_b, sd_f, sd_b, xbc, hg_norm_w, gla_norm_w, ssd_d, ssd_norm_w,
           w_hg, w_gla, w_ssd, w_out, mod2, geo, tm):
    t, d = xs.shape
    w = 512

    def rows(width, col):
        return pl.BlockSpec((tm, width), lambda i: (i, col))

    def const(shape):
        return pl.BlockSpec(shape, lambda i: (0, 0))

    tile4 = lambda v: jnp.tile(v, w // v.shape[0]).reshape(1, w)
    return pl.pallas_call(
        _merge_kernel, grid=(t // tm,),
        in_specs=[rows(d, 0),
                  rows(w, 0), rows(w, 0), rows(w, (P_HG + 4 * w) // w),
                  rows(w, 0), rows(w, 0), rows(w, (P_GLA + 2 * w) // w),
                  rows(w, 0), rows(w, 0), rows(w, 0), rows(w, P_SSD // w),
                  rows(3 * d, P_GATE // (3 * d)),
                  const((1, w)), const((1, w)), const((1, w)), const((1, w)),
                  const((w, d)), const((w, d)), const((w, d)), const((d, d)),
                  _mod_block(d, tm, geo)],
        out_specs=rows(d, 0),
        out_shape=jax.ShapeDtypeStruct((t, d), f32),
        compiler_params=_cparams(("parallel",)), name="merge",
        cost_estimate=pl.CostEstimate(flops=2 * t * d * (3 * w + d), transcendentals=t * (3 * w + 3 * d),
                                      bytes_accessed=t * (8 * d + 2 * 11 * w + 6 * d) + 2 * d * (3 * w + d)),
    )(xs, hg_f, hg_b, p, gl_f, gl_b, p, sd_f, sd_b, xbc, p, p,
      tile4(hg_norm_w), tile4(gla_norm_w), jnp.repeat(ssd_d, SSD_HEADDIM).reshape(1, w), ssd_norm_w.reshape(1, w),
      w_hg, w_gla, w_ssd, w_out, mod2)


MOE_SUB = 4


def _expert_kernel(be_ref, x_ref, *refs):
    o_ref, tok_ref = refs[-2:]
    first = pl.program_id(0) * MOE_SUB
    same = be_ref[first + 1] == be_ref[first]
    for s in range(2, MOE_SUB):
        same = jnp.logical_and(same, be_ref[first + s] == be_ref[first])

    def ffn(x, wg_ref, wu_ref, wd_ref):
        a = _silu(_dot(x, wg_ref[...])) * _dot(x, wu_ref[...])
        return _dot(a.astype(bf16), wd_ref[...]).astype(o_ref.dtype)

    tok_ref[...] = jnp.zeros_like(tok_ref)

    @pl.when(same)
    def _():
        o_ref[...] = ffn(x_ref[...], *refs[0:3])

    @pl.when(jnp.logical_not(same))
    def _():
        for s in range(MOE_SUB):
            rows = slice(s * MOE_BLOCK, (s + 1) * MOE_BLOCK)
            o_ref[rows, :] = ffn(x_ref[rows, :], *refs[3 * s:3 * s + 3])


def _experts(xg, block_e, w_gate, w_up, w_down):
    npad, d = xg.shape
    ff = w_gate.shape[2]
    rows = MOE_SUB * MOE_BLOCK
    w_specs = []
    for s in range(MOE_SUB):
        pick = lambda i, be, s=s: (be[i * MOE_SUB + s], 0, 0)
        w_specs += [pl.BlockSpec((None, d, ff), pick), pl.BlockSpec((None, d, ff), pick),
                    pl.BlockSpec((None, ff, d), pick)]
    grid_spec = pltpu.PrefetchScalarGridSpec(
        num_scalar_prefetch=1, grid=(npad // rows,),
        in_specs=[pl.BlockSpec((rows, d), lambda i, be: (i, 0))] + w_specs,
        out_specs=[pl.BlockSpec((rows, d), lambda i, be: (i, 0)),
                   pl.BlockSpec((SUBLANES, LANES), lambda i, be: (0, 0))])
    return pl.pallas_call(
        _expert_kernel, grid_spec=grid_spec,
        out_shape=[jax.ShapeDtypeStruct((npad, d), bf16),
                   jax.ShapeDtypeStruct((SUBLANES, LANES), jnp.int32)],
        compiler_params=_cparams(("arbitrary",)), name="experts",
        cost_estimate=pl.CostEstimate(flops=6 * npad * d * ff, transcendentals=npad * ff,
                                      bytes_accessed=4 * npad * d + 6 * N_EXPERTS * d * ff),
    )(block_e, xg, *([w_gate, w_up, w_down] * MOE_SUB))


def _cast_kernel(x_ref, o_ref):
    o_ref[...] = x_ref[...].astype(o_ref.dtype)


def _cast_expert_weights(w, group):
    depth, ne, a, b = w.shape
    spec = pl.BlockSpec((None, group, a, b), lambda l, j: (l, j, 0, 0))
    return pl.pallas_call(
        _cast_kernel, grid=(depth, ne // group), in_specs=[spec], out_specs=spec,
        out_shape=jax.ShapeDtypeStruct(w.shape, bf16),
        compiler_params=_cparams(("parallel", "parallel")), name="cast_experts",
    )(w)


def _slots_kernel(ps_ref, e_ref, r_ref, o_ref):
    e = e_ref[...]
    acc = r_ref[...]
    for j in range(N_EXPERTS):
        acc = acc + jnp.where(e == j, ps_ref[j], 0)
    o_ref[...] = acc


def _slots(experts, rank, pad_start, tl):
    k, t = experts.shape
    spec = pl.BlockSpec((k, tl), lambda i, ps: (0, i))
    return pl.pallas_call(
        _slots_kernel,
        grid_spec=pltpu.PrefetchScalarGridSpec(num_scalar_prefetch=1, grid=(t // tl,),
                                               in_specs=[spec, spec], out_specs=spec),
        out_shape=jax.ShapeDtypeStruct((k, t), jnp.int32),
        compiler_params=_cparams(("parallel",)), name="slots",
    )(pad_start, experts, rank)


def _ffn_out_kernel(x_ref, h_ref, y_ref, wr_ref, sg_ref, su_ref, sd_ref, m5_ref, o_ref):
    h = h_ref[...]
    a = _silu(_dot(h, sg_ref[...])) * _dot(h, su_ref[...])
    acc = _dot(a.astype(bf16), sd_ref[...])
    wr = wr_ref[...]
    for k in range(TOP_K):
        acc = acc + y_ref[k].astype(f32) * wr[:, k:k + 1]
    o_ref[...] = x_ref[...] + m5_ref[...] * acc


def _ffn_out(xs, h, yk, wrow, s_gate, s_up, s_down, mod5, geo, tm):
    t, d = xs.shape
    ff = s_gate.shape[1]
    rows = pl.BlockSpec((tm, d), lambda i: (i, 0))
    return pl.pallas_call(
        _ffn_out_kernel, grid=(t // tm,),
        in_specs=[rows, rows, pl.BlockSpec((TOP_K, tm, d), lambda i: (0, i, 0)),
                  pl.BlockSpec((tm, LANES), lambda i: (i, 0)),
                  pl.BlockSpec((d, ff), lambda i: (0, 0)), pl.BlockSpec((d, ff), lambda i: (0, 0)),
                  pl.BlockSpec((ff, d), lambda i: (0, 0)), _mod_block(d, tm, geo)],
        out_specs=rows,
        out_shape=jax.ShapeDtypeStruct((t, d), f32),
        compiler_params=_cparams(("parallel",)), name="ffn_out",
    )(xs, h, yk, wrow, s_gate, s_up, s_down, mod5)


def _dispatch(experts, rank, counts, tl):
    t = experts.shape[1]
    n_assign = t * TOP_K
    counts = counts[:, 0].astype(jnp.int32)
    padded = (counts + MOE_BLOCK - 1) // MOE_BLOCK * MOE_BLOCK
    pad_end = jnp.cumsum(padded)
    pad_start = (pad_end - padded).astype(jnp.int32)
    slot = _slots(experts, rank, pad_start, tl)
    n_blocks = (n_assign + MOE_BLOCK - 1) // MOE_BLOCK + N_EXPERTS
    n_blocks = (n_blocks + MOE_SUB - 1) // MOE_SUB * MOE_SUB
    token = jnp.broadcast_to(jnp.arange(t, dtype=jnp.int32), (TOP_K, t)).reshape(-1)
    slot_flat = slot.reshape(-1)
    filler = jnp.arange(n_blocks * MOE_BLOCK, dtype=jnp.int32) % t
    buf_t = filler.at[slot_flat].add(token - slot_flat % t, unique_indices=True)
    first_row = jnp.arange(n_blocks, dtype=jnp.int32) * MOE_BLOCK
    block_e = jnp.minimum(jnp.sum(pad_end[None, :] <= first_row[:, None], axis=1), N_EXPERTS - 1)
    return slot, buf_t, block_e.astype(jnp.int32)


def _final_norm_kernel(x_ref, w_ref, o_ref):
    o_ref[...] = _norm_rows(x_ref[...], w_ref[...])


def _final_norm(xs, w, nctx, tm):
    t, d = xs.shape
    off = nctx // tm
    return pl.pallas_call(
        _final_norm_kernel, grid=((t - nctx) // tm,),
        in_specs=[pl.BlockSpec((tm, d), lambda i: (i + off, 0)), pl.BlockSpec((1, d), lambda i: (0, 0))],
        out_specs=pl.BlockSpec((tm, d), lambda i: (i, 0)),
        out_shape=jax.ShapeDtypeStruct((t - nctx, d), f32),
        compiler_params=_cparams(("parallel",)), name="final_norm",
    )(xs, w.reshape(1, d))


def _grid_sincos(rows, cols, dim):
    quarter = dim // 4
    omega = 1.0 / (POS_BASE ** (jnp.arange(quarter, dtype=f32) / quarter))

    def axis_embed(n):
        ang = jnp.arange(n, dtype=f32)[:, None] * omega
        return jnp.concatenate([jnp.sin(ang), jnp.cos(ang)], axis=-1)

    er, ec = axis_embed(rows), axis_embed(cols)
    pe = jnp.concatenate([jnp.broadcast_to(er[:, None], (rows, cols, dim // 2)),
                          jnp.broadcast_to(ec[None], (rows, cols, dim // 2))], axis=-1)
    return pe.reshape(rows * cols, dim)


def _lower_bounds(logits):
    p = jax.nn.softmax(logits.astype(f32), axis=1)
    return jnp.cumsum(p, axis=1) - p[:, :1]


def _split_w_in(w_in):
    a0 = 5 * HG_WIDTH + 2 * GLA_HEADS * GLA_DK + 2 * GLA_HEADS * GLA_DV
    a1 = a0 + 2 * GLA_RANK
    b0 = a1 + 2 * SSD_INNER + 2 * SSD_GROUPS * SSD_STATE
    b1 = b0 + 2 * SSD_HEADS
    main = jnp.concatenate([w_in[..., b1:], w_in[..., :a0], w_in[..., a1:b0]], axis=-1).astype(bf16)
    small = jnp.concatenate([w_in[..., a0:a1], w_in[..., b0:b1]], axis=-1)
    small = jnp.pad(small, ((0, 0), (0, 0), (0, LANES - small.shape[-1]))).astype(bf16)
    return main, small


def kernel(x, c, ctx, c_ctx, norm_mix_w, norm_ffn_w, final_norm_w, ada_w, ada_b, w_in,
           hg_lb_logits, hg_norm_w, gla_gate_w, gla_gate_b, gla_norm_w, ssd_conv_w, ssd_conv_b,
           ssd_a_log, ssd_dt_bias, ssd_d, ssd_norm_w, w_br_hg, w_br_gla, w_br_ssd, w_out,
           router_w, router_bias, exp_w_gate, exp_w_up, exp_w_down, sh_w_gate, sh_w_up, sh_w_down):
    nb_all, seq, d = x.shape
    lc = ctx.shape[1]
    depth = w_in.shape[0]
    tm = min(512, lc)
    tr = min(256, lc)
    r_scan = min(256, lc)
    c_hg = min(64, lc)
    c_gla = min(128, lc)
    c_ssd = min(128, lc)

    n_streams = 2 if nb_all % 2 == 0 else 1
    nb = nb_all // n_streams
    nctx = nb * lc
    geo = (nctx, seq, nb)

    pe = _grid_sincos(seq // GRID_W, GRID_W, d)
    lb = _lower_bounds(hg_lb_logits)
    w_main, w_small = _split_w_in(w_in)
    w_hg, w_gla, w_ssd, w_o = (w.astype(bf16) for w in (w_br_hg, w_br_gla, w_br_ssd, w_out))
    e_gate, e_up, e_down = (_cast_expert_weights(w, 8) for w in (exp_w_gate, exp_w_up, exp_w_down))
    s_gate, s_up, s_down = (w.astype(bf16) for w in (sh_w_gate, sh_w_up, sh_w_down))

    mods, xs = [], []
    for s in range(n_streams):
        bsl = slice(s * nb, (s + 1) * nb)
        c_all = jnp.zeros((SUBLANES, d), f32).at[:nb].set(c[bsl]).at[nb].set(c_ctx)
        m = _mods(c_all, ada_w, ada_b)
        mods.append(m.reshape(depth, SUBLANES, N_MOD, 1, d).transpose(0, 2, 1, 3, 4))
        xs.append(_embed(ctx[bsl].reshape(nctx, d), x[bsl].reshape(nb * seq, d), pe, tm))

    def mix_and_route(s, l):
        m = mods[s][l]
        p, psmall, psmall_t = _in_proj(xs[s], norm_mix_w[l], m[0], m[1], w_main[l], w_small[l], geo, tm)
        hg_f, hg_b = _hgrn2_scan(p, lb[0, l], lb[1, l], geo, c_hg, r_scan)
        gl_f, gl_b = _gla_scan(p, psmall, gla_gate_w[:, l], gla_gate_b[:, l], geo, c_gla, r_scan)
        xbc = _ssd_conv(p, ssd_conv_w[l], ssd_conv_b[l], geo, tr)
        sd_f, sd_b = _ssd_scan(xbc, psmall, psmall_t, ssd_a_log[:, l], ssd_dt_bias[:, l], geo, c_ssd, r_scan)
        xs[s] = _merge(xs[s], p, hg_f, hg_b, gl_f, gl_b, sd_f, sd_b, xbc, hg_norm_w[l], gla_norm_w[l],
                       ssd_d[l], ssd_norm_w[l], w_hg[l], w_gla[l], w_ssd[l], w_o[l], m[2], geo, tr)
        h2, experts, _, rank, wrow, counts = _router(xs[s], norm_ffn_w[l], m[3], m[4], router_w[l],
                                                     router_bias[l], geo, tm)
        slot, buf_t, block_e = _dispatch(experts, rank, counts, tm)
        xg = h2.at[buf_t].get(mode="promise_in_bounds")
        zero = jnp.minimum(counts[0, 0], 0.0)
        return dict(h2=h2, xg=xg, block_e=block_e, slot=slot, wrow=wrow, zero=zero)

    def expert_ffn(st, l, after=None):
        block_e = st["block_e"] if after is None else st["block_e"] + after.astype(jnp.int32)
        yg, tok = _experts(st["xg"], block_e, e_gate[l], e_up[l], e_down[l])
        yk = yg.at[st["slot"].reshape(-1)].get(mode="promise_in_bounds").reshape(TOP_K, -1, d)
        return dict(h2=st["h2"], yk=yk, wrow=st["wrow"], zero=tok[0, 0].astype(f32))

    def ffn_out(s, st, l, after=None):
        wrow = st["wrow"] if after is None else st["wrow"] + after
        xs[s] = _ffn_out(xs[s], st["h2"], st["yk"], wrow, s_gate[l], s_up[l], s_down[l], mods[s][l][5], geo,
                         tm)

    if n_streams == 1:
        for l in range(depth):
            ffn_out(0, expert_ffn(mix_and_route(0, l), l), l)
    else:
        pending = None
        for l in range(depth):
            ra = mix_and_route(0, l)
            if pending is not None:
                ffn_out(1, pending, l - 1, after=ra["zero"])
            rb = mix_and_route(1, l)
            ca = expert_ffn(ra, l, after=rb["zero"])
            cb = expert_ffn(rb, l)
            ffn_out(0, ca, l, after=cb["zero"])
            pending = cb
        ffn_out(1, pending, depth - 1)

    out = [_final_norm(xs[s], final_norm_w, nctx, tm).reshape(nb, seq, d) for s in range(n_streams)]
    return jnp.concatenate(out, axis=0)
```

```python
import functools
import math

import numpy as np
import jax
import jax.numpy as jnp
from jax import lax
from jax.experimental import pallas as pl
from jax.experimental.pallas import tpu as pltpu

f32 = jnp.float32
bf16 = jnp.bfloat16
HIGHEST = lax.Precision.HIGHEST

EPS = 1e-6
POS_BASE = 10000.0
GRID_W = 64
N_MOD = 6
HG_HEADS = 4
HG_DK = 128
HG_WIDTH = HG_HEADS * HG_DK
GLA_HEADS = 4
GLA_DK = 64
GLA_DV = 128
GLA_RANK = 16
GLA_TAU = 16.0
SSD_HEADS = 8
SSD_HEADDIM = 64
SSD_INNER = SSD_HEADS * SSD_HEADDIM
SSD_GROUPS = 2
SSD_STATE = 128
SSD_CONV = 5
N_BRANCH = 3
N_EXPERTS = 64
TOP_K = 8
N_GROUPS = 8
TOPK_GROUPS = 4
ROUTE_SCALE = 2.5
MOE_BLOCK = 256
NEG_MASK = -1e4

LANES = 128
SUBLANES = 8
VMEM_BYTES = 64 * 1024 * 1024
VMEM_LIMIT = VMEM_BYTES * 7 // 8
EXP_CLAMP = 80.0

P_GATE = 0
P_HG = 3072
P_GLA = 5632
P_SSD = 7168
P_TOTAL = 8704
S_LRF, S_LRB, S_DTF, S_DTB = 0, 16, 32, 40


def _cparams(sem):
    return pltpu.CompilerParams(dimension_semantics=sem, vmem_limit_bytes=VMEM_LIMIT)


def _split_hi_lo(x):
    hi = x.astype(bf16)
    lo = (x - hi.astype(f32)).astype(bf16)
    return hi, lo


def _dot(a, b):
    return jnp.dot(a, b, preferred_element_type=f32)


def _dot_nt(a, b):
    return lax.dot_general(a, b, (((1,), (1,)), ((), ())), preferred_element_type=f32)


def _dot_tn(a, b):
    return lax.dot_general(a, b, (((0,), (0,)), ((), ())), preferred_element_type=f32)


def _dot2(m, x):
    hi, lo = _split_hi_lo(x)
    return _dot(m, hi) + _dot(m, lo)


def _dot2_l(x, m):
    hi, lo = _split_hi_lo(x)
    return _dot(hi, m) + _dot(lo, m)


def _silu(x):
    return x * jax.nn.sigmoid(x)


def _log_sigmoid(z):
    return jnp.minimum(z, 0.0) - jnp.log(1.0 + jnp.exp(-jnp.abs(z)))


def _softplus(z):
    return jnp.maximum(z, 0.0) + jnp.log(1.0 + jnp.exp(-jnp.abs(z)))


def _scan_mask(c, rev):
    r = lax.broadcasted_iota(jnp.int32, (c, c), 0)
    s = lax.broadcasted_iota(jnp.int32, (c, c), 1)
    return (r <= s) if rev else (r >= s)


def _mods_kernel(c_ref, w_ref, b_ref, o_ref):
    o_ref[...] = jnp.dot(_silu(c_ref[...]), w_ref[...], precision=HIGHEST,
                         preferred_element_type=f32) + b_ref[...]


def _mods(c_all, ada_w, ada_b):
    depth, d, n = ada_w.shape
    tn = 1024
    return pl.pallas_call(
        _mods_kernel,
        grid=(depth, n // tn),
        in_specs=[pl.BlockSpec((SUBLANES, d), lambda l, j: (0, 0)),
                  pl.BlockSpec((None, d, tn), lambda l, j: (l, 0, j)),
                  pl.BlockSpec((None, 1, tn), lambda l, j: (l, 0, j))],
        out_specs=pl.BlockSpec((None, SUBLANES, tn), lambda l, j: (l, 0, j)),
        out_shape=jax.ShapeDtypeStruct((depth, SUBLANES, n), f32),
        compiler_params=_cparams(("parallel", "parallel")),
        name="mods",
    )(c_all, ada_w, ada_b.reshape(depth, 1, n))


def _embed_kernel(ctx_ref, x_ref, pe_ref, o_ref, *, nct):
    i = pl.program_id(0)

    @pl.when(i < nct)
    def _():
        o_ref[...] = ctx_ref[...]

    @pl.when(i >= nct)
    def _():
        o_ref[...] = x_ref[...] + pe_ref[...]


def _embed(ctx2, x2, pe, te):
    nctx, d = ctx2.shape
    nlat = x2.shape[0]
    nct, npe = nctx // te, pe.shape[0] // te
    return pl.pallas_call(
        functools.partial(_embed_kernel, nct=nct),
        grid=((nctx + nlat) // te,),
        in_specs=[pl.BlockSpec((te, d), lambda i: (jnp.minimum(i, nct - 1), 0)),
                  pl.BlockSpec((te, d), lambda i: (jnp.maximum(i - nct, 0), 0)),
                  pl.BlockSpec((te, d), lambda i: (jnp.maximum(i - nct, 0) % npe, 0))],
        out_specs=pl.BlockSpec((te, d), lambda i: (i, 0)),
        out_shape=jax.ShapeDtypeStruct((nctx + nlat, d), f32),
        compiler_params=_cparams(("parallel",)),
        name="embed",
    )(ctx2, x2, pe)


def _norm_rows(x, w):
    return x * lax.rsqrt(jnp.mean(x * x, axis=-1, keepdims=True) + EPS) * w


def _beats(a, b, a_first):
    return (a >= b) if a_first else (a > b)


def _router_kernel(x_ref, w_ref, sh_ref, sc_ref, rwt_ref, rb_ref,
                   h_ref, e_ref, rank_ref, wrow_ref, cnt_ref, wt_ref, *, n_tiles):
    @pl.when(pl.program_id(0) == 0)
    def _():
        cnt_ref[...] = jnp.zeros_like(cnt_ref)

    @pl.when(pl.program_id(0) >= n_tiles)
    def _():
        h_ref[...] = jnp.zeros_like(h_ref)

    @pl.when(pl.program_id(0) < n_tiles)
    def _():
        _route_tile(x_ref, w_ref, sh_ref, sc_ref, rwt_ref, rb_ref,
                    h_ref, e_ref, rank_ref, wrow_ref, cnt_ref, wt_ref)


def _route_tile(x_ref, w_ref, sh_ref, sc_ref, rwt_ref, rb_ref,
                h_ref, e_ref, rank_ref, wrow_ref, cnt_ref, wt_ref):
    y = _norm_rows(x_ref[...], w_ref[...])
    h = y * (1.0 + sc_ref[...]) + sh_ref[...]
    h_ref[...] = h.astype(bf16)
    tm = h.shape[0]
    logits = lax.dot_general(rwt_ref[...], h, (((1,), (1,)), ((), ())), precision=HIGHEST,
                             preferred_element_type=f32)
    scores = jax.nn.sigmoid(logits)
    biased = scores + rb_ref[...]

    gsz = N_EXPERTS // N_GROUPS
    miota = lax.broadcasted_iota(jnp.int32, (gsz, tm), 0)
    blocks, gscore = [], []
    for g in range(N_GROUPS):
        blk = biased[g * gsz:(g + 1) * gsz, :]
        m1 = jnp.max(blk, axis=0, keepdims=True)
        i1 = jnp.min(jnp.where(blk == m1, miota, gsz), axis=0, keepdims=True)
        m2 = jnp.max(jnp.where(miota == i1, -jnp.inf, blk), axis=0, keepdims=True)
        blocks.append(blk)
        gscore.append(m1 + m2)
    vals = []
    for g in range(N_GROUPS):
        ahead = jnp.zeros((1, tm), jnp.int32)
        for o in range(N_GROUPS):
            if o != g:
                ahead = ahead + _beats(gscore[o], gscore[g], o < g).astype(jnp.int32)
        vals.append(jnp.where(ahead < TOPK_GROUPS, blocks[g], NEG_MASK))
    vals = jnp.concatenate(vals, axis=0)

    eiota = lax.broadcasted_iota(jnp.int32, (N_EXPERTS, tm), 0)
    member = jnp.zeros((N_EXPERTS, tm), f32)
    chosen, weights = [], []
    for k in range(TOP_K):
        m = jnp.max(vals, axis=0, keepdims=True)
        ei = jnp.min(jnp.where(vals == m, eiota, N_EXPERTS), axis=0, keepdims=True)
        sel = eiota == ei
        chosen.append(ei)
        weights.append(jnp.sum(jnp.where(sel, scores, 0.0), axis=0, keepdims=True))
        member = member + sel.astype(f32)
        vals = jnp.where(sel, -jnp.inf, vals)
    wsum = weights[0]
    for k in range(1, TOP_K):
        wsum = wsum + weights[k]

    r = lax.broadcasted_iota(jnp.int32, (tm, tm), 0)
    s = lax.broadcasted_iota(jnp.int32, (tm, tm), 1)
    member_bf = member.astype(bf16)
    rank_all = _dot(member_bf, (r < s).astype(bf16)) + cnt_ref[:, 0:1]
    for k in range(TOP_K):
        e_ref[k:k + 1, :] = chosen[k]
        wt_ref[k:k + 1, :] = weights[k] / wsum * ROUTE_SCALE
        rank_ref[k:k + 1, :] = jnp.sum(jnp.where(eiota == chosen[k], rank_all, 0.0), axis=0,
                                       keepdims=True).astype(jnp.int32)
    cnt_ref[...] = cnt_ref[...] + _dot(member_bf, jnp.ones((tm, LANES), bf16))
    eye = (lax.broadcasted_iota(jnp.int32, (TOP_K, LANES), 0)
           == lax.broadcasted_iota(jnp.int32, (TOP_K, LANES), 1)).astype(bf16)
    w_hi, w_lo = _split_hi_lo(wt_ref[...])
    wrow_ref[...] = _dot_tn(w_hi, eye) + _dot_tn(w_lo, eye)


def _mod_block(d, tm, geo):
    nctx, seq, nb = geo
    return pl.BlockSpec((None, 1, d), lambda i: (jnp.where(i * tm < nctx, nb, (i * tm - nctx) // seq), 0, 0))


def _router(xs, w, shift, scale, router_w, router_bias, geo, tm):
    t, d = xs.shape
    ne = router_w.shape[1]
    nt = t // tm
    nt_table = max(nt, -(-VMEM_BYTES // (tm * d * 2)))
    last = nt - 1
    nctx, seq, nb = geo

    def mod(i):
        r0 = jnp.minimum(i, last) * tm
        return (jnp.where(r0 < nctx, nb, (r0 - nctx) // seq), 0, 0)

    kt = pl.BlockSpec((TOP_K, tm), lambda i: (0, jnp.minimum(i, last)))
    return pl.pallas_call(
        functools.partial(_router_kernel, n_tiles=nt), grid=(nt_table,),
        in_specs=[pl.BlockSpec((tm, d), lambda i: (jnp.minimum(i, last), 0)),
                  pl.BlockSpec((1, d), lambda i: (0, 0)),
                  pl.BlockSpec((None, 1, d), mod), pl.BlockSpec((None, 1, d), mod),
                  pl.BlockSpec((ne, d), lambda i: (0, 0)),
                  pl.BlockSpec((ne, 1), lambda i: (0, 0))],
        out_specs=[pl.BlockSpec((tm, d), lambda i: (i, 0)), kt, kt,
                   pl.BlockSpec((tm, LANES), lambda i: (jnp.minimum(i, last), 0)),
                   pl.BlockSpec((ne, LANES), lambda i: (0, 0))],
        out_shape=[jax.ShapeDtypeStruct((nt_table * tm, d), bf16),
                   jax.ShapeDtypeStruct((TOP_K, t), jnp.int32),
                   jax.ShapeDtypeStruct((TOP_K, t), jnp.int32),
                   jax.ShapeDtypeStruct((t, LANES), f32),
                   jax.ShapeDtypeStruct((ne, LANES), f32)],
        scratch_shapes=[pltpu.VMEM((TOP_K, tm), f32)],
        compiler_params=_cparams(("arbitrary",)), name="router",
    )(xs, w.reshape(1, d), shift, scale, router_w.T, router_bias.astype(f32).reshape(ne, 1))


PROJ_TN = 512


def _in_proj_kernel(x_ref, nw_ref, sh_ref, sc_ref, w_ref, ws_ref, wst_ref, p_ref, ps_ref, pst_ref):
    y = _norm_rows(x_ref[...], nw_ref[...])
    h = (y * (1.0 + sc_ref[...]) + sh_ref[...]).astype(bf16)
    for j in range(p_ref.shape[1] // PROJ_TN):
        sl = slice(j * PROJ_TN, (j + 1) * PROJ_TN)
        p_ref[:, sl] = _dot(h, w_ref[:, sl]).astype(p_ref.dtype)
    ps_ref[...] = _dot(h, ws_ref[...])
    pst_ref[...] = _dot_nt(wst_ref[...], h)


def _in_proj(xs, nw, shift, scale, w_main, w_small, geo, tm):
    t, d = xs.shape
    n = w_main.shape[1]
    ns = w_small.shape[1]

    def resident(shape):
        return pl.BlockSpec(shape, lambda i: (0, 0), pipeline_mode=pl.Buffered(1))

    return pl.pallas_call(
        _in_proj_kernel, grid=(t // tm,),
        in_specs=[pl.BlockSpec((tm, d), lambda i: (i, 0)),
                  pl.BlockSpec((1, d), lambda i: (0, 0)),
                  _mod_block(d, tm, geo), _mod_block(d, tm, geo),
                  resident((d, n)), resident((d, ns)), resident((ns, d))],
        out_specs=[pl.BlockSpec((tm, n), lambda i: (i, 0)),
                   pl.BlockSpec((tm, ns), lambda i: (i, 0)),
                   pl.BlockSpec((ns, tm), lambda i: (0, i))],
        out_shape=[jax.ShapeDtypeStruct((t, n), bf16),
                   jax.ShapeDtypeStruct((t, ns), f32),
                   jax.ShapeDtypeStruct((ns, t), f32)],
        compiler_params=_cparams(("parallel",)), name="in_proj",
        cost_estimate=pl.CostEstimate(flops=2 * t * d * (n + 2 * ns), transcendentals=t,
                                      bytes_accessed=t * (4 * d + 2 * n + 8 * ns) + 2 * d * (n + 2 * ns)),
    )(xs, nw.reshape(1, d), shift, scale, w_main, w_small, w_small.T)


def _chunk_blocks(geo, c, rev):
    nctx, seq, nb = geo
    lc = nctx // nb
    ncc, nlc = lc // c, seq // c

    def blk(b, i):
        j_ctx = (ncc - 1 - i) if rev else i
        j_lat = (nlc - 1 - (i - ncc)) if rev else (i - ncc)
        return jnp.where(i < ncc, b * ncc + j_ctx, nctx // c + b * nlc + j_lat)

    return blk, ncc + nlc


def _sub_chunks(n_rows, c, rev):
    n = n_rows // c
    order = range(n - 1, -1, -1) if rev else range(n)
    return [slice(j * c, (j + 1) * c) for j in order]


def _decay_terms(q, k, g, mask, rev):
    c = q.shape[0]
    b = _dot2(mask.astype(bf16), g)
    last = 0 if rev else c - 1
    b_tot = b[last:last + 1, :]
    b_mid = b[c // 2:c // 2 + 1, :]
    rel = b - b_mid
    qd = q * jnp.exp(jnp.minimum(rel, EXP_CLAMP))
    kd = k * jnp.exp(jnp.minimum(-rel, EXP_CLAMP))
    qe = qd * jnp.exp(b_mid)
    ke = kd * jnp.exp(b_tot - b_mid)
    return qd.astype(bf16), kd.astype(bf16), qe.astype(bf16), ke.astype(bf16), jnp.exp(b_tot)


def _hgrn2_gates(z, lb):
    e = jnp.exp(-jnp.abs(z))
    inv = 1.0 / (1.0 + e)
    sig_pos = jnp.where(z >= 0, 1.0, e) * inv
    sig_neg = jnp.where(z >= 0, e, 1.0) * inv
    ls = jnp.minimum(z, 0.0) - jnp.log(1.0 + e)
    log_f = jnp.where(lb > 0.0, jnp.log(jnp.where(lb > 0.0, lb + (1.0 - lb) * sig_pos, 1.0)), ls)
    return (1.0 - lb) * sig_neg, log_f


def _hgrn2_kernel(qf_ref, zf_ref, vf_ref, qb_ref, zb_ref, vb_ref, lbf_ref, lbb_ref,
                  of_ref, ob_ref, sf_ref, sb_ref, *, c):
    @pl.when(pl.program_id(1) == 0)
    def _():
        sf_ref[...] = jnp.zeros_like(sf_ref)
        sb_ref[...] = jnp.zeros_like(sb_ref)

    for rev, q_ref, z_ref, v_ref, lb_ref, o_ref, s_ref in (
            (False, qf_ref, zf_ref, vf_ref, lbf_ref, of_ref, sf_ref),
            (True, qb_ref, zb_ref, vb_ref, lbb_ref, ob_ref, sb_ref)):
        mask = _scan_mask(c, rev)
        for rows in _sub_chunks(q_ref.shape[0], c, rev):
            q = q_ref[rows, :].astype(f32)
            k, g = _hgrn2_gates(z_ref[rows, :].astype(f32), lb_ref[...])
            v = v_ref[rows, :]
            qd, kd, qe, ke, dtot = _decay_terms(q, k, g, mask, rev)
            for h in range(HG_HEADS):
                sl = slice(h * HG_DK, (h + 1) * HG_DK)
                st = s_ref[h]
                att = jnp.where(mask, _dot_nt(qd[:, sl], kd[:, sl]), 0.0)
                o = _dot(att.astype(bf16), v[:, sl]) + _dot_nt(qe[:, sl], st.astype(bf16))
                o_ref[rows, sl] = o.astype(o_ref.dtype)
                s_ref[h] = st * dtot[:, sl] + _dot_tn(v[:, sl], ke[:, sl])


def _hgrn2_scan(p, lb_f, lb_b, geo, c, r):
    t = p.shape[0]
    nb = geo[2]
    blk_f, steps = _chunk_blocks(geo, r, False)
    blk_b, _ = _chunk_blocks(geo, r, True)
    w = HG_WIDTH
    c0 = P_HG // w

    def spec(blk, col):
        return pl.BlockSpec((r, w), lambda b, i: (blk(b, i), col))

    vec = pl.BlockSpec((1, w), lambda b, i: (0, 0))
    return pl.pallas_call(
        functools.partial(_hgrn2_kernel, c=c), grid=(nb, steps),
        in_specs=[spec(blk_f, c0), spec(blk_f, c0 + 1), spec(blk_f, c0 + 3),
                  spec(blk_b, c0), spec(blk_b, c0 + 2), spec(blk_b, c0 + 3), vec, vec],
        out_specs=[spec(blk_f, 0), spec(blk_b, 0)],
        out_shape=[jax.ShapeDtypeStruct((t, w), bf16)] * 2,
        scratch_shapes=[pltpu.VMEM((HG_HEADS, HG_DK, HG_DK), f32)] * 2,
        compiler_params=_cparams(("parallel", "arbitrary")), name="hgrn2_scan",
    )(p, p, p, p, p, p, lb_f.reshape(1, w), lb_b.reshape(1, w))


def _gla_kernel(qf_ref, kf_ref, vf_ref, sf_in_ref, qb_ref, kb_ref, vb_ref, sb_in_ref,
                gwf_ref, gwb_ref, gbf_ref, gbb_ref, of_ref, ob_ref, sf_ref, sb_ref, *, c):
    @pl.when(pl.program_id(1) == 0)
    def _():
        sf_ref[...] = jnp.zeros_like(sf_ref)
        sb_ref[...] = jnp.zeros_like(sb_ref)

    npair = GLA_HEADS // 2
    lane = lax.broadcasted_iota(jnp.int32, (1, LANES), 1)
    vrow = lax.broadcasted_iota(jnp.int32, (2 * GLA_DV, LANES), 0)
    vcol = lax.broadcasted_iota(jnp.int32, (2 * GLA_DV, LANES), 1)
    block_diag = (vrow // GLA_DV) == (vcol // GLA_DK)
    for rev, q_ref, k_ref, v_ref, sm_ref, gw_ref, gb_ref, o_ref, s_ref in (
            (False, qf_ref, kf_ref, vf_ref, sf_in_ref, gwf_ref, gbf_ref, of_ref, sf_ref),
            (True, qb_ref, kb_ref, vb_ref, sb_in_ref, gwb_ref, gbb_ref, ob_ref, sb_ref)):
        mask = _scan_mask(c, rev)
        for rows in _sub_chunks(q_ref.shape[0], c, rev):
            q = q_ref[rows, :].astype(f32) * (GLA_DK ** -0.5)
            k = k_ref[rows, :].astype(f32)
            v = v_ref[rows, :]
            pre = _dot2_l(sm_ref[rows, :], gw_ref[...]) + gb_ref[...]
            g = _log_sigmoid(pre) / GLA_TAU
            qd, kd, qe, ke, dtot = _decay_terms(q, k, g, mask, rev)
            for p in range(npair):
                sl = slice(p * LANES, (p + 1) * LANES)
                vsl = slice(p * 2 * GLA_DV, (p + 1) * 2 * GLA_DV)
                st = s_ref[p]
                o_inter = _dot_nt(qe[:, sl], st.astype(bf16))
                for hh in range(2):
                    head = (lane // GLA_DK) == hh
                    qh = jnp.where(head, qd[:, sl], jnp.zeros_like(qd[:, sl]))
                    att = jnp.where(mask, _dot_nt(qh, kd[:, sl]), 0.0)
                    osl = slice((2 * p + hh) * GLA_DV, (2 * p + hh + 1) * GLA_DV)
                    o = _dot(att.astype(bf16), v[:, osl]) + o_inter[:, hh * GLA_DV:(hh + 1) * GLA_DV]
                    o_ref[rows, osl] = o.astype(o_ref.dtype)
                upd = st * dtot[:, sl] + _dot_tn(v[:, vsl], ke[:, sl])
                s_ref[p] = jnp.where(block_diag, upd, 0.0)


def _gla_scan(p, psmall, gate_w, gate_b, geo, c, r):
    t = p.shape[0]
    nb = geo[2]
    blk_f, steps = _chunk_blocks(geo, r, False)
    blk_b, _ = _chunk_blocks(geo, r, True)
    wk, wv = GLA_HEADS * GLA_DK, GLA_HEADS * GLA_DV

    def spec(blk, width, col):
        return pl.BlockSpec((r, width), lambda b, i: (blk(b, i), col))

    def const(shape):
        return pl.BlockSpec(shape, lambda b, i: (0, 0))

    gwf = jnp.zeros((LANES, wk), f32).at[S_LRF:S_LRF + GLA_RANK].set(gate_w[0]).astype(bf16)
    gwb = jnp.zeros((LANES, wk), f32).at[S_LRB:S_LRB + GLA_RANK].set(gate_w[1]).astype(bf16)
    qcol, kcol, vcol = P_GLA // wk, P_GLA // wk + 1, (P_GLA + 2 * wk) // wv
    return pl.pallas_call(
        functools.partial(_gla_kernel, c=c), grid=(nb, steps),
        in_specs=[spec(blk_f, wk, qcol), spec(blk_f, wk, kcol), spec(blk_f, wv, vcol), spec(blk_f, LANES, 0),
                  spec(blk_b, wk, qcol), spec(blk_b, wk, kcol), spec(blk_b, wv, vcol), spec(blk_b, LANES, 0),
                  const((LANES, wk)), const((LANES, wk)), const((1, wk)), const((1, wk))],
        out_specs=[spec(blk_f, wv, 0), spec(blk_b, wv, 0)],
        out_shape=[jax.ShapeDtypeStruct((t, wv), bf16)] * 2,
        scratch_shapes=[pltpu.VMEM((GLA_HEADS // 2, 2 * GLA_DV, 2 * GLA_DK), f32)] * 2,
        compiler_params=_cparams(("parallel", "arbitrary")), name="gla_scan",
    )(p, p, p, psmall, p, p, p, psmall, gwf, gwb, gate_b[0].reshape(1, wk), gate_b[1].reshape(1, wk))


HALO = SUBLANES


def _conv_kernel(prev_ref, cur_ref, next_ref, w_ref, b_ref, o_ref, ext_ref, *, geo, tc):
    nctx, seq, nb = geo
    r0 = pl.program_id(0) * tc
    in_ctx = r0 < nctx
    seg_len = jnp.where(in_ctx, nctx // nb, seq)
    pos = jnp.where(in_ctx, r0 % (nctx // nb), (r0 - nctx) % seq)
    first = pos == 0
    last = pos + tc == seg_len
    ext_ref[0:HALO, :] = jnp.where(first, 0.0, prev_ref[...].astype(f32))
    ext_ref[HALO:HALO + tc, :] = cur_ref[...].astype(f32)
    ext_ref[HALO + tc:HALO + tc + HALO, :] = jnp.where(last, 0.0, next_ref[...].astype(f32))
    acc = jnp.zeros(o_ref.shape, f32) + b_ref[...]
    for j in range(SSD_CONV):
        off = HALO - SSD_CONV // 2 + j
        acc = acc + ext_ref[off:off + tc, :] * w_ref[j:j + 1, :]
    o_ref[...] = _silu(acc).astype(o_ref.dtype)


def _ssd_conv(p, conv_w, conv_b, geo, tc):
    t = p.shape[0]
    nch = conv_w.shape[1]
    tw = 512
    col0 = (P_SSD + SSD_INNER) // tw
    hb = tc // HALO
    nhb = t // HALO
    return pl.pallas_call(
        functools.partial(_conv_kernel, geo=geo, tc=tc),
        grid=(t // tc, nch // tw),
        in_specs=[pl.BlockSpec((HALO, tw), lambda i, j: (jnp.maximum(i * hb - 1, 0), col0 + j)),
                  pl.BlockSpec((tc, tw), lambda i, j: (i, col0 + j)),
                  pl.BlockSpec((HALO, tw), lambda i, j: (jnp.minimum((i + 1) * hb, nhb - 1), col0 + j)),
                  pl.BlockSpec((SSD_CONV, tw), lambda i, j: (0, j)),
                  pl.BlockSpec((1, tw), lambda i, j: (0, j))],
        out_specs=pl.BlockSpec((tc, tw), lambda i, j: (i, j)),
        out_shape=jax.ShapeDtypeStruct((t, nch), bf16),
        scratch_shapes=[pltpu.VMEM((tc + 2 * HALO, tw), f32)],
        compiler_params=_cparams(("parallel", "parallel")), name="ssd_conv",
    )(p, p, p, conv_w, conv_b.reshape(1, nch))


def _ssd_kernel(xf_ref, bf_ref, cf_ref, smf_ref, smtf_ref, xb_ref, bb_ref, cb_ref, smb_ref, smtb_ref,
                arow_ref, brow_ref, acol_ref, bcol_ref, ef_ref, eb_ref,
                yf_ref, yb_ref, sf_ref, sb_ref, *, c):
    @pl.when(pl.program_id(1) == 0)
    def _():
        sf_ref[...] = jnp.zeros_like(sf_ref)
        sb_ref[...] = jnp.zeros_like(sb_ref)

    gw = SSD_INNER // SSD_GROUPS
    lane = lax.broadcasted_iota(jnp.int32, (1, LANES), 1)
    for d, rev, x_ref, bm_ref, cm_ref, sm_ref, smt_ref, e_ref, y_ref, s_ref, lane0 in (
            (0, False, xf_ref, bf_ref, cf_ref, smf_ref, smtf_ref, ef_ref, yf_ref, sf_ref, S_DTF),
            (1, True, xb_ref, bb_ref, cb_ref, smb_ref, smtb_ref, eb_ref, yb_ref, sb_ref, S_DTB)):
        mask = _scan_mask(c, rev)
        mask_bf = mask.astype(bf16)
        mask_t_bf = _scan_mask(c, not rev).astype(bf16)
        expand = e_ref[...]
        for rows in _sub_chunks(x_ref.shape[0], c, rev):
            dt = _softplus(sm_ref[rows, :] + brow_ref[d:d + 1, :])
            cum = _dot2(mask_bf, dt * arow_ref[d:d + 1, :])
            last = 0 if rev else c - 1
            cum_end = cum[last:last + 1, :]
            dt_t = _softplus(smt_ref[:, rows] + bcol_ref[d])
            cum_t = _dot2_l(dt_t * acol_ref[d], mask_t_bf)

            x = x_ref[rows, :].astype(f32)
            xdt = x * _dot2_l(dt, expand)
            w = (xdt * _dot2_l(jnp.exp(cum_end - cum), expand)).astype(bf16)
            xdt = xdt.astype(bf16)
            dec_t = _dot2_l(jnp.exp(cum), expand)
            dec_end = _dot2_l(jnp.broadcast_to(jnp.exp(cum_end), (SUBLANES, LANES)), expand)[0:1, :]
            bm = bm_ref[rows, :]
            cm = cm_ref[rows, :]
            for g in range(SSD_GROUPS):
                gsl = slice(g * SSD_STATE, (g + 1) * SSD_STATE)
                hsl = slice(g * gw, (g + 1) * gw)
                st = s_ref[:, hsl]
                scores = _dot_nt(cm[:, gsl], bm[:, gsl])
                y_inter = _dot(cm[:, gsl], st.astype(bf16)) * dec_t[:, hsl]
                for tile in range(gw // LANES):
                    tsl = slice(g * gw + tile * LANES, g * gw + (tile + 1) * LANES)
                    xt = xdt[:, tsl]
                    acc = y_inter[:, tile * LANES:(tile + 1) * LANES]
                    for hh in range(LANES // SSD_HEADDIM):
                        h = (g * gw + tile * LANES) // SSD_HEADDIM + hh
                        diff = cum[:, lane0 + h:lane0 + h + 1] - cum_t[h:h + 1, :]
                        seg = jnp.where(mask, jnp.exp(jnp.minimum(diff, 0.0)), 0.0)
                        xh = jnp.where((lane // SSD_HEADDIM) == hh, xt, jnp.zeros_like(xt))
                        acc = acc + _dot((scores * seg).astype(bf16), xh)
                    y_ref[rows, tsl] = acc.astype(y_ref.dtype)
                s_ref[:, hsl] = st * dec_end[:, hsl] + _dot_tn(bm[:, gsl], w[:, hsl])


def _ssd_scan(xbc, psmall, psmall_t, a_log, dt_bias, geo, c, r):
    t = xbc.shape[0]
    nb = geo[2]
    blk_f, steps = _chunk_blocks(geo, r, False)
    blk_b, _ = _chunk_blocks(geo, r, True)
    wi, wn = SSD_INNER, SSD_GROUPS * SSD_STATE

    def spec(blk, width, col):
        return pl.BlockSpec((r, width), lambda b, i: (blk(b, i), col))

    def spec_t(blk, row):
        return pl.BlockSpec((SUBLANES, r), lambda b, i: (row, blk(b, i)))

    def const(shape):
        return pl.BlockSpec(shape, lambda b, i: (0,) * len(shape))

    a = -jnp.exp(a_log.astype(f32))
    arow = jnp.zeros((2, LANES), f32).at[0, S_DTF:S_DTF + SSD_HEADS].set(a[0]).at[1, S_DTB:S_DTB + SSD_HEADS].set(a[1])
    brow = jnp.zeros((2, LANES), f32).at[0, S_DTF:S_DTF + SSD_HEADS].set(dt_bias[0]).at[1, S_DTB:S_DTB + SSD_HEADS].set(dt_bias[1])
    head_of_lane = np.arange(wi) // SSD_HEADDIM
    ef = jnp.asarray((np.arange(LANES)[:, None] == S_DTF + head_of_lane[None, :]), bf16)
    eb = jnp.asarray((np.arange(LANES)[:, None] == S_DTB + head_of_lane[None, :]), bf16)
    xcol, bcol, ccol = 0, wi // wn, wi // wn + 1
    return pl.pallas_call(
        functools.partial(_ssd_kernel, c=c), grid=(nb, steps),
        in_specs=[spec(blk_f, wi, xcol), spec(blk_f, wn, bcol), spec(blk_f, wn, ccol), spec(blk_f, LANES, 0),
                  spec_t(blk_f, S_DTF // SUBLANES),
                  spec(blk_b, wi, xcol), spec(blk_b, wn, bcol), spec(blk_b, wn, ccol), spec(blk_b, LANES, 0),
                  spec_t(blk_b, S_DTB // SUBLANES),
                  const((2, LANES)), const((2, LANES)), const((2, SSD_HEADS, 1)), const((2, SSD_HEADS, 1)),
                  const((LANES, wi)), const((LANES, wi))],
        out_specs=[spec(blk_f, wi, 0), spec(blk_b, wi, 0)],
        out_shape=[jax.ShapeDtypeStruct((t, wi), bf16)] * 2,
        scratch_shapes=[pltpu.VMEM((SSD_STATE, wi), f32)] * 2,
        compiler_params=_cparams(("parallel", "arbitrary")), name="ssd_scan",
    )(xbc, xbc, xbc, psmall, psmall_t, xbc, xbc, xbc, psmall, psmall_t,
      arow, brow, a.reshape(2, SSD_HEADS, 1), dt_bias.astype(f32).reshape(2, SSD_HEADS, 1), ef, eb)


def _group_norm(o, w, width):
    parts = []
    for j in range(o.shape[1] // width):
        blk = o[:, j * width:(j + 1) * width]
        parts.append(blk * lax.rsqrt(jnp.mean(blk * blk, axis=-1, keepdims=True) + EPS))
    return jnp.concatenate(parts, axis=1) * w


def _merge_kernel(x_ref, hgf_ref, hgb_ref, hgg_ref, glf_ref, glb_ref, glg_ref,
                  sdf_ref, sdb_ref, sdx_ref, sdz_ref, gate_ref,
                  hgw_ref, glw_ref, sdd_ref, sdw_ref, wh_ref, wg_ref, ws_ref, wo_ref, m2_ref, o_ref):
    o_hg = _group_norm(hgf_ref[...].astype(f32) + hgb_ref[...].astype(f32), hgw_ref[...], HG_DK)
    o_hg = o_hg * _silu(hgg_ref[...].astype(f32))
    o_gl = _group_norm(glf_ref[...].astype(f32) + glb_ref[...].astype(f32), glw_ref[...], GLA_DV)
    o_gl = o_gl * _silu(glg_ref[...].astype(f32))
    y = sdf_ref[...].astype(f32) + sdb_ref[...].astype(f32) + sdd_ref[...] * sdx_ref[...].astype(f32)
    o_sd = _group_norm(y * _silu(sdz_ref[...].astype(f32)), sdw_ref[...], SSD_INNER // SSD_GROUPS)
    d = x_ref.shape[1]
    gate = jax.nn.sigmoid(gate_ref[...].astype(f32))
    y = (gate[:, 0:d] * _dot(o_hg.astype(bf16), wh_ref[...])
         + gate[:, d:2 * d] * _dot(o_gl.astype(bf16), wg_ref[...])
         + gate[:, 2 * d:3 * d] * _dot(o_sd.astype(bf16), ws_ref[...]))
    o_ref[...] = x_ref[...] + m2_ref[...] * _dot(y.astype(bf16), wo_ref[...])


def _merge(xs, p, hg_f, hg_b, gl_f, gl_b, sd_f, sd_b, xbc, hg_norm_w, gla_norm_w, ssd_d, ssd_norm_w,
           w_hg, w_gla, w_ssd, w_out, mod2, geo, tm):
    t, d = xs.shape
    w = 512

    def rows(width, col):
        return pl.BlockSpec((tm, width), lambda i: (i, col))

    def const(shape):
        return pl.BlockSpec(shape, lambda i: (0, 0))

    tile4 = lambda v: jnp.tile(v, w // v.shape[0]).reshape(1, w)
    return pl.pallas_call(
        _merge_kernel, grid=(t // tm,),
        in_specs=[rows(d, 0),
                  rows(w, 0), rows(w, 0), rows(w, (P_HG + 4 * w) // w),
                  rows(w, 0), rows(w, 0), rows(w, (P_GLA + 2 * w) // w),
                  rows(w, 0), rows(w, 0), rows(w, 0), rows(w, P_SSD // w),
                  rows(3 * d, P_GATE // (3 * d)),
                  const((1, w)), const((1, w)), const((1, w)), const((1, w)),
                  const((w, d)), const((w, d)), const((w, d)), const((d, d)),
                  _mod_block(d, tm, geo)],
        out_specs=rows(d, 0),
        out_shape=jax.ShapeDtypeStruct((t, d), f32),
        compiler_params=_cparams(("parallel",)), name="merge",
        cost_estimate=pl.CostEstimate(flops=2 * t * d * (3 * w + d), transcendentals=t * (3 * w + 3 * d),
                                      bytes_accessed=t * (8 * d + 2 * 11 * w + 6 * d) + 2 * d * (3 * w + d)),
    )(xs, hg_f, hg_b, p, gl_f, gl_b, p, sd_f, sd_b, xbc, p, p,
      tile4(hg_norm_w), tile4(gla_norm_w), jnp.repeat(ssd_d, SSD_HEADDIM).reshape(1, w), ssd_norm_w.reshape(1, w),
      w_hg, w_gla, w_ssd, w_out, mod2)


MOE_SUB = 4


def _expert_kernel(be_ref, x_ref, *refs):
    o_ref, tok_ref = refs[-2:]
    first = pl.program_id(0) * MOE_SUB
    same = be_ref[first + 1] == be_ref[first]
    for s in range(2, MOE_SUB):
        same = jnp.logical_and(same, be_ref[first + s] == be_ref[first])

    def ffn(x, wg_ref, wu_ref, wd_ref):
        a = _silu(_dot(x, wg_ref[...])) * _dot(x, wu_ref[...])
        return _dot(a.astype(bf16), wd_ref[...]).astype(o_ref.dtype)

    tok_ref[...] = jnp.zeros_like(tok_ref)

    @pl.when(same)
    def _():
        o_ref[...] = ffn(x_ref[...], *refs[0:3])

    @pl.when(jnp.logical_not(same))
    def _():
        for s in range(MOE_SUB):
            rows = slice(s * MOE_BLOCK, (s + 1) * MOE_BLOCK)
            o_ref[rows, :] = ffn(x_ref[rows, :], *refs[3 * s:3 * s + 3])


def _experts(xg, block_e, w_gate, w_up, w_down):
    npad, d = xg.shape
    ff = w_gate.shape[2]
    rows = MOE_SUB * MOE_BLOCK
    w_specs = []
    for s in range(MOE_SUB):
        pick = lambda i, be, s=s: (be[i * MOE_SUB + s], 0, 0)
        w_specs += [pl.BlockSpec((None, d, ff), pick), pl.BlockSpec((None, d, ff), pick),
                    pl.BlockSpec((None, ff, d), pick)]
    grid_spec = pltpu.PrefetchScalarGridSpec(
        num_scalar_prefetch=1, grid=(npad // rows,),
        in_specs=[pl.BlockSpec((rows, d), lambda i, be: (i, 0))] + w_specs,
        out_specs=[pl.BlockSpec((rows, d), lambda i, be: (i, 0)),
                   pl.BlockSpec((SUBLANES, LANES), lambda i, be: (0, 0))])
    return pl.pallas_call(
        _expert_kernel, grid_spec=grid_spec,
        out_shape=[jax.ShapeDtypeStruct((npad, d), bf16),
                   jax.ShapeDtypeStruct((SUBLANES, LANES), jnp.int32)],
        compiler_params=_cparams(("arbitrary",)), name="experts",
        cost_estimate=pl.CostEstimate(flops=6 * npad * d * ff, transcendentals=npad * ff,
                                      bytes_accessed=4 * npad * d + 6 * N_EXPERTS * d * ff),
    )(block_e, xg, *([w_gate, w_up, w_down] * MOE_SUB))


def _cast_kernel(x_ref, o_ref):
    o_ref[...] = x_ref[...].astype(o_ref.dtype)


def _cast_expert_weights(w, group):
    depth, ne, a, b = w.shape
    spec = pl.BlockSpec((None, group, a, b), lambda l, j: (l, j, 0, 0))
    return pl.pallas_call(
        _cast_kernel, grid=(depth, ne // group), in_specs=[spec], out_specs=spec,
        out_shape=jax.ShapeDtypeStruct(w.shape, bf16),
        compiler_params=_cparams(("parallel", "parallel")), name="cast_experts",
    )(w)


def _slots_kernel(ps_ref, e_ref, r_ref, o_ref):
    e = e_ref[...]
    acc = r_ref[...]
    for j in range(N_EXPERTS):
        acc = acc + jnp.where(e == j, ps_ref[j], 0)
    o_ref[...] = acc


def _slots(experts, rank, pad_start, tl):
    k, t = experts.shape
    spec = pl.BlockSpec((k, tl), lambda i, ps: (0, i))
    return pl.pallas_call(
        _slots_kernel,
        grid_spec=pltpu.PrefetchScalarGridSpec(num_scalar_prefetch=1, grid=(t // tl,),
                                               in_specs=[spec, spec], out_specs=spec),
        out_shape=jax.ShapeDtypeStruct((k, t), jnp.int32),
        compiler_params=_cparams(("parallel",)), name="slots",
    )(pad_start, experts, rank)


def _ffn_out_kernel(x_ref, h_ref, y_ref, wr_ref, sg_ref, su_ref, sd_ref, m5_ref, o_ref):
    h = h_ref[...]
    a = _silu(_dot(h, sg_ref[...])) * _dot(h, su_ref[...])
    acc = _dot(a.astype(bf16), sd_ref[...])
    wr = wr_ref[...]
    for k in range(TOP_K):
        acc = acc + y_ref[k].astype(f32) * wr[:, k:k + 1]
    o_ref[...] = x_ref[...] + m5_ref[...] * acc


def _ffn_out(xs, h, yk, wrow, s_gate, s_up, s_down, mod5, geo, tm):
    t, d = xs.shape
    ff = s_gate.shape[1]
    rows = pl.BlockSpec((tm, d), lambda i: (i, 0))
    return pl.pallas_call(
        _ffn_out_kernel, grid=(t // tm,),
        in_specs=[rows, rows, pl.BlockSpec((TOP_K, tm, d), lambda i: (0, i, 0)),
                  pl.BlockSpec((tm, LANES), lambda i: (i, 0)),
                  pl.BlockSpec((d, ff), lambda i: (0, 0)), pl.BlockSpec((d, ff), lambda i: (0, 0)),
                  pl.BlockSpec((ff, d), lambda i: (0, 0)), _mod_block(d, tm, geo)],
        out_specs=rows,
        out_shape=jax.ShapeDtypeStruct((t, d), f32),
        compiler_params=_cparams(("parallel",)), name="ffn_out",
    )(xs, h, yk, wrow, s_gate, s_up, s_down, mod5)


def _dispatch(experts, rank, counts, tl):
    t = experts.shape[1]
    n_assign = t * TOP_K
    counts = counts[:, 0].astype(jnp.int32)
    padded = (counts + MOE_BLOCK - 1) // MOE_BLOCK * MOE_BLOCK
    pad_end = jnp.cumsum(padded)
    pad_start = (pad_end - padded).astype(jnp.int32)
    slot = _slots(experts, rank, pad_start, tl)
    n_blocks = (n_assign + MOE_BLOCK - 1) // MOE_BLOCK + N_EXPERTS
    n_blocks = (n_blocks + MOE_SUB - 1) // MOE_SUB * MOE_SUB
    token = jnp.broadcast_to(jnp.arange(t, dtype=jnp.int32), (TOP_K, t)).reshape(-1)
    slot_flat = slot.reshape(-1)
    filler = jnp.arange(n_blocks * MOE_BLOCK, dtype=jnp.int32) % t
    buf_t = filler.at[slot_flat].add(token - slot_flat % t, unique_indices=True)
    first_row = jnp.arange(n_blocks, dtype=jnp.int32) * MOE_BLOCK
    block_e = jnp.minimum(jnp.sum(pad_end[None, :] <= first_row[:, None], axis=1), N_EXPERTS - 1)
    return slot, buf_t, block_e.astype(jnp.int32)


def _final_norm_kernel(x_ref, w_ref, o_ref):
    o_ref[...] = _norm_rows(x_ref[...], w_ref[...])


def _final_norm(xs, w, nctx, tm):
    t, d = xs.shape
    off = nctx // tm
    return pl.pallas_call(
        _final_norm_kernel, grid=((t - nctx) // tm,),
        in_specs=[pl.BlockSpec((tm, d), lambda i: (i + off, 0)), pl.BlockSpec((1, d), lambda i: (0, 0))],
        out_specs=pl.BlockSpec((tm, d), lambda i: (i, 0)),
        out_shape=jax.ShapeDtypeStruct((t - nctx, d), f32),
        compiler_params=_cparams(("parallel",)), name="final_norm",
    )(xs, w.reshape(1, d))


def _grid_sincos(rows, cols, dim):
    quarter = dim // 4
    omega = 1.0 / (POS_BASE ** (jnp.arange(quarter, dtype=f32) / quarter))

    def axis_embed(n):
        ang = jnp.arange(n, dtype=f32)[:, None] * omega
        return jnp.concatenate([jnp.sin(ang), jnp.cos(ang)], axis=-1)

    er, ec = axis_embed(rows), axis_embed(cols)
    pe = jnp.concatenate([jnp.broadcast_to(er[:, None], (rows, cols, dim // 2)),
                          jnp.broadcast_to(ec[None], (rows, cols, dim // 2))], axis=-1)
    return pe.reshape(rows * cols, dim)


def _lower_bounds(logits):
    p = jax.nn.softmax(logits.astype(f32), axis=1)
    return jnp.cumsum(p, axis=1) - p[:, :1]


def _split_w_in(w_in):
    a0 = 5 * HG_WIDTH + 2 * GLA_HEADS * GLA_DK + 2 * GLA_HEADS * GLA_DV
    a1 = a0 + 2 * GLA_RANK
    b0 = a1 + 2 * SSD_INNER + 2 * SSD_GROUPS * SSD_STATE
    b1 = b0 + 2 * SSD_HEADS
    main = jnp.concatenate([w_in[..., b1:], w_in[..., :a0], w_in[..., a1:b0]], axis=-1).astype(bf16)
    small = jnp.concatenate([w_in[..., a0:a1], w_in[..., b0:b1]], axis=-1)
    small = jnp.pad(small, ((0, 0), (0, 0), (0, LANES - small.shape[-1]))).astype(bf16)
    return main, small


def kernel(x, c, ctx, c_ctx, norm_mix_w, norm_ffn_w, final_norm_w, ada_w, ada_b, w_in,
           hg_lb_logits, hg_norm_w, gla_gate_w, gla_gate_b, gla_norm_w, ssd_conv_w, ssd_conv_b,
           ssd_a_log, ssd_dt_bias, ssd_d, ssd_norm_w, w_br_hg, w_br_gla, w_br_ssd, w_out,
           router_w, router_bias, exp_w_gate, exp_w_up, exp_w_down, sh_w_gate, sh_w_up, sh_w_down):
    nb_all, seq, d = x.shape
    lc = ctx.shape[1]
    depth = w_in.shape[0]
    tm = min(512, lc)
    tr = min(256, lc)
    r_scan = min(256, lc)
    c_hg = min(64, lc)
    c_gla = min(128, lc)
    c_ssd = min(128, lc)

    n_streams = 2 if nb_all % 2 == 0 else 1
    nb = nb_all // n_streams
    nctx = nb * lc
    geo = (nctx, seq, nb)

    pe = _grid_sincos(seq // GRID_W, GRID_W, d)
    lb = _lower_bounds(hg_lb_logits)
    w_main, w_small = _split_w_in(w_in)
    w_hg, w_gla, w_ssd, w_o = (w.astype(bf16) for w in (w_br_hg, w_br_gla, w_br_ssd, w_out))
    e_gate, e_up, e_down = (_cast_expert_weights(w, 8) for w in (exp_w_gate, exp_w_up, exp_w_down))
    s_gate, s_up, s_down = (w.astype(bf16) for w in (sh_w_gate, sh_w_up, sh_w_down))

    mods, xs = [], []
    for s in range(n_streams):
        bsl = slice(s * nb, (s + 1) * nb)
        c_all = jnp.zeros((SUBLANES, d), f32).at[:nb].set(c[bsl]).at[nb].set(c_ctx)
        m = _mods(c_all, ada_w, ada_b)
        mods.append(m.reshape(depth, SUBLANES, N_MOD, 1, d).transpose(0, 2, 1, 3, 4))
        xs.append(_embed(ctx[bsl].reshape(nctx, d), x[bsl].reshape(nb * seq, d), pe, tm))

    def mix_and_route(s, l):
        m = mods[s][l]
        p, psmall, psmall_t = _in_proj(xs[s], norm_mix_w[l], m[0], m[1], w_main[l], w_small[l], geo, tm)
        hg_f, hg_b = _hgrn2_scan(p, lb[0, l], lb[1, l], geo, c_hg, r_scan)
        gl_f, gl_b = _gla_scan(p, psmall, gla_gate_w[:, l], gla_gate_b[:, l], geo, c_gla, r_scan)
        xbc = _ssd_conv(p, ssd_conv_w[l], ssd_conv_b[l], geo, tr)
        sd_f, sd_b = _ssd_scan(xbc, psmall, psmall_t, ssd_a_log[:, l], ssd_dt_bias[:, l], geo, c_ssd, r_scan)
        xs[s] = _merge(xs[s], p, hg_f, hg_b, gl_f, gl_b, sd_f, sd_b, xbc, hg_norm_w[l], gla_norm_w[l],
                       ssd_d[l], ssd_norm_w[l], w_hg[l], w_gla[l], w_ssd[l], w_o[l], m[2], geo, tm)
        h2, experts, rank, wrow, counts = _router(xs[s], norm_ffn_w[l], m[3], m[4], router_w[l],
                                                  router_bias[l], geo, tm)
        slot, buf_t, block_e = _dispatch(experts, rank, counts, tm)
        xg = h2.at[buf_t].get(mode="promise_in_bounds")
        zero = jnp.minimum(counts[0, 0], 0.0)
        return dict(h2=h2, xg=xg, block_e=block_e, slot=slot, wrow=wrow, zero=zero)

    def expert_ffn(st, l, after=None):
        block_e = st["block_e"] if after is None else st["block_e"] + after.astype(jnp.int32)
        yg, tok = _experts(st["xg"], block_e, e_gate[l], e_up[l], e_down[l])
        yk = yg.at[st["slot"].reshape(-1)].get(mode="promise_in_bounds").reshape(TOP_K, -1, d)
        return dict(h2=st["h2"], yk=yk, wrow=st["wrow"], zero=tok[0, 0].astype(f32))

    def ffn_out(s, st, l, after=None):
        wrow = st["wrow"] if after is None else st["wrow"] + after
        xs[s] = _ffn_out(xs[s], st["h2"], st["yk"], wrow, s_gate[l], s_up[l], s_down[l], mods[s][l][5], geo,
                         tm)

    if n_streams == 1:
        for l in range(depth):
            ffn_out(0, expert_ffn(mix_and_route(0, l), l), l)
    else:
        pending = None
        for l in range(depth):
            ra = mix_and_route(0, l)
            if pending is not None:
                ffn_out(1, pending, l - 1, after=ra["zero"])
            rb = mix_and_route(1, l)
            ca = expert_ffn(ra, l, after=rb["zero"])
            cb = expert_ffn(rb, l)
            ffn_out(0, ca, l, after=cb["zero"])
            pending = cb
        ffn_out(1, pending, depth - 1)

    out = [_final_norm(xs[s], final_norm_w, nctx, tm).reshape(nb, seq, d) for s in range(n_streams)]
    return jnp.concatenate(out, axis=0)
```
